```python
import jax, jax.numpy as jnp
from jax import lax
import numpy as np

D_MODEL = 2048
BATCH = 8
SEQ = 2048
DEPTH = 2

GRID_W = 64
CTX_LEN = 256
EPS = 1e-6
NEG = -1e30
N_EVEN = (DEPTH + 1) // 2
N_ODD = DEPTH // 2

NA_HEADS = 8
NA_HEAD_DIM = 128
NA_WIDTH = NA_HEADS * NA_HEAD_DIM
NA_WIN_H = 8
NA_WIN_W = 16
SG_GROUPS = 8
SG_WIDTH = D_MODEL // 2
SG_GROUP_DIM = SG_WIDTH // SG_GROUPS
SG_CHUNK = 128
AB_IN = 3 * NA_WIDTH + 2 * SG_WIDTH
AB_OUT = NA_WIDTH + SG_WIDTH
MLA_HEADS = 16
Q_LORA = 512
KV_LORA = 256
QK_NOPE = 128
QK_ROPE = 64
V_DIM = 128
QK_DIM = QK_NOPE + QK_ROPE
MLA_IN = Q_LORA + KV_LORA + QK_ROPE
Q_BLOCK = 128
ROPE_THETA = 10000.0
N_EXPERTS = 32
TOP_K = 4
D_FF = D_MODEL
SWIGLU_ALPHA = 1.702
SWIGLU_LIMIT = 7.0
MOE_BLOCK = 128

kernel_name = "hybrid_na_gmlp_mla_moe_diffusion_trunk"


def rms_norm(x, g):
    xf = x.astype(jnp.float32)
    y = xf * lax.rsqrt(jnp.mean(xf * xf, axis=-1, keepdims=True) + EPS)
    return (y * g.astype(jnp.float32)).astype(x.dtype)


def layer_norm(x, g, b):
    xf = x.astype(jnp.float32)
    mu = jnp.mean(xf, axis=-1, keepdims=True)
    var = jnp.mean(jnp.square(xf - mu), axis=-1, keepdims=True)
    y = (xf - mu) * lax.rsqrt(var + EPS) * g.astype(jnp.float32) + b.astype(jnp.float32)
    return y.astype(x.dtype)


def modulate(x, shift, scale):
    return x * (1 + scale) + shift


def axial_rope_tables(n_tokens, dim):
    t = jnp.arange(n_tokens)
    row = (t // GRID_W).astype(jnp.float32)
    col = (t % GRID_W).astype(jnp.float32)
    n_freq = dim // 4
    inv = ROPE_THETA ** (-jnp.arange(n_freq, dtype=jnp.float32) / n_freq)
    ang = jnp.concatenate([row[:, None] * inv, col[:, None] * inv], axis=-1)
    return jnp.cos(ang), jnp.sin(ang)


def apply_rope(x, cos, sin):
    x1, x2 = jnp.split(x, 2, axis=-1)
    c = cos[:, None, :].astype(x.dtype)
    s = sin[:, None, :].astype(x.dtype)
    return jnp.concatenate([x1 * c - x2 * s, x1 * s + x2 * c], axis=-1)


def dense_attention(q, k, v):
    s = jnp.einsum('bqhd,bkhd->bhqk', q, k).astype(jnp.float32) * (q.shape[-1] ** -0.5)
    p = jax.nn.softmax(s, axis=-1)
    return jnp.einsum('bhqk,bkhd->bqhd', p.astype(v.dtype), v)


def blocked_attention(q, k, v):
    B, S, H, dq = q.shape
    n_blk = S // Q_BLOCK
    qb = jnp.moveaxis(q.reshape(B, n_blk, Q_BLOCK, H, dq), 1, 0)
    out = lax.map(lambda qi: dense_attention(qi, k, v), qb)
    return jnp.moveaxis(out, 0, 1).reshape(B, S, H, v.shape[-1])


def neighbourhood_attention(q, k, v, kc, vc, rel_bias):
    B, S, H, d = q.shape
    rows = S // GRID_W
    kh = min(NA_WIN_H, rows)
    kw = NA_WIN_W
    scale = d ** -0.5
    kg = k.reshape(B, rows, GRID_W, H, d)
    vg = v.reshape(B, rows, GRID_W, H, d)
    col = jnp.arange(GRID_W)
    col_start = jnp.clip(col - kw // 2, 0, GRID_W - kw)
    col_mask = (col[None, :] >= col_start[:, None]) & (col[None, :] < col_start[:, None] + kw)
    mask = jnp.broadcast_to(col_mask[:, None, :], (GRID_W, kh, GRID_W)).reshape(GRID_W, kh * GRID_W)
    col_idx = jnp.clip(col[None, :] - col[:, None] + NA_WIN_W - 1, 0, 2 * NA_WIN_W - 2)

    def row_block(args):
        r, q_r = args
        r0 = jnp.clip(r - kh // 2, 0, rows - kh)
        k_band = lax.dynamic_slice_in_dim(kg, r0, kh, axis=1).reshape(B, kh * GRID_W, H, d)
        v_band = lax.dynamic_slice_in_dim(vg, r0, kh, axis=1).reshape(B, kh * GRID_W, H, d)
        row_idx = r0 + jnp.arange(kh) - r + NA_WIN_H - 1
        bias = rel_bias[:, row_idx[:, None, None], col_idx[None, :, :]]
        bias = jnp.transpose(bias, (0, 2, 1, 3)).reshape(H, GRID_W, kh * GRID_W).astype(jnp.float32)
        s_loc = jnp.einsum('bqhd,bkhd->bhqk', q_r, k_band).astype(jnp.float32) * scale + bias
        s_loc = jnp.where(mask, s_loc, NEG)
        s_ctx = jnp.einsum('bqhd,bchd->bhqc', q_r, kc).astype(jnp.float32) * scale
        p = jax.nn.softmax(jnp.concatenate([s_loc, s_ctx], axis=-1), axis=-1).astype(v.dtype)
        n_loc = kh * GRID_W
        return (jnp.einsum('bhqk,bkhd->bqhd', p[..., :n_loc], v_band)
                + jnp.einsum('bhqc,bchd->bqhd', p[..., n_loc:], vc))

    q_rows = jnp.moveaxis(q.reshape(B, rows, GRID_W, H, d), 1, 0)
    out = lax.map(row_block, (jnp.arange(rows), q_rows))
    return jnp.moveaxis(out, 0, 1).reshape(B, S, H, d)


def spatial_gating(u, z, w_s, b_s):
    B, T, _ = u.shape
    n = T // SG_CHUNK
    zg = z.reshape(B, n, SG_CHUNK, SG_GROUPS, SG_GROUP_DIM)
    mixed = jnp.einsum('gpq,bnqgc->bnpgc', w_s, zg) + b_s.T[None, None, :, :, None]
    return u * mixed.reshape(B, T, SG_WIDTH)


def mixer_ab(h, hc, w_in, w_out, q_g, k_g, rel_bias, sg_norm_g, sg_norm_b, sg_w, sg_b, ctx_out):
    B, S, _ = h.shape
    Bc, T, _ = hc.shape

    def heads(p):
        return p.reshape(*p.shape[:-1], NA_HEADS, NA_HEAD_DIM)

    def gated_mix(p_uv):
        u, z = jnp.split(jax.nn.gelu(p_uv), 2, axis=-1)
        return spatial_gating(u, layer_norm(z, sg_norm_g, sg_norm_b), sg_w, sg_b)

    p = h @ w_in
    q = rms_norm(heads(p[..., :NA_WIDTH]), q_g)
    k = rms_norm(heads(p[..., NA_WIDTH:2 * NA_WIDTH]), k_g)
    v = heads(p[..., 2 * NA_WIDTH:3 * NA_WIDTH])
    if ctx_out:
        pc = hc @ w_in
        qc = rms_norm(heads(pc[..., :NA_WIDTH]), q_g)
        kc = rms_norm(heads(pc[..., NA_WIDTH:2 * NA_WIDTH]), k_g)
        vc = heads(pc[..., 2 * NA_WIDTH:3 * NA_WIDTH])
    else:
        pkv = hc @ w_in[:, NA_WIDTH:3 * NA_WIDTH]
        kc = rms_norm(heads(pkv[..., :NA_WIDTH]), k_g)
        vc = heads(pkv[..., NA_WIDTH:])
    a_out = neighbourhood_attention(q, k, v, kc, vc, rel_bias).reshape(B, S, NA_WIDTH)
    y = jnp.concatenate([a_out, gated_mix(p[..., 3 * NA_WIDTH:])], axis=-1) @ w_out
    if not ctx_out:
        return y, None
    ac = dense_attention(qc, kc, vc).reshape(Bc, T, NA_WIDTH)
    yc = jnp.concatenate([ac, gated_mix(pc[..., 3 * NA_WIDTH:])], axis=-1) @ w_out
    return y, yc


def mixer_mla(h, hc, cos, sin, w_in, q_norm_g, kv_norm_g, w_uq, w_ukv, q_g, k_g, w_out, ctx_out):
    B, S, _ = h.shape
    Bc, T, _ = hc.shape

    def rope_tail(t):
        return jnp.concatenate([t[..., :QK_NOPE], apply_rope(t[..., QK_NOPE:], cos, sin)], axis=-1)

    def queries(p_q, rope):
        c_q = rms_norm(p_q, q_norm_g)
        q = (c_q @ w_uq).reshape(*p_q.shape[:2], MLA_HEADS, QK_DIM)
        q = rms_norm(q, q_g)
        return rope_tail(q) if rope else q

    def keys_values(p_kv, rope):
        c_kv = rms_norm(p_kv[..., :KV_LORA], kv_norm_g)
        k_pe = p_kv[..., KV_LORA:]
        kv = (c_kv @ w_ukv).reshape(*p_kv.shape[:2], MLA_HEADS, QK_NOPE + V_DIM)
        k_pe = jnp.broadcast_to(k_pe[:, :, None, :], (*p_kv.shape[:2], MLA_HEADS, QK_ROPE))
        k = rms_norm(jnp.concatenate([kv[..., :QK_NOPE], k_pe], axis=-1), k_g)
        return (rope_tail(k) if rope else k), kv[..., QK_NOPE:]

    p = h @ w_in
    q = queries(p[..., :Q_LORA], True)
    k, v = keys_values(p[..., Q_LORA:], True)
    if ctx_out:
        pc = hc @ w_in
        qc = queries(pc[..., :Q_LORA], False)
        kc, vc = keys_values(pc[..., Q_LORA:], False)
    else:
        kc, vc = keys_values(hc @ w_in[:, Q_LORA:], False)
    k_all = jnp.concatenate([kc, k], axis=1)
    v_all = jnp.concatenate([vc, v], axis=1)
    y = blocked_attention(q, k_all, v_all).reshape(B, S, MLA_HEADS * V_DIM) @ w_out
    if not ctx_out:
        return y, None
    yc = dense_attention(qc, kc, vc).reshape(Bc, T, MLA_HEADS * V_DIM) @ w_out
    return y, yc


def moe_ffn(t, w_router, b_router, w_gu, b_gu, w_down, b_down):
    N, D = t.shape
    logits = (t @ w_router).astype(jnp.float32) + b_router.astype(jnp.float32)
    top_val, top_idx = lax.top_k(logits, TOP_K)
    gate = jax.nn.softmax(top_val, axis=-1)
    nk = N * TOP_K
    flat_e = top_idx.reshape(nk)
    order = jnp.argsort(flat_e)
    sorted_e = flat_e[order]
    counts = jnp.bincount(flat_e, length=N_EXPERTS)
    padded = (counts + MOE_BLOCK - 1) // MOE_BLOCK * MOE_BLOCK
    pad_end = jnp.cumsum(padded)
    pad_start = pad_end - padded
    start = jnp.cumsum(counts) - counts
    dest = pad_start[sorted_e] + jnp.arange(nk) - start[sorted_e]
    n_blocks = -(-nk // MOE_BLOCK) + N_EXPERTS
    tok_sorted = order // TOP_K
    rows_tok = jnp.zeros((n_blocks * MOE_BLOCK,), jnp.int32).at[dest].set(tok_sorted.astype(jnp.int32))
    block_e = jnp.minimum(jnp.searchsorted(pad_end, jnp.arange(n_blocks) * MOE_BLOCK, side='right'), N_EXPERTS - 1)

    def expert_block(args):
        e, tok = args
        gu = t[tok] @ w_gu[e] + b_gu[e]
        g, u = jnp.split(gu, 2, axis=-1)
        g = jnp.minimum(g, SWIGLU_LIMIT)
        u = jnp.clip(u, -SWIGLU_LIMIT, SWIGLU_LIMIT)
        hid = (u + 1) * (g * jax.nn.sigmoid(SWIGLU_ALPHA * g))
        return hid @ w_down[e] + b_down[e]

    out = lax.map(expert_block, (block_e, rows_tok.reshape(n_blocks, MOE_BLOCK)))
    y_sorted = out.reshape(n_blocks * MOE_BLOCK, D)[dest]
    w_sorted = gate.reshape(nk)[order].astype(t.dtype)
    return jax.ops.segment_sum(y_sorted * w_sorted[:, None], tok_sorted, num_segments=N)


def setup_inputs(seed: int = 0) -> dict:
    key = jax.random.key(seed)
    ks = jax.random.split(key, 40)
    f32 = jnp.float32

    def nrm(k, shape, scale):
        return jax.random.normal(k, shape, f32) * scale

    def gain(k, shape):
        return 1.0 + 0.02 * jax.random.normal(k, shape, f32)

    D = D_MODEL
    return {
        "x": nrm(ks[0], (BATCH, SEQ, D), 1.0),
        "c": nrm(ks[1], (BATCH, D), 1.0),
        "ctx": nrm(ks[2], (BATCH, CTX_LEN, D), 1.0),
        "c_ctx": nrm(ks[3], (D,), 1.0),
        "ada_w": nrm(ks[4], (DEPTH, D, 6 * D), 0.5 * D ** -0.5),
        "ada_b": nrm(ks[5], (DEPTH, 6 * D), 0.01),
        "norm1_g": gain(ks[6], (DEPTH, D)),
        "norm2_g": gain(ks[7], (DEPTH, D)),
        "ab_w_in": nrm(ks[8], (N_EVEN, D, AB_IN), D ** -0.5),
        "ab_w_out": nrm(ks[9], (N_EVEN, AB_OUT, D), AB_OUT ** -0.5),
        "na_q_g": gain(ks[10], (N_EVEN, NA_HEAD_DIM)),
        "na_k_g": gain(ks[11], (N_EVEN, NA_HEAD_DIM)),
        "na_rel_bias": nrm(ks[12], (N_EVEN, NA_HEADS, 2 * NA_WIN_H - 1, 2 * NA_WIN_W - 1), 0.1),
        "sg_norm_g": gain(ks[13], (N_EVEN, SG_WIDTH)),
        "sg_norm_b": nrm(ks[14], (N_EVEN, SG_WIDTH), 0.02),
        "sg_w": nrm(ks[15], (N_EVEN, SG_GROUPS, SG_CHUNK, SG_CHUNK), SG_CHUNK ** -0.5),
        "sg_b": gain(ks[16], (N_EVEN, SG_GROUPS, SG_CHUNK)),
        "mla_w_in": nrm(ks[17], (N_ODD, D, MLA_IN), D ** -0.5),
        "mla_q_norm_g": gain(ks[18], (N_ODD, Q_LORA)),
        "mla_kv_norm_g": gain(ks[19], (N_ODD, KV_LORA)),
        "mla_w_uq": nrm(ks[20], (N_ODD, Q_LORA, MLA_HEADS * QK_DIM), Q_LORA ** -0.5),
        "mla_w_ukv": nrm(ks[21], (N_ODD, KV_LORA, MLA_HEADS * (QK_NOPE + V_DIM)), KV_LORA ** -0.5),
        "mla_q_g": gain(ks[22], (N_ODD, QK_DIM)),
        "mla_k_g": gain(ks[23], (N_ODD, QK_DIM)),
        "mla_w_out": nrm(ks[24], (N_ODD, MLA_HEADS * V_DIM, D), (MLA_HEADS * V_DIM) ** -0.5),
        "moe_w_router": nrm(ks[25], (DEPTH, D, N_EXPERTS), D ** -0.5),
        "moe_b_router": nrm(ks[26], (DEPTH, N_EXPERTS), 0.01),
        "moe_w_gu": nrm(ks[27], (DEPTH, N_EXPERTS, D, 2 * D_FF), D ** -0.5),
        "moe_b_gu": nrm(ks[28], (DEPTH, N_EXPERTS, 2 * D_FF), 0.01),
        "moe_w_down": nrm(ks[29], (DEPTH, N_EXPERTS, D_FF, D), D_FF ** -0.5),
        "moe_b_down": nrm(ks[30], (DEPTH, N_EXPERTS, D), 0.01),
    }


def reference(x, c, ctx, c_ctx, ada_w, ada_b, norm1_g, norm2_g,
              ab_w_in, ab_w_out, na_q_g, na_k_g, na_rel_bias, sg_norm_g, sg_norm_b, sg_w, sg_b,
              mla_w_in, mla_q_norm_g, mla_kv_norm_g, mla_w_uq, mla_w_ukv, mla_q_g, mla_k_g, mla_w_out,
              moe_w_router, moe_b_router, moe_w_gu, moe_b_gu, moe_w_down, moe_b_down):
    B, S, D = x.shape
    T = ctx.shape[1]
    cos, sin = axial_rope_tables(S, QK_ROPE)
    xc = ctx
    for l in range(DEPTH):
        last = l == DEPTH - 1
        i = l // 2
        mod = jnp.split(jax.nn.silu(c) @ ada_w[l] + ada_b[l], 6, axis=-1)
        sh1, sc1, g1, sh2, sc2, g2 = [m[:, None, :] for m in mod]
        sh1c, sc1c, g1c, sh2c, sc2c, g2c = jnp.split(jax.nn.silu(c_ctx) @ ada_w[l] + ada_b[l], 6, axis=-1)
        h = modulate(rms_norm(x, norm1_g[l]), sh1, sc1)
        hc = modulate(rms_norm(xc, norm1_g[l]), sh1c, sc1c)
        if l % 2 == 0:
            y, yc = mixer_ab(h, hc, ab_w_in[i], ab_w_out[i], na_q_g[i], na_k_g[i], na_rel_bias[i],
                             sg_norm_g[i], sg_norm_b[i], sg_w[i], sg_b[i], not last)
        else:
            y, yc = mixer_mla(h, hc, cos, sin, mla_w_in[i], mla_q_norm_g[i], mla_kv_norm_g[i],
                              mla_w_uq[i], mla_w_ukv[i], mla_q_g[i], mla_k_g[i], mla_w_out[i], not last)
        x = x + g1 * y
        h2 = modulate(rms_norm(x, norm2_g[l]), sh2, sc2).reshape(B * S, D)
        if last:
            f = moe_ffn(h2, moe_w_router[l], moe_b_router[l], moe_w_gu[l], moe_b_gu[l],
                        moe_w_down[l], moe_b_down[l])
            x = x + g2 * f.reshape(B, S, D)
        else:
            xc = xc + g1c * yc
            h2c = modulate(rms_norm(xc, norm2_g[l]), sh2c, sc2c).reshape(B * T, D)
            f = moe_ffn(jnp.concatenate([h2c, h2], axis=0), moe_w_router[l], moe_b_router[l],
                        moe_w_gu[l], moe_b_gu[l], moe_w_down[l], moe_b_down[l])
            xc = xc + g2c * f[:B * T].reshape(B, T, D)
            x = x + g2 * f[B * T:].reshape(B, S, D)
    return x
```

```python
import functools

import numpy as np
import jax
import jax.numpy as jnp
from jax import lax
from jax.experimental import pallas as pl
from jax.experimental.pallas import tpu as pltpu

F32 = jnp.float32
BF16 = jnp.bfloat16

D_MODEL = 2048
GRID_W = 64
EPS = 1e-6
NEG = -1e30

NA_HEADS = 8
NA_HEAD_DIM = 128
NA_WIDTH = NA_HEADS * NA_HEAD_DIM
NA_WIN_H = 8
NA_WIN_W = 16
SG_GROUPS = 8
SG_WIDTH = D_MODEL // 2
SG_GROUP_DIM = SG_WIDTH // SG_GROUPS
SG_CHUNK = 128

MLA_HEADS = 16
Q_LORA = 512
KV_LORA = 256
QK_NOPE = 128
QK_ROPE = 64
V_DIM = 128
QK_DIM = QK_NOPE + QK_ROPE
QK_PAD = 256
MLA_IN_PAD = Q_LORA + KV_LORA + 128
ROPE_THETA = 10000.0

N_EXPERTS = 32
TOP_K = 4
D_FF = D_MODEL
SWIGLU_ALPHA = 1.702
SWIGLU_LIMIT = 7.0

LANE = 128
V7X_VMEM_LIMIT = 56 * 1024 * 1024

NA_TILE_ROWS = 4
NA_BAND_ROWS = NA_TILE_ROWS + NA_WIN_H - 1
MOE_TM = 256
MOE_TN = 512


def _params(n_axes):
    return pltpu.CompilerParams(dimension_semantics=("arbitrary",) * n_axes,
                                vmem_limit_bytes=V7X_VMEM_LIMIT)


def _norm_mod_body(x_ref, g_ref, sh_ref, sc_ref):
    x = x_ref[...]
    y = x * lax.rsqrt(jnp.mean(x * x, axis=-1, keepdims=True) + EPS) * g_ref[...]
    return y * (1.0 + sc_ref[...]) + sh_ref[...]


def _norm_mod_kernel(x_ref, g_ref, sh_ref, sc_ref, o_ref):
    o_ref[...] = _norm_mod_body(x_ref, g_ref, sh_ref, sc_ref).astype(o_ref.dtype)


def _norm_mod_router_kernel(x_ref, g_ref, sh_ref, sc_ref, wr_ref, br_ref, o_ref, lg_ref):
    h = _norm_mod_body(x_ref, g_ref, sh_ref, sc_ref)
    o_ref[...] = h.astype(o_ref.dtype)
    lg_ref[...] = jnp.dot(h, wr_ref[...], precision=lax.Precision.HIGHEST,
                          preferred_element_type=F32) + br_ref[...]


def _mod_row(i, tm, n_ctx_rows, seq):
    n_ctx_tiles = n_ctx_rows // tm
    return jnp.where(i < n_ctx_tiles, 0, 1 + (i - n_ctx_tiles) // (seq // tm))


def _norm_mod(x, gain, mod, k_shift, k_scale, n_ctx_rows, seq, router=None, tm=256):
    rows, d = x.shape
    mrow = functools.partial(_mod_row, tm=tm, n_ctx_rows=n_ctx_rows, seq=seq)
    in_specs = [
        pl.BlockSpec((tm, d), lambda i: (i, 0)),
        pl.BlockSpec((1, d), lambda i: (0, 0)),
        pl.BlockSpec((None, 1, d), lambda i: (mrow(i), 0, k_shift)),
        pl.BlockSpec((None, 1, d), lambda i: (mrow(i), 0, k_scale)),
    ]
    args = [x, gain.reshape(1, d), mod, mod]
    if router is None:
        return pl.pallas_call(
            _norm_mod_kernel, grid=(rows // tm,), in_specs=in_specs,
            out_specs=pl.BlockSpec((tm, d), lambda i: (i, 0)),
            out_shape=jax.ShapeDtypeStruct((rows, d), BF16),
            compiler_params=_params(1), name="norm_mod")(*args)
    w_r, b_r = router
    n_e = w_r.shape[1]
    in_specs += [pl.BlockSpec((d, n_e), lambda i: (0, 0)), pl.BlockSpec((1, n_e), lambda i: (0, 0))]
    return pl.pallas_call(
        _norm_mod_router_kernel, grid=(rows // tm,), in_specs=in_specs,
        out_specs=[pl.BlockSpec((tm, d), lambda i: (i, 0)), pl.BlockSpec((tm, n_e), lambda i: (i, 0))],
        out_shape=[jax.ShapeDtypeStruct((rows, d), BF16), jax.ShapeDtypeStruct((rows, n_e), F32)],
        compiler_params=_params(1), name="norm_mod_router")(*args, w_r, b_r.reshape(1, n_e))


def _matmul_kernel(*refs, n_extra, n_out, epilogue):
    x_ref, w_ref = refs[:2]
    extra = refs[2:2 + n_extra]
    outs = refs[2 + n_extra:2 + n_extra + n_out]
    wbf_ref = refs[-1]

    @pl.when(pl.program_id(1) == 0)
    def _():
        wbf_ref[...] = w_ref[...].astype(BF16)

    acc = jnp.dot(x_ref[...], wbf_ref[...], preferred_element_type=F32)
    epilogue(acc, extra, outs)


def _matmul(x, w, *, w_lead=(), col0=0, n_cols, tm, tn, epilogue, extras=(), outs, name):
    m, k = x.shape
    assert m % tm == 0 and n_cols % tn == 0 and col0 % tn == 0
    lead = tuple(w_lead)
    w_spec = pl.BlockSpec((None,) * len(lead) + (k, tn), lambda j, i: lead + (0, col0 // tn + j))
    in_specs = [pl.BlockSpec((tm, k), lambda j, i: (i, 0)), w_spec]
    in_specs += [pl.BlockSpec(bs, im) for _, bs, im in extras]
    out_specs = [pl.BlockSpec(bs, im) for _, bs, im in outs]
    kern = functools.partial(_matmul_kernel, n_extra=len(extras), n_out=len(outs), epilogue=epilogue)
    res = pl.pallas_call(
        kern, grid=(n_cols // tn, m // tm), in_specs=in_specs, out_specs=out_specs,
        out_shape=[s for s, _, _ in outs],
        scratch_shapes=[pltpu.VMEM((k, tn), BF16)],
        compiler_params=_params(2), name=name)(x, w, *[a for a, _, _ in extras])
    return res


def _ep_bias(acc, extra, outs):
    outs[0][...] = acc + extra[0][...]


def _ep_cast(acc, extra, outs):
    outs[0][...] = acc.astype(outs[0].dtype)


def _ep_gelu(acc, extra, outs):
    outs[0][...] = jax.nn.gelu(acc).astype(outs[0].dtype)


def _ep_head_rms(acc, extra, outs):
    g = extra[0][...]
    for h in range(acc.shape[1] // LANE):
        a = acc[:, h * LANE:(h + 1) * LANE]
        r = lax.rsqrt(jnp.mean(a * a, axis=-1, keepdims=True) + EPS)
        outs[0][:, h * LANE:(h + 1) * LANE] = (a * r * g).astype(outs[0].dtype)


def _ep_residual(acc, extra, outs):
    outs[0][...] = extra[0][...] + extra[1][...] * acc


def _rms(a, g):
    return a * lax.rsqrt(jnp.mean(a * a, axis=-1, keepdims=True) + EPS) * g


def _ep_mla_in(acc, extra, outs):
    outs[0][...] = _rms(acc[:, :Q_LORA], extra[0][...]).astype(BF16)
    outs[1][...] = _rms(acc[:, Q_LORA:Q_LORA + KV_LORA], extra[1][...]).astype(BF16)
    outs[2][...] = acc[:, Q_LORA + KV_LORA:]


def _rope_tail(t, tab_ref):
    c = tab_ref[:, 0:LANE]
    s_left = tab_ref[:, LANE:2 * LANE]
    s_right = tab_ref[:, 2 * LANE:3 * LANE]
    return (t * c + pltpu.roll(t, LANE - QK_ROPE // 2, axis=1) * s_left
            + pltpu.roll(t, QK_ROPE // 2, axis=1) * s_right)


def _ep_mla_q(acc, extra, outs):
    g_ref, tab_ref = extra
    g0 = g_ref[:, :LANE]
    g1 = g_ref[:, LANE:]
    for h in range(acc.shape[1] // QK_PAD):
        a0 = acc[:, h * QK_PAD:h * QK_PAD + LANE]
        a1 = acc[:, h * QK_PAD + LANE:(h + 1) * QK_PAD]
        ss = jnp.sum(a0 * a0, axis=-1, keepdims=True) + jnp.sum(a1 * a1, axis=-1, keepdims=True)
        r = lax.rsqrt(ss * (1.0 / QK_DIM) + EPS)
        outs[0][:, h * QK_PAD:h * QK_PAD + LANE] = (a0 * r * g0).astype(BF16)
        outs[0][:, h * QK_PAD + LANE:(h + 1) * QK_PAD] = _rope_tail(a1 * r * g1, tab_ref).astype(BF16)


def _ep_mla_kv(acc, extra, outs):
    g_ref, tab_ref, pe_ref = extra
    k_out, v_out = outs
    g0 = g_ref[:, :LANE]
    g1 = g_ref[:, LANE:]
    pe = pe_ref[...]
    pe_ss = jnp.sum(pe * pe, axis=-1, keepdims=True)
    for h in range(acc.shape[1] // (QK_NOPE + V_DIM)):
        base = h * (QK_NOPE + V_DIM)
        kn = acc[:, base:base + QK_NOPE]
        r = lax.rsqrt((jnp.sum(kn * kn, axis=-1, keepdims=True) + pe_ss) * (1.0 / QK_DIM) + EPS)
        k_out[:, h * QK_PAD:h * QK_PAD + LANE] = (kn * r * g0).astype(BF16)
        k_out[:, h * QK_PAD + LANE:(h + 1) * QK_PAD] = _rope_tail(pe * r * g1, tab_ref).astype(BF16)
        v_out[:, h * V_DIM:(h + 1) * V_DIM] = acc[:, base + QK_NOPE:base + QK_NOPE + V_DIM].astype(BF16)


def _softmax_pv(s_parts, v_parts):
    m = s_parts[0].max(axis=-1, keepdims=True)
    for s in s_parts[1:]:
        m = jnp.maximum(m, s.max(axis=-1, keepdims=True))
    l = 0.0
    o = 0.0
    for s, v in zip(s_parts, v_parts):
        p = jnp.exp(s - m)
        l = l + p.sum(axis=-1, keepdims=True)
        o = o + jnp.dot(p.astype(BF16), v, preferred_element_type=F32)
    return o / l


def _qk(q, k):
    return lax.dot_general(q, k, (((1,), (1,)), ((), ())), preferred_element_type=F32)


def _attn_kernel(*refs, has_ctx):
    if has_ctx:
        q_ref, k_ref, v_ref, kc_ref, vc_ref, o_ref = refs
    else:
        q_ref, k_ref, v_ref, o_ref = refs
    q = q_ref[...]
    s_parts = [_qk(q, k_ref[...])]
    v_parts = [v_ref[...]]
    if has_ctx:
        s_parts.append(_qk(q, kc_ref[...]))
        v_parts.append(vc_ref[...])
    o_ref[...] = _softmax_pv(s_parts, v_parts).astype(o_ref.dtype)


def _attention(q, k, v, *, n_batch, n_heads, dq, dv, tq, n_q_tiles, q_blk0, t_k, k_blk0, t_ctx=None, name):
    has_ctx = t_ctx is not None
    in_specs = [
        pl.BlockSpec((tq, dq), lambda b, h, i: (q_blk0 + b * n_q_tiles + i, h)),
        pl.BlockSpec((t_k, dq), lambda b, h, i: (k_blk0 + b, h)),
        pl.BlockSpec((t_k, dv), lambda b, h, i: (k_blk0 + b, h)),
    ]
    args = [q, k, v]
    if has_ctx:
        in_specs += [pl.BlockSpec((t_ctx, dq), lambda b, h, i: (b, h)),
                     pl.BlockSpec((t_ctx, dv), lambda b, h, i: (b, h))]
        args += [k, v]
    return pl.pallas_call(
        functools.partial(_attn_kernel, has_ctx=has_ctx),
        grid=(n_batch, n_heads, n_q_tiles), in_specs=in_specs,
        out_specs=pl.BlockSpec((tq, dv), lambda b, h, i: (b * n_q_tiles + i, h)),
        out_shape=jax.ShapeDtypeStruct((n_batch * n_q_tiles * tq, n_heads * dv), BF16),
        compiler_params=_params(3), name=name)(*args)


def _na_band_start(t, rows):
    return jnp.clip(t * NA_TILE_ROWS - NA_WIN_H // 2, 0, rows - NA_BAND_ROWS)


def _na_kernel(q_ref, k_ref, v_ref, kc_ref, vc_ref, bias_ref, o_ref, *, rows):
    t = pl.program_id(1)
    ks = pl.multiple_of(_na_band_start(t, rows) * GRID_W, GRID_W)
    for h in range(NA_HEADS):
        cols = slice(h * NA_HEAD_DIM, (h + 1) * NA_HEAD_DIM)
        q = q_ref[:, cols]
        k_band = k_ref[pl.ds(ks, NA_BAND_ROWS * GRID_W), cols]
        v_band = v_ref[pl.ds(ks, NA_BAND_ROWS * GRID_W), cols]
        s_loc = _qk(q, k_band) + bias_ref[h]
        s_ctx = _qk(q, kc_ref[:, cols])
        o_ref[:, cols] = _softmax_pv([s_loc, s_ctx], [v_band, vc_ref[:, cols]]).astype(o_ref.dtype)


def _na_bias_tables(rel_bias, rows):
    n_tiles = rows // NA_TILE_ROWS
    qr = np.arange(NA_TILE_ROWS)[:, None, None, None]
    qc = np.arange(GRID_W)[None, :, None, None]
    ur = np.arange(NA_BAND_ROWS)[None, None, :, None]
    kc = np.arange(GRID_W)[None, None, None, :]
    geoms, ids = [], []
    for t in range(n_tiles):
        u0 = int(np.clip(t * NA_TILE_ROWS - NA_WIN_H // 2, 0, rows - NA_BAND_ROWS))
        r = t * NA_TILE_ROWS + qr
        key_row = u0 + ur
        r0 = np.clip(r - NA_WIN_H // 2, 0, rows - NA_WIN_H)
        c0 = np.clip(qc - NA_WIN_W // 2, 0, GRID_W - NA_WIN_W)
        valid = (key_row >= r0) & (key_row < r0 + NA_WIN_H) & (kc >= c0) & (kc < c0 + NA_WIN_W)
        ridx = np.clip(key_row - r + NA_WIN_H - 1, 0, 2 * NA_WIN_H - 2)
        cidx = np.clip(kc - qc + NA_WIN_W - 1, 0, 2 * NA_WIN_W - 2)
        shape = (NA_TILE_ROWS, GRID_W, NA_BAND_ROWS, GRID_W)
        geom = (np.broadcast_to(valid, shape), np.broadcast_to(ridx, shape), np.broadcast_to(cidx, shape))
        for gi, g in enumerate(geoms):
            if all(np.array_equal(a, b) for a, b in zip(g, geom)):
                ids.append(gi)
                break
        else:
            ids.append(len(geoms))
            geoms.append(geom)
    nq = NA_TILE_ROWS * GRID_W
    nk = NA_BAND_ROWS * GRID_W
    valid = np.stack([g[0].reshape(nq, nk) for g in geoms])
    ridx = np.stack([g[1].reshape(nq, nk) for g in geoms])
    cidx = np.stack([g[2].reshape(nq, nk) for g in geoms])
    bias = rel_bias.astype(F32)[:, ridx, cidx]
    tables = jnp.where(valid[None], bias, NEG).transpose(1, 0, 2, 3)
    return tables, ids


def _na_attention(q, k, v, rel_bias, *, n_batch, seq, t_ctx):
    rows = seq // GRID_W
    n_tiles = rows // NA_TILE_ROWS
    tq = NA_TILE_ROWS * GRID_W
    tables, ids = _na_bias_tables(rel_bias, rows)
    assert ids == [0] + [1] * (n_tiles - 2) + [2], ids
    ctx_tiles = n_batch * t_ctx // tq
    ctx_units = n_batch * t_ctx // seq

    def table_id(t):
        return jnp.where(t == 0, 0, jnp.where(t == n_tiles - 1, 2, 1))

    return pl.pallas_call(
        functools.partial(_na_kernel, rows=rows),
        grid=(n_batch, n_tiles),
        in_specs=[
            pl.BlockSpec((tq, NA_WIDTH), lambda b, t: (ctx_tiles + b * n_tiles + t, 0)),
            pl.BlockSpec((seq, NA_WIDTH), lambda b, t: (ctx_units + b, 0)),
            pl.BlockSpec((seq, NA_WIDTH), lambda b, t: (ctx_units + b, 0)),
            pl.BlockSpec((t_ctx, NA_WIDTH), lambda b, t: (b, 0)),
            pl.BlockSpec((t_ctx, NA_WIDTH), lambda b, t: (b, 0)),
            pl.BlockSpec((None, NA_HEADS, tq, NA_BAND_ROWS * GRID_W), lambda b, t: (table_id(t), 0, 0, 0)),
        ],
        out_specs=pl.BlockSpec((tq, NA_WIDTH), lambda b, t: (b * n_tiles + t, 0)),
        out_shape=jax.ShapeDtypeStruct((n_batch * seq, NA_WIDTH), BF16),
        compiler_params=_params(2), name="na_attention")(q, k, v, k, v, tables)


def _sg_kernel(uv_ref, g_ref, b_ref, ws_ref, bs_ref, o_ref, *, chunks):
    for c in range(chunks):
        r = slice(c * SG_CHUNK, (c + 1) * SG_CHUNK)
        z = uv_ref[r, SG_WIDTH:].astype(F32)
        mu = jnp.mean(z, axis=-1, keepdims=True)
        zc = z - mu
        var = jnp.mean(zc * zc, axis=-1, keepdims=True)
        zn = (zc * lax.rsqrt(var + EPS) * g_ref[...] + b_ref[...]).astype(BF16)
        for g in range(SG_GROUPS):
            cols = slice(g * SG_GROUP_DIM, (g + 1) * SG_GROUP_DIM)
            mixed = jnp.dot(ws_ref[g], zn[:, cols], preferred_element_type=F32) + bs_ref[g]
            o_ref[r, cols] = (uv_ref[r, cols].astype(F32) * mixed).astype(o_ref.dtype)


def _spatial_gating(uv, ln_g, ln_b, w_s, b_s, tm=512):
    rows = uv.shape[0]
    bs = jnp.broadcast_to(b_s.astype(F32)[:, :, None], (SG_GROUPS, SG_CHUNK, SG_GROUP_DIM))
    return pl.pallas_call(
        functools.partial(_sg_kernel, chunks=tm // SG_CHUNK),
        grid=(rows // tm,),
        in_specs=[
            pl.BlockSpec((tm, 2 * SG_WIDTH), lambda i: (i, 0)),
            pl.BlockSpec((1, SG_WIDTH), lambda i: (0, 0)),
            pl.BlockSpec((1, SG_WIDTH), lambda i: (0, 0)),
            pl.BlockSpec((SG_GROUPS, SG_CHUNK, SG_CHUNK), lambda i: (0, 0, 0)),
            pl.BlockSpec((SG_GROUPS, SG_CHUNK, SG_GROUP_DIM), lambda i: (0, 0, 0)),
        ],
        out_specs=pl.BlockSpec((tm, SG_WIDTH), lambda i: (i, 0)),
        out_shape=jax.ShapeDtypeStruct((rows, SG_WIDTH), BF16),
        compiler_params=_params(1), name="spatial_gating")(
            uv, ln_g.reshape(1, SG_WIDTH), ln_b.reshape(1, SG_WIDTH), w_s.astype(BF16), bs)


def _tile_weights_changed(i, te_ref):
    return (i == 0) | (te_ref[i] != te_ref[jnp.maximum(i - 1, 0)])


def _moe_gu_kernel(te_ref, nu_ref, x_ref, wg_ref, wu_ref, bg_ref, bu_ref, o_ref, wg_bf, wu_bf):
    i = pl.program_id(1)

    @pl.when(_tile_weights_changed(i, te_ref))
    def _():
        wg_bf[...] = wg_ref[...].astype(BF16)
        wu_bf[...] = wu_ref[...].astype(BF16)

    @pl.when(i < nu_ref[0])
    def _():
        x = x_ref[...]
        g = jnp.dot(x, wg_bf[...], preferred_element_type=F32) + bg_ref[...]
        u = jnp.dot(x, wu_bf[...], preferred_element_type=F32) + bu_ref[...]
        g = jnp.minimum(g, SWIGLU_LIMIT)
        u = jnp.clip(u, -SWIGLU_LIMIT, SWIGLU_LIMIT)
        o_ref[...] = ((u + 1.0) * (g * jax.nn.sigmoid(SWIGLU_ALPHA * g))).astype(o_ref.dtype)

    @pl.when(i >= nu_ref[0])
    def _():
        o_ref[...] = jnp.zeros_like(o_ref)


def _moe_down_kernel(te_ref, nu_ref, h_ref, w_ref, b_ref, rw_ref, o_ref, w_bf):
    i = pl.program_id(1)

    @pl.when(_tile_weights_changed(i, te_ref))
    def _():
        w_bf[...] = w_ref[...].astype(BF16)

    @pl.when(i < nu_ref[0])
    def _():
        y = jnp.dot(h_ref[...], w_bf[...], preferred_element_type=F32) + b_ref[...]
        o_ref[...] = y * rw_ref[...]

    @pl.when(i >= nu_ref[0])
    def _():
        o_ref[...] = jnp.zeros_like(o_ref)


def _moe_experts(x_sorted, row_w, tile_e, n_used, layer, w_gu, b_gu, w_down, b_down):
    r_pad, d = x_sorted.shape
    n_tiles = r_pad // MOE_TM
    tm, tn = MOE_TM, MOE_TN
    n_j = D_FF // tn

    def row_blk(i, nu):
        return jnp.minimum(i, nu[0] - 1)

    hid = pl.pallas_call(
        _moe_gu_kernel,
        grid_spec=pltpu.PrefetchScalarGridSpec(
            num_scalar_prefetch=2, grid=(n_j, n_tiles),
            in_specs=[
                pl.BlockSpec((tm, d), lambda j, i, te, nu: (row_blk(i, nu), 0)),
                pl.BlockSpec((None, None, d, tn), lambda j, i, te, nu: (layer, te[i], 0, j)),
                pl.BlockSpec((None, None, d, tn), lambda j, i, te, nu: (layer, te[i], 0, n_j + j)),
                pl.BlockSpec((None, None, 1, tn), lambda j, i, te, nu: (layer, te[i], 0, j)),
                pl.BlockSpec((None, None, 1, tn), lambda j, i, te, nu: (layer, te[i], 0, n_j + j)),
            ],
            out_specs=pl.BlockSpec((tm, tn), lambda j, i, te, nu: (i, j)),
            scratch_shapes=[pltpu.VMEM((d, tn), BF16), pltpu.VMEM((d, tn), BF16)]),
        out_shape=jax.ShapeDtypeStruct((r_pad, D_FF), BF16),
        compiler_params=_params(2), name="moe_gate_up")(
            tile_e, n_used, x_sorted, w_gu, w_gu, b_gu[:, :, None, :], b_gu[:, :, None, :])

    n_jd = d // tn
    return pl.pallas_call(
        _moe_down_kernel,
        grid_spec=pltpu.PrefetchScalarGridSpec(
            num_scalar_prefetch=2, grid=(n_jd, n_tiles),
            in_specs=[
                pl.BlockSpec((tm, D_FF), lambda j, i, te, nu: (row_blk(i, nu), 0)),
                pl.BlockSpec((None, None, D_FF, tn), lambda j, i, te, nu: (layer, te[i], 0, j)),
                pl.BlockSpec((None, None, 1, tn), lambda j, i, te, nu: (layer, te[i], 0, j)),
                pl.BlockSpec((tm, 1), lambda j, i, te, nu: (row_blk(i, nu), 0)),
            ],
            out_specs=pl.BlockSpec((tm, tn), lambda j, i, te, nu: (i, j)),
            scratch_shapes=[pltpu.VMEM((D_FF, tn), BF16)]),
        out_shape=jax.ShapeDtypeStruct((r_pad, d), F32),
        compiler_params=_params(2), name="moe_down")(
            tile_e, n_used, hid, w_down, b_down[:, :, None, :], row_w)


def _moe(h2, logits, layer, w_gu, b_gu, w_down, b_down):
    n, d = h2.shape
    nk = n * TOP_K
    tm = MOE_TM
    top_val, top_idx = lax.top_k(logits, TOP_K)
    gate = jax.nn.softmax(top_val, axis=-1)
    flat_e = top_idx.reshape(nk)
    order = jnp.argsort(flat_e)
    sorted_e = flat_e[order]
    counts = jnp.bincount(flat_e, length=N_EXPERTS)
    padded = (counts + tm - 1) // tm * tm
    pad_end = jnp.cumsum(padded)
    pad_start = pad_end - padded
    start = jnp.cumsum(counts) - counts
    dest = (pad_start[sorted_e] + jnp.arange(nk) - start[sorted_e]).astype(jnp.int32)
    n_tiles = nk // tm + N_EXPERTS
    r_pad = n_tiles * tm
    tok_sorted = (order // TOP_K).astype(jnp.int32)
    rows_tok = jnp.zeros((r_pad,), jnp.int32).at[dest].set(tok_sorted)
    row_w = jnp.zeros((r_pad,), F32).at[dest].set(gate.reshape(nk)[order])
    n_used = (pad_end[-1] // tm).astype(jnp.int32)
    tile_pos = jnp.minimum(jnp.arange(n_tiles), n_used - 1) * tm
    tile_e = jnp.minimum(jnp.searchsorted(pad_end, tile_pos, side='right'), N_EXPERTS - 1).astype(jnp.int32)
    pos = jnp.zeros((nk,), jnp.int32).at[order].set(dest).reshape(n, TOP_K)

    x_sorted = h2[rows_tok]
    y = _moe_experts(x_sorted, row_w.reshape(r_pad, 1), tile_e, n_used.reshape(1), layer,
                     w_gu, b_gu, w_down, b_down)
    return y[pos].sum(axis=1)


def _axial_rope_tables(seq, tm):
    half = QK_ROPE // 2
    t = jnp.arange(seq)
    row = (t // GRID_W).astype(F32)
    col = (t % GRID_W).astype(F32)
    n_freq = QK_ROPE // 4
    inv = ROPE_THETA ** (-jnp.arange(n_freq, dtype=F32) / n_freq)
    ang = jnp.concatenate([row[:, None] * inv, col[:, None] * inv], axis=-1)
    cos, sin = jnp.cos(ang), jnp.sin(ang)
    z = jnp.zeros((seq, half), F32)
    zpad = jnp.zeros((seq, LANE - 2 * half), F32)
    c = jnp.concatenate([cos, cos, zpad], axis=-1)
    s_left = jnp.concatenate([-sin, z, zpad], axis=-1)
    s_right = jnp.concatenate([z, sin, zpad], axis=-1)
    tab = jnp.concatenate([c, s_left, s_right], axis=-1)
    ident = jnp.concatenate([jnp.ones((tm, 2 * half), F32), jnp.zeros((tm, 3 * LANE - 2 * half), F32)], axis=-1)
    return jnp.concatenate([ident, tab], axis=0)


def _ada_mod(cvec, ada_w, ada_b, layer):
    n = cvec.shape[0]
    m_pad = 16
    a = jnp.zeros((m_pad, D_MODEL), F32).at[:n].set(jax.nn.silu(cvec)).astype(BF16)
    n_cols = ada_w.shape[-1]
    tn = 1024
    (mod,) = _matmul(
        a, ada_w, w_lead=(layer,), n_cols=n_cols, tm=m_pad, tn=tn, epilogue=_ep_bias,
        extras=[(ada_b[:, None, :], (None, 1, tn), lambda j, i: (layer, 0, j))],
        outs=[(jax.ShapeDtypeStruct((m_pad, n_cols), F32), (m_pad, tn), lambda j, i: (i, j))],
        name="ada_mod")
    return mod[:n, None, :]


def kernel(x, c, ctx, c_ctx, ada_w, ada_b, norm1_g, norm2_g, ab_w_in, ab_w_out, na_q_g, na_k_g, na_rel_bias,
           sg_norm_g, sg_norm_b, sg_w, sg_b, mla_w_in, mla_q_norm_g, mla_kv_norm_g, mla_w_uq, mla_w_ukv,
           mla_q_g, mla_k_g, mla_w_out, moe_w_router, moe_b_router, moe_w_gu, moe_b_gu, moe_w_down, moe_b_down):
    n_batch, seq, d = x.shape
    t_ctx = ctx.shape[1]
    n_ctx = n_batch * t_ctx
    n_lat = n_batch * seq
    n_all = n_ctx + n_lat
    tm = 512
    tn = 512
    cvec = jnp.concatenate([c_ctx[None, :], c], axis=0)
    xs = jnp.concatenate([ctx.reshape(n_ctx, d), x.reshape(n_lat, d)], axis=0)
    mrow = functools.partial(_mod_row, tm=tm, n_ctx_rows=n_ctx, seq=seq)

    mod = _ada_mod(cvec, ada_w, ada_b, 0)
    h = _norm_mod(xs, norm1_g[0], mod, 0, 1, n_ctx, seq)
    w_in = ab_w_in
    q_gain = (na_q_g[0] * (NA_HEAD_DIM ** -0.5)).reshape(1, NA_HEAD_DIM)
    k_gain = na_k_g[0].reshape(1, NA_HEAD_DIM)

    def in_proj(col0, n_cols, epilogue, extras, name):
        (o,) = _matmul(h, w_in, w_lead=(0,), col0=col0, n_cols=n_cols, tm=tm, tn=tn, epilogue=epilogue,
                       extras=extras,
                       outs=[(jax.ShapeDtypeStruct((n_all, n_cols), BF16), (tm, tn), lambda j, i: (i, j))],
                       name=name)
        return o

    gain_spec = lambda g: [(g, (1, NA_HEAD_DIM), lambda j, i: (0, 0))]
    q = in_proj(0, NA_WIDTH, _ep_head_rms, gain_spec(q_gain), "ab_in_q")
    k = in_proj(NA_WIDTH, NA_WIDTH, _ep_head_rms, gain_spec(k_gain), "ab_in_k")
    v = in_proj(2 * NA_WIDTH, NA_WIDTH, _ep_cast, [], "ab_in_v")
    uv = in_proj(3 * NA_WIDTH, 2 * SG_WIDTH, _ep_gelu, [], "ab_in_uv")

    a_lat = _na_attention(q, k, v, na_rel_bias[0], n_batch=n_batch, seq=seq, t_ctx=t_ctx)
    a_ctx = _attention(q, k, v, n_batch=n_batch, n_heads=NA_HEADS, dq=NA_HEAD_DIM, dv=NA_HEAD_DIM,
                       tq=t_ctx, n_q_tiles=1, q_blk0=0, t_k=t_ctx, k_blk0=0, name="ctx_attention")
    gated = _spatial_gating(uv, sg_norm_g[0], sg_norm_b[0], sg_w[0], sg_b[0])
    mix = jnp.concatenate([jnp.concatenate([a_ctx, a_lat], axis=0), gated], axis=1)

    def out_proj(inp, w, resid, n_ctx_rows, name):
        mr = functools.partial(_mod_row, tm=tm, n_ctx_rows=n_ctx_rows, seq=seq)
        (o,) = _matmul(inp, w, w_lead=(0,), n_cols=d, tm=tm, tn=tn, epilogue=_ep_residual,
                       extras=[(resid, (tm, tn), lambda j, i: (i, j)),
                               (mod, (None, 1, tn), lambda j, i: (mr(i), 0, 2 * d // tn + j))],
                       outs=[(jax.ShapeDtypeStruct(resid.shape, F32), (tm, tn), lambda j, i: (i, j))],
                       name=name)
        return o

    xs = out_proj(mix, ab_w_out, xs, n_ctx, "ab_out")
    h2, logits = _norm_mod(xs, norm2_g[0], mod, 3, 4, n_ctx, seq, router=(moe_w_router[0], moe_b_router[0]))
    f = _moe(h2, logits, 0, moe_w_gu, moe_b_gu, moe_w_down, moe_b_down)
    g2 = mod[:, 0, 5 * d:6 * d]
    g2_rows = jnp.concatenate([jnp.broadcast_to(g2[0], (n_ctx, d)),
                               jnp.broadcast_to(g2[1:, None, :], (n_batch, seq, d)).reshape(n_lat, d)], axis=0)
    xs = xs + g2_rows * f

    mod = _ada_mod(cvec, ada_w, ada_b, 1)
    h = _norm_mod(xs, norm1_g[1], mod, 0, 1, n_ctx, seq)
    w_in_pad = jnp.pad(mla_w_in, ((0, 0), (0, 0), (0, MLA_IN_PAD - mla_w_in.shape[-1])))
    c_q, c_kv, k_pe = _matmul(
        h, w_in_pad, w_lead=(0,), n_cols=MLA_IN_PAD, tm=tm, tn=MLA_IN_PAD, epilogue=_ep_mla_in,
        extras=[(mla_q_norm_g[0].reshape(1, Q_LORA), (1, Q_LORA), lambda j, i: (0, 0)),
                (mla_kv_norm_g[0].reshape(1, KV_LORA), (1, KV_LORA), lambda j, i: (0, 0))],
        outs=[(jax.ShapeDtypeStruct((n_all, Q_LORA), BF16), (tm, Q_LORA), lambda j, i: (i, 0)),
              (jax.ShapeDtypeStruct((n_all, KV_LORA), BF16), (tm, KV_LORA), lambda j, i: (i, 0)),
              (jax.ShapeDtypeStruct((n_all, LANE), F32), (tm, LANE), lambda j, i: (i, 0))],
        name="mla_in")

    rope_tab = _axial_rope_tables(seq, tm)
    n_ctx_tiles = n_ctx // tm
    seq_tiles = seq // tm
    pad_gain = lambda g, s: jnp.pad(g * s, (0, QK_PAD - QK_DIM)).reshape(1, QK_PAD)
    w_uq = jnp.pad(mla_w_uq[0].reshape(Q_LORA, MLA_HEADS, QK_DIM),
                   ((0, 0), (0, 0), (0, QK_PAD - QK_DIM))).reshape(Q_LORA, MLA_HEADS * QK_PAD)
    tn_up = 1024
    (q,) = _matmul(
        c_q[n_ctx:], w_uq, n_cols=MLA_HEADS * QK_PAD, tm=tm, tn=tn_up, epilogue=_ep_mla_q,
        extras=[(pad_gain(mla_q_g[0], QK_DIM ** -0.5), (1, QK_PAD), lambda j, i: (0, 0)),
                (rope_tab, (tm, 3 * LANE), lambda j, i: (1 + i % seq_tiles, 0))],
        outs=[(jax.ShapeDtypeStruct((n_lat, MLA_HEADS * QK_PAD), BF16), (tm, tn_up), lambda j, i: (i, j))],
        name="mla_up_q")
    heads_per_tile = tn_up // (QK_NOPE + V_DIM)
    k, v = _matmul(
        c_kv, mla_w_ukv, w_lead=(0,), n_cols=MLA_HEADS * (QK_NOPE + V_DIM), tm=tm, tn=tn_up, epilogue=_ep_mla_kv,
        extras=[(pad_gain(mla_k_g[0], 1.0), (1, QK_PAD), lambda j, i: (0, 0)),
                (rope_tab, (tm, 3 * LANE),
                 lambda j, i: (jnp.where(i < n_ctx_tiles, 0, 1 + (i - n_ctx_tiles) % seq_tiles), 0)),
                (k_pe, (tm, LANE), lambda j, i: (i, 0))],
        outs=[(jax.ShapeDtypeStruct((n_all, MLA_HEADS * QK_PAD), BF16), (tm, heads_per_tile * QK_PAD),
               lambda j, i: (i, j)),
              (jax.ShapeDtypeStruct((n_all, MLA_HEADS * V_DIM), BF16), (tm, heads_per_tile * V_DIM),
               lambda j, i: (i, j))],
        name="mla_up_kv")

    tq = 512
    attn = _attention(q, k, v, n_batch=n_batch, n_heads=MLA_HEADS, dq=QK_PAD, dv=V_DIM, tq=tq,
                      n_q_tiles=seq // tq, q_blk0=0, t_k=seq, k_blk0=n_ctx // seq, t_ctx=t_ctx,
                      name="mla_attention")
    x_lat = out_proj(attn, mla_w_out, xs[n_ctx:], 0, "mla_out")
    h2, logits = _norm_mod(x_lat, norm2_g[1], mod, 3, 4, 0, seq, router=(moe_w_router[1], moe_b_router[1]))
    f = _moe(h2, logits, 1, moe_w_gu, moe_b_gu, moe_w_down, moe_b_down)
    g2 = mod[1:, :, 5 * d:6 * d]
    return x_lat.reshape(n_batch, seq, d) + g2 * f.reshape(n_batch, seq, d)
```

```python
import functools

import numpy as np
import jax
import jax.numpy as jnp
from jax import lax
from jax.experimental import pallas as pl
from jax.experimental.pallas import tpu as pltpu

F32 = jnp.float32
BF16 = jnp.bfloat16

D_MODEL = 2048
GRID_W = 64
EPS = 1e-6
NEG = -1e30

NA_HEADS = 8
NA_HEAD_DIM = 128
NA_WIDTH = NA_HEADS * NA_HEAD_DIM
NA_WIN_H = 8
NA_WIN_W = 16
SG_GROUPS = 8
SG_WIDTH = D_MODEL // 2
SG_GROUP_DIM = SG_WIDTH // SG_GROUPS
SG_CHUNK = 128

MLA_HEADS = 16
Q_LORA = 512
KV_LORA = 256
QK_NOPE = 128
QK_ROPE = 64
V_DIM = 128
QK_DIM = QK_NOPE + QK_ROPE
QK_PAD = 256
MLA_IN_PAD = Q_LORA + KV_LORA + 128
ROPE_THETA = 10000.0

N_EXPERTS = 32
TOP_K = 4
D_FF = D_MODEL
SWIGLU_ALPHA = 1.702
SWIGLU_LIMIT = 7.0

LANE = 128
V7X_VMEM_LIMIT = 56 * 1024 * 1024

NA_TILE_ROWS = 4
NA_BAND_ROWS = NA_TILE_ROWS + NA_WIN_H - 1
MOE_TM = 512
MOE_TN_GU = 512
MOE_TN_DOWN = 1024


def _params(n_axes):
    return pltpu.CompilerParams(dimension_semantics=("arbitrary",) * n_axes,
                                vmem_limit_bytes=V7X_VMEM_LIMIT)


def _norm_mod_body(x_ref, g_ref, sh_ref, sc_ref):
    x = x_ref[...]
    y = x * lax.rsqrt(jnp.mean(x * x, axis=-1, keepdims=True) + EPS) * g_ref[...]
    return y * (1.0 + sc_ref[...]) + sh_ref[...]


def _norm_mod_kernel(x_ref, g_ref, sh_ref, sc_ref, o_ref):
    o_ref[...] = _norm_mod_body(x_ref, g_ref, sh_ref, sc_ref).astype(o_ref.dtype)


def _norm_mod_router_kernel(x_ref, g_ref, sh_ref, sc_ref, wr_ref, br_ref, o_ref, lg_ref):
    h = _norm_mod_body(x_ref, g_ref, sh_ref, sc_ref)
    o_ref[...] = h.astype(o_ref.dtype)
    lg_ref[...] = jnp.dot(h, wr_ref[...], precision=lax.Precision.HIGHEST,
                          preferred_element_type=F32) + br_ref[...]


def _mod_row(i, tm, n_ctx_rows, seq):
    n_ctx_tiles = n_ctx_rows // tm
    return jnp.where(i < n_ctx_tiles, 0, 1 + (i - n_ctx_tiles) // (seq // tm))


def _norm_mod(x, gain, mod, k_shift, k_scale, n_ctx_rows, seq, router=None, tm=256):
    rows, d = x.shape
    mrow = functools.partial(_mod_row, tm=tm, n_ctx_rows=n_ctx_rows, seq=seq)
    in_specs = [
        pl.BlockSpec((tm, d), lambda i: (i, 0)),
        pl.BlockSpec((1, d), lambda i: (0, 0)),
        pl.BlockSpec((None, 1, d), lambda i: (mrow(i), 0, k_shift)),
        pl.BlockSpec((None, 1, d), lambda i: (mrow(i), 0, k_scale)),
    ]
    args = [x, gain.reshape(1, d), mod, mod]
    if router is None:
        return pl.pallas_call(
            _norm_mod_kernel, grid=(rows // tm,), in_specs=in_specs,
            out_specs=pl.BlockSpec((tm, d), lambda i: (i, 0)),
            out_shape=jax.ShapeDtypeStruct((rows, d), BF16),
            compiler_params=_params(1), name="norm_mod")(*args)
    w_r, b_r = router
    n_e = w_r.shape[1]
    in_specs += [pl.BlockSpec((d, n_e), lambda i: (0, 0)), pl.BlockSpec((1, n_e), lambda i: (0, 0))]
    return pl.pallas_call(
        _norm_mod_router_kernel, grid=(rows // tm,), in_specs=in_specs,
        out_specs=[pl.BlockSpec((tm, d), lambda i: (i, 0)), pl.BlockSpec((tm, n_e), lambda i: (i, 0))],
        out_shape=[jax.ShapeDtypeStruct((rows, d), BF16), jax.ShapeDtypeStruct((rows, n_e), F32)],
        compiler_params=_params(1), name="norm_mod_router")(*args, w_r, b_r.reshape(1, n_e))


def _matmul_kernel(*refs, n_extra, n_out, epilogue):
    x_ref, w_ref = refs[:2]
    extra = refs[2:2 + n_extra]
    outs = refs[2 + n_extra:2 + n_extra + n_out]
    wbf_ref = refs[-1]

    @pl.when(pl.program_id(1) == 0)
    def _():
        wbf_ref[...] = w_ref[...].astype(BF16)

    acc = jnp.dot(x_ref[...], wbf_ref[...], preferred_element_type=F32)
    epilogue(acc, extra, outs)


def _matmul(x, w, *, w_lead=(), col0=0, n_cols, tm, tn, epilogue, extras=(), outs, name, row0=0, m=None):
    k = x.shape[1]
    m = x.shape[0] - row0 if m is None else m
    assert m % tm == 0 and row0 % tm == 0 and n_cols % tn == 0 and col0 % tn == 0
    lead = tuple(w_lead)
    w_spec = pl.BlockSpec((None,) * len(lead) + (k, tn), lambda j, i: lead + (0, col0 // tn + j))
    in_specs = [pl.BlockSpec((tm, k), lambda j, i: (row0 // tm + i, 0)), w_spec]
    in_specs += [pl.BlockSpec(bs, im) for _, bs, im in extras]
    out_specs = [pl.BlockSpec(bs, im) for _, bs, im in outs]
    kern = functools.partial(_matmul_kernel, n_extra=len(extras), n_out=len(outs), epilogue=epilogue)
    res = pl.pallas_call(
        kern, grid=(n_cols // tn, m // tm), in_specs=in_specs, out_specs=out_specs,
        out_shape=[s for s, _, _ in outs],
        scratch_shapes=[pltpu.VMEM((k, tn), BF16)],
        compiler_params=_params(2), name=name)(x, w, *[a for a, _, _ in extras])
    return res


def _ep_bias(acc, extra, outs):
    outs[0][...] = acc + extra[0][...]


def _ep_cast(acc, extra, outs):
    outs[0][...] = acc.astype(outs[0].dtype)


def _ep_gelu(acc, extra, outs):
    outs[0][...] = jax.nn.gelu(acc).astype(outs[0].dtype)


def _ep_head_rms(acc, extra, outs):
    g = extra[0][...]
    for h in range(acc.shape[1] // LANE):
        a = acc[:, h * LANE:(h + 1) * LANE]
        r = lax.rsqrt(jnp.mean(a * a, axis=-1, keepdims=True) + EPS)
        outs[0][:, h * LANE:(h + 1) * LANE] = (a * r * g).astype(outs[0].dtype)


def _ep_residual(acc, extra, outs):
    outs[0][...] = extra[0][...] + extra[1][...] * acc


def _rms(a, g):
    return a * lax.rsqrt(jnp.mean(a * a, axis=-1, keepdims=True) + EPS) * g


def _ep_mla_in(acc, extra, outs):
    outs[0][...] = _rms(acc[:, :Q_LORA], extra[0][...]).astype(BF16)
    outs[1][...] = _rms(acc[:, Q_LORA:Q_LORA + KV_LORA], extra[1][...]).astype(BF16)
    outs[2][...] = acc[:, Q_LORA + KV_LORA:]


def _rope_tail(t, tab_ref):
    c = tab_ref[:, 0:LANE]
    s_left = tab_ref[:, LANE:2 * LANE]
    s_right = tab_ref[:, 2 * LANE:3 * LANE]
    return (t * c + pltpu.roll(t, LANE - QK_ROPE // 2, axis=1) * s_left
            + pltpu.roll(t, QK_ROPE // 2, axis=1) * s_right)


def _ep_mla_q(acc, extra, outs):
    g_ref, tab_ref = extra
    g0 = g_ref[:, :LANE]
    g1 = g_ref[:, LANE:]
    for h in range(acc.shape[1] // QK_PAD):
        a0 = acc[:, h * QK_PAD:h * QK_PAD + LANE]
        a1 = acc[:, h * QK_PAD + LANE:(h + 1) * QK_PAD]
        ss = jnp.sum(a0 * a0, axis=-1, keepdims=True) + jnp.sum(a1 * a1, axis=-1, keepdims=True)
        r = lax.rsqrt(ss * (1.0 / QK_DIM) + EPS)
        outs[0][:, h * QK_PAD:h * QK_PAD + LANE] = (a0 * r * g0).astype(BF16)
        outs[0][:, h * QK_PAD + LANE:(h + 1) * QK_PAD] = _rope_tail(a1 * r * g1, tab_ref).astype(BF16)


def _ep_mla_kv(acc, extra, outs):
    g_ref, tab_ref, pe_ref = extra
    k_out, v_out = outs
    g0 = g_ref[:, :LANE]
    g1 = g_ref[:, LANE:]
    pe = pe_ref[...]
    pe_ss = jnp.sum(pe * pe, axis=-1, keepdims=True)
    for h in range(acc.shape[1] // (QK_NOPE + V_DIM)):
        base = h * (QK_NOPE + V_DIM)
        kn = acc[:, base:base + QK_NOPE]
        r = lax.rsqrt((jnp.sum(kn * kn, axis=-1, keepdims=True) + pe_ss) * (1.0 / QK_DIM) + EPS)
        k_out[:, h * QK_PAD:h * QK_PAD + LANE] = (kn * r * g0).astype(BF16)
        k_out[:, h * QK_PAD + LANE:(h + 1) * QK_PAD] = _rope_tail(pe * r * g1, tab_ref).astype(BF16)
        v_out[:, h * V_DIM:(h + 1) * V_DIM] = acc[:, base + QK_NOPE:base + QK_NOPE + V_DIM].astype(BF16)


def _softmax_pv(s_parts, v_parts):
    m = s_parts[0].max(axis=-1, keepdims=True)
    for s in s_parts[1:]:
        m = jnp.maximum(m, s.max(axis=-1, keepdims=True))
    l = 0.0
    o = 0.0
    for s, v in zip(s_parts, v_parts):
        p = jnp.exp(s - m)
        l = l + p.sum(axis=-1, keepdims=True)
        o = o + jnp.dot(p.astype(BF16), v, preferred_element_type=F32)
    return o / l


def _qk(q, k):
    return lax.dot_general(q, k, (((1,), (1,)), ((), ())), preferred_element_type=F32)


def _attn_kernel(*refs, has_ctx, n_sub):
    if has_ctx:
        q_ref, k_ref, v_ref, kc_ref, vc_ref, o_ref = refs
    else:
        q_ref, k_ref, v_ref, o_ref = refs
    sub = q_ref.shape[0] // n_sub

    def scores(r):
        q = q_ref[r * sub:(r + 1) * sub, :]
        parts = [_qk(q, k_ref[...])]
        if has_ctx:
            parts.append(_qk(q, kc_ref[...]))
        return parts

    s_next = scores(0)
    for r in range(n_sub):
        s_cur = s_next
        if r + 1 < n_sub:
            s_next = scores(r + 1)
        v_parts = [v_ref[...]] + ([vc_ref[...]] if has_ctx else [])
        o_ref[r * sub:(r + 1) * sub, :] = _softmax_pv(s_cur, v_parts).astype(o_ref.dtype)


def _attention(q, k, v, *, n_batch, n_heads, dq, dv, tq, n_q_tiles, q_blk0, t_k, k_blk0, t_ctx=None, n_sub=1, name):
    has_ctx = t_ctx is not None
    in_specs = [
        pl.BlockSpec((tq, dq), lambda b, h, i: (q_blk0 + b * n_q_tiles + i, h)),
        pl.BlockSpec((t_k, dq), lambda b, h, i: (k_blk0 + b, h)),
        pl.BlockSpec((t_k, dv), lambda b, h, i: (k_blk0 + b, h)),
    ]
    args = [q, k, v]
    if has_ctx:
        in_specs += [pl.BlockSpec((t_ctx, dq), lambda b, h, i: (b, h)),
                     pl.BlockSpec((t_ctx, dv), lambda b, h, i: (b, h))]
        args += [k, v]
    return pl.pallas_call(
        functools.partial(_attn_kernel, has_ctx=has_ctx, n_sub=n_sub),
        grid=(n_batch, n_heads, n_q_tiles), in_specs=in_specs,
        out_specs=pl.BlockSpec((tq, dv), lambda b, h, i: (b * n_q_tiles + i, h)),
        out_shape=jax.ShapeDtypeStruct((n_batch * n_q_tiles * tq, n_heads * dv), BF16),
        compiler_params=_params(3), name=name)(*args)


def _na_band_start(t, rows):
    return jnp.clip(t * NA_TILE_ROWS - NA_WIN_H // 2, 0, rows - NA_BAND_ROWS)


def _na_kernel(q_ref, k_ref, v_ref, kc_ref, vc_ref, bias_ref, o_ref, *, rows):
    t = pl.program_id(1)
    ks = pl.multiple_of(_na_band_start(t, rows) * GRID_W, GRID_W)
    for h in range(NA_HEADS):
        cols = slice(h * NA_HEAD_DIM, (h + 1) * NA_HEAD_DIM)
        q = q_ref[:, cols]
        k_band = k_ref[pl.ds(ks, NA_BAND_ROWS * GRID_W), cols]
        v_band = v_ref[pl.ds(ks, NA_BAND_ROWS * GRID_W), cols]
        s_loc = _qk(q, k_band) + bias_ref[h]
        s_ctx = _qk(q, kc_ref[:, cols])
        o_ref[:, cols] = _softmax_pv([s_loc, s_ctx], [v_band, vc_ref[:, cols]]).astype(o_ref.dtype)


def _na_bias_tables(rel_bias, rows):
    n_tiles = rows // NA_TILE_ROWS
    qr = np.arange(NA_TILE_ROWS)[:, None, None, None]
    qc = np.arange(GRID_W)[None, :, None, None]
    ur = np.arange(NA_BAND_ROWS)[None, None, :, None]
    kc = np.arange(GRID_W)[None, None, None, :]
    geoms, ids = [], []
    for t in range(n_tiles):
        u0 = int(np.clip(t * NA_TILE_ROWS - NA_WIN_H // 2, 0, rows - NA_BAND_ROWS))
        r = t * NA_TILE_ROWS + qr
        key_row = u0 + ur
        r0 = np.clip(r - NA_WIN_H // 2, 0, rows - NA_WIN_H)
        c0 = np.clip(qc - NA_WIN_W // 2, 0, GRID_W - NA_WIN_W)
        valid = (key_row >= r0) & (key_row < r0 + NA_WIN_H) & (kc >= c0) & (kc < c0 + NA_WIN_W)
        ridx = np.clip(key_row - r + NA_WIN_H - 1, 0, 2 * NA_WIN_H - 2)
        cidx = np.clip(kc - qc + NA_WIN_W - 1, 0, 2 * NA_WIN_W - 2)
        shape = (NA_TILE_ROWS, GRID_W, NA_BAND_ROWS, GRID_W)
        geom = (np.broadcast_to(valid, shape), np.broadcast_to(ridx, shape), np.broadcast_to(cidx, shape))
        for gi, g in enumerate(geoms):
            if all(np.array_equal(a, b) for a, b in zip(g, geom)):
                ids.append(gi)
                break
        else:
            ids.append(len(geoms))
            geoms.append(geom)
    nq = NA_TILE_ROWS * GRID_W
    nk = NA_BAND_ROWS * GRID_W
    valid = np.stack([g[0].reshape(nq, nk) for g in geoms])
    ridx = np.stack([g[1].reshape(nq, nk) for g in geoms])
    cidx = np.stack([g[2].reshape(nq, nk) for g in geoms])
    bias = rel_bias.astype(F32)[:, ridx, cidx]
    tables = jnp.where(valid[None], bias, NEG).transpose(1, 0, 2, 3)
    return tables, ids


def _na_attention(q, k, v, rel_bias, *, n_batch, seq, t_ctx):
    rows = seq // GRID_W
    n_tiles = rows // NA_TILE_ROWS
    tq = NA_TILE_ROWS * GRID_W
    tables, ids = _na_bias_tables(rel_bias, rows)
    assert ids == [0] + [1] * (n_tiles - 2) + [2], ids
    ctx_tiles = n_batch * t_ctx // tq
    ctx_units = n_batch * t_ctx // seq

    def table_id(t):
        return jnp.where(t == 0, 0, jnp.where(t == n_tiles - 1, 2, 1))

    return pl.pallas_call(
        functools.partial(_na_kernel, rows=rows),
        grid=(n_batch, n_tiles),
        in_specs=[
            pl.BlockSpec((tq, NA_WIDTH), lambda b, t: (ctx_tiles + b * n_tiles + t, 0)),
            pl.BlockSpec((seq, NA_WIDTH), lambda b, t: (ctx_units + b, 0)),
            pl.BlockSpec((seq, NA_WIDTH), lambda b, t: (ctx_units + b, 0)),
            pl.BlockSpec((t_ctx, NA_WIDTH), lambda b, t: (b, 0)),
            pl.BlockSpec((t_ctx, NA_WIDTH), lambda b, t: (b, 0)),
            pl.BlockSpec((None, NA_HEADS, tq, NA_BAND_ROWS * GRID_W), lambda b, t: (table_id(t), 0, 0, 0)),
        ],
        out_specs=pl.BlockSpec((tq, NA_WIDTH), lambda b, t: (b * n_tiles + t, 0)),
        out_shape=jax.ShapeDtypeStruct((n_batch * seq, NA_WIDTH), BF16),
        compiler_params=_params(2), name="na_attention")(q, k, v, k, v, tables)


def _sg_kernel(uv_ref, g_ref, b_ref, ws_ref, bs_ref, o_ref, *, chunks):
    for c in range(chunks):
        r = slice(c * SG_CHUNK, (c + 1) * SG_CHUNK)
        z = uv_ref[r, SG_WIDTH:].astype(F32)
        mu = jnp.mean(z, axis=-1, keepdims=True)
        zc = z - mu
        var = jnp.mean(zc * zc, axis=-1, keepdims=True)
        zn = (zc * lax.rsqrt(var + EPS) * g_ref[...] + b_ref[...]).astype(BF16)
        for g in range(SG_GROUPS):
            cols = slice(g * SG_GROUP_DIM, (g + 1) * SG_GROUP_DIM)
            mixed = jnp.dot(ws_ref[g], zn[:, cols], preferred_element_type=F32) + bs_ref[g]
            o_ref[r, cols] = (uv_ref[r, cols].astype(F32) * mixed).astype(o_ref.dtype)


def _spatial_gating(uv, ln_g, ln_b, w_s, b_s, tm=512):
    rows = uv.shape[0]
    bs = jnp.broadcast_to(b_s.astype(F32)[:, :, None], (SG_GROUPS, SG_CHUNK, SG_GROUP_DIM))
    return pl.pallas_call(
        functools.partial(_sg_kernel, chunks=tm // SG_CHUNK),
        grid=(rows // tm,),
        in_specs=[
            pl.BlockSpec((tm, 2 * SG_WIDTH), lambda i: (i, 0)),
            pl.BlockSpec((1, SG_WIDTH), lambda i: (0, 0)),
            pl.BlockSpec((1, SG_WIDTH), lambda i: (0, 0)),
            pl.BlockSpec((SG_GROUPS, SG_CHUNK, SG_CHUNK), lambda i: (0, 0, 0)),
            pl.BlockSpec((SG_GROUPS, SG_CHUNK, SG_GROUP_DIM), lambda i: (0, 0, 0)),
        ],
        out_specs=pl.BlockSpec((tm, SG_WIDTH), lambda i: (i, 0)),
        out_shape=jax.ShapeDtypeStruct((rows, SG_WIDTH), BF16),
        compiler_params=_params(1), name="spatial_gating")(
            uv, ln_g.reshape(1, SG_WIDTH), ln_b.reshape(1, SG_WIDTH), w_s.astype(BF16), bs)


def _tile_weights_changed(i, te_ref):
    return (i == 0) | (te_ref[i] != te_ref[jnp.maximum(i - 1, 0)])


def _swiglu(g, u):
    g = jnp.minimum(g, SWIGLU_LIMIT)
    u = jnp.clip(u, -SWIGLU_LIMIT, SWIGLU_LIMIT)
    return (u + 1.0) * (g * jax.nn.sigmoid(SWIGLU_ALPHA * g))


def _for_each_valid_half(i, tv_ref, o_ref, compute):
    half = o_ref.shape[0] // 2
    for s in range(2):
        rows = slice(s * half, (s + 1) * half)

        @pl.when(tv_ref[i] > s * half)
        def _(rows=rows):
            o_ref[rows, :] = compute(rows).astype(o_ref.dtype)

        @pl.when(tv_ref[i] <= s * half)
        def _(rows=rows):
            o_ref[rows, :] = jnp.zeros((half, o_ref.shape[1]), o_ref.dtype)


def _moe_gu_kernel(te_ref, tv_ref, ts_ref, x_ref, wg_ref, wu_ref, bg_ref, bu_ref, o_ref, wg_bf, wu_bf):
    i = pl.program_id(1)

    @pl.when(_tile_weights_changed(i, te_ref))
    def _():
        wg_bf[...] = wg_ref[...].astype(BF16)
        wu_bf[...] = wu_ref[...].astype(BF16)

    def compute(rows):
        x = x_ref[rows, :]
        g = jnp.dot(x, wg_bf[...], preferred_element_type=F32) + bg_ref[...]
        u = jnp.dot(x, wu_bf[...], preferred_element_type=F32) + bu_ref[...]
        return _swiglu(g, u)

    _for_each_valid_half(i, tv_ref, o_ref, compute)


def _moe_down_kernel(te_ref, tv_ref, ts_ref, h_ref, w_ref, b_ref, o_ref, w_bf):
    i = pl.program_id(1)

    @pl.when(_tile_weights_changed(i, te_ref))
    def _():
        w_bf[...] = w_ref[...].astype(BF16)

    def compute(rows):
        return jnp.dot(h_ref[rows, :], w_bf[...], preferred_element_type=F32) + b_ref[...]

    _for_each_valid_half(i, tv_ref, o_ref, compute)


def _moe_experts(x_sorted, tile_e, tile_valid, tile_src, layer, w_gu, b_gu, w_down, b_down):
    r_pad, d = x_sorted.shape
    tm, tn, tn_down = MOE_TM, MOE_TN_GU, MOE_TN_DOWN
    n_tiles = r_pad // tm
    n_j = D_FF // tn
    x_map = lambda j, i, te, tv, ts: (ts[i], 0)
    out_map = lambda j, i, te, tv, ts: (i, j)

    hid = pl.pallas_call(
        _moe_gu_kernel,
        grid_spec=pltpu.PrefetchScalarGridSpec(
            num_scalar_prefetch=3, grid=(n_j, n_tiles),
            in_specs=[
                pl.BlockSpec((tm, d), x_map),
                pl.BlockSpec((None, None, d, tn), lambda j, i, te, tv, ts: (layer, te[i], 0, j)),
                pl.BlockSpec((None, None, d, tn), lambda j, i, te, tv, ts: (layer, te[i], 0, n_j + j)),
                pl.BlockSpec((None, None, 1, tn), lambda j, i, te, tv, ts: (layer, te[i], 0, j)),
                pl.BlockSpec((None, None, 1, tn), lambda j, i, te, tv, ts: (layer, te[i], 0, n_j + j)),
            ],
            out_specs=pl.BlockSpec((tm, tn), out_map),
            scratch_shapes=[pltpu.VMEM((d, tn), BF16), pltpu.VMEM((d, tn), BF16)]),
        out_shape=jax.ShapeDtypeStruct((r_pad, D_FF), BF16),
        compiler_params=_params(2), name="moe_gate_up")(
            tile_e, tile_valid, tile_src, x_sorted, w_gu, w_gu, b_gu[:, :, None, :], b_gu[:, :, None, :])

    return pl.pallas_call(
        _moe_down_kernel,
        grid_spec=pltpu.PrefetchScalarGridSpec(
            num_scalar_prefetch=3, grid=(d // tn_down, n_tiles),
            in_specs=[
                pl.BlockSpec((tm, D_FF), x_map),
                pl.BlockSpec((None, None, D_FF, tn_down), lambda j, i, te, tv, ts: (layer, te[i], 0, j)),
                pl.BlockSpec((None, None, 1, tn_down), lambda j, i, te, tv, ts: (layer, te[i], 0, j)),
            ],
            out_specs=pl.BlockSpec((tm, tn_down), out_map),
            scratch_shapes=[pltpu.VMEM((D_FF, tn_down), BF16)]),
        out_shape=jax.ShapeDtypeStruct((r_pad, d), BF16),
        compiler_params=_params(2), name="moe_down")(
            tile_e, tile_valid, tile_src, hid, w_down, b_down[:, :, None, :])


def _moe_route(logits):
    n = logits.shape[0]
    nk = n * TOP_K
    tm = MOE_TM
    i32 = jnp.int32
    top_val, top_idx = lax.top_k(logits, TOP_K)
    gate = jax.nn.softmax(top_val, axis=-1)
    flat_e = top_idx.reshape(nk).astype(i32)
    iota = jnp.arange(nk, dtype=i32)
    _, order = lax.sort((flat_e, iota), num_keys=1)
    _, inv_order = lax.sort((order, iota), num_keys=1)
    onehot = flat_e[:, None] == jnp.arange(N_EXPERTS, dtype=i32)[None, :]
    counts = jnp.sum(onehot, axis=0, dtype=i32)
    padded = (counts + tm - 1) // tm * tm
    pad_end = jnp.cumsum(padded)
    pad_start = pad_end - padded
    start = jnp.cumsum(counts) - counts
    pos = inv_order + jnp.sum(jnp.where(onehot, (pad_start - start)[None, :], 0), axis=1)
    pos = pos.reshape(n, TOP_K).T.reshape(nk)

    n_tiles = nk // tm + N_EXPERTS
    n_used = pad_end[-1] // tm
    tile_src = jnp.minimum(jnp.arange(n_tiles, dtype=i32), n_used - 1)
    tile_e = jnp.minimum(jnp.sum(pad_end[None, :] <= (tile_src * tm)[:, None], axis=1, dtype=i32), N_EXPERTS - 1)
    in_use = jnp.arange(n_tiles, dtype=i32) < n_used
    rank0 = tile_src * tm - pad_start[tile_e]
    tile_valid = jnp.where(in_use, jnp.clip(counts[tile_e] - rank0, 0, tm), 0).astype(i32)
    within = jnp.arange(tm, dtype=i32)[None, :]
    valid = within < tile_valid[:, None]
    src = jnp.clip((start[tile_e] + rank0)[:, None] + within, 0, nk - 1)
    rows_tok = jnp.where(valid, order[src] // TOP_K, 0).reshape(n_tiles * tm)
    return gate, rows_tok, pos, tile_e, tile_valid, tile_src.astype(i32)


def _moe_combine_kernel(x_ref, g2_ref, gate_ref, y_ref, o_ref):
    gate = gate_ref[...]
    f = gate[:, 0:1] * y_ref[0].astype(F32)
    for k in range(1, TOP_K):
        f = f + gate[:, k:k + 1] * y_ref[k].astype(F32)
    o_ref[...] = x_ref[...] + g2_ref[...] * f


def _moe_block(xs, h2, logits, mod, layer, n_ctx_rows, seq, w_gu, b_gu, w_down, b_down, tm=256):
    n, d = h2.shape
    gate, rows_tok, pos, tile_e, tile_valid, tile_src = _moe_route(logits)
    x_sorted = jnp.take(h2, rows_tok, axis=0)
    y = _moe_experts(x_sorted, tile_e, tile_valid, tile_src, layer, w_gu, b_gu, w_down, b_down)
    y_k = jnp.take(y, pos, axis=0).reshape(TOP_K, n, d)
    mrow = functools.partial(_mod_row, tm=tm, n_ctx_rows=n_ctx_rows, seq=seq)
    return pl.pallas_call(
        _moe_combine_kernel, grid=(n // tm,),
        in_specs=[
            pl.BlockSpec((tm, d), lambda i: (i, 0)),
            pl.BlockSpec((None, 1, d), lambda i: (mrow(i), 0, 5)),
            pl.BlockSpec((tm, TOP_K), lambda i: (i, 0)),
            pl.BlockSpec((TOP_K, tm, d), lambda i: (0, i, 0)),
        ],
        out_specs=pl.BlockSpec((tm, d), lambda i: (i, 0)),
        out_shape=jax.ShapeDtypeStruct((n, d), F32),
        compiler_params=_params(1), name="moe_combine")(xs, mod, gate, y_k)


def _axial_rope_tables(seq, tm):
    half = QK_ROPE // 2
    t = jnp.arange(seq)
    row = (t // GRID_W).astype(F32)
    col = (t % GRID_W).astype(F32)
    n_freq = QK_ROPE // 4
    inv = ROPE_THETA ** (-jnp.arange(n_freq, dtype=F32) / n_freq)
    ang = jnp.concatenate([row[:, None] * inv, col[:, None] * inv], axis=-1)
    cos, sin = jnp.cos(ang), jnp.sin(ang)
    z = jnp.zeros((seq, half), F32)
    zpad = jnp.zeros((seq, LANE - 2 * half), F32)
    c = jnp.concatenate([cos, cos, zpad], axis=-1)
    s_left = jnp.concatenate([-sin, z, zpad], axis=-1)
    s_right = jnp.concatenate([z, sin, zpad], axis=-1)
    tab = jnp.concatenate([c, s_left, s_right], axis=-1)
    ident = jnp.concatenate([jnp.ones((tm, 2 * half), F32), jnp.zeros((tm, 3 * LANE - 2 * half), F32)], axis=-1)
    return jnp.concatenate([ident, tab], axis=0)


def _ada_mod(cvec, ada_w, ada_b, layer):
    n = cvec.shape[0]
    m_pad = 16
    a = jnp.zeros((m_pad, D_MODEL), F32).at[:n].set(jax.nn.silu(cvec)).astype(BF16)
    n_cols = ada_w.shape[-1]
    tn = 1024
    (mod,) = _matmul(
        a, ada_w, w_lead=(layer,), n_cols=n_cols, tm=m_pad, tn=tn, epilogue=_ep_bias,
        extras=[(ada_b[:, None, :], (None, 1, tn), lambda j, i: (layer, 0, j))],
        outs=[(jax.ShapeDtypeStruct((m_pad, n_cols), F32), (m_pad, tn), lambda j, i: (i, j))],
        name="ada_mod")
    return mod[:n, None, :]


def kernel(x, c, ctx, c_ctx, ada_w, ada_b, norm1_g, norm2_g, ab_w_in, ab_w_out, na_q_g, na_k_g, na_rel_bias,
           sg_norm_g, sg_norm_b, sg_w, sg_b, mla_w_in, mla_q_norm_g, mla_kv_norm_g, mla_w_uq, mla_w_ukv,
           mla_q_g, mla_k_g, mla_w_out, moe_w_router, moe_b_router, moe_w_gu, moe_b_gu, moe_w_down, moe_b_down):
    n_batch, seq, d = x.shape
    t_ctx = ctx.shape[1]
    n_ctx = n_batch * t_ctx
    n_lat = n_batch * seq
    n_all = n_ctx + n_lat
    tm = 512
    tn = 512
    cvec = jnp.concatenate([c_ctx[None, :], c], axis=0)
    xs = jnp.concatenate([ctx.reshape(n_ctx, d), x.reshape(n_lat, d)], axis=0)
    mrow = functools.partial(_mod_row, tm=tm, n_ctx_rows=n_ctx, seq=seq)

    mod = _ada_mod(cvec, ada_w, ada_b, 0)
    h = _norm_mod(xs, norm1_g[0], mod, 0, 1, n_ctx, seq)
    w_in = ab_w_in
    q_gain = (na_q_g[0] * (NA_HEAD_DIM ** -0.5)).reshape(1, NA_HEAD_DIM)
    k_gain = na_k_g[0].reshape(1, NA_HEAD_DIM)

    def in_proj(col0, n_cols, epilogue, extras, name):
        (o,) = _matmul(h, w_in, w_lead=(0,), col0=col0, n_cols=n_cols, tm=tm, tn=tn, epilogue=epilogue,
                       extras=extras,
                       outs=[(jax.ShapeDtypeStruct((n_all, n_cols), BF16), (tm, tn), lambda j, i: (i, j))],
                       name=name)
        return o

    gain_spec = lambda g: [(g, (1, NA_HEAD_DIM), lambda j, i: (0, 0))]
    q = in_proj(0, NA_WIDTH, _ep_head_rms, gain_spec(q_gain), "ab_in_q")
    k = in_proj(NA_WIDTH, NA_WIDTH, _ep_head_rms, gain_spec(k_gain), "ab_in_k")
    v = in_proj(2 * NA_WIDTH, NA_WIDTH, _ep_cast, [], "ab_in_v")
    uv = in_proj(3 * NA_WIDTH, 2 * SG_WIDTH, _ep_gelu, [], "ab_in_uv")

    a_lat = _na_attention(q, k, v, na_rel_bias[0], n_batch=n_batch, seq=seq, t_ctx=t_ctx)
    a_ctx = _attention(q, k, v, n_batch=n_batch, n_heads=NA_HEADS, dq=NA_HEAD_DIM, dv=NA_HEAD_DIM,
                       tq=t_ctx, n_q_tiles=1, q_blk0=0, t_k=t_ctx, k_blk0=0, name="ctx_attention")
    gated = _spatial_gating(uv, sg_norm_g[0], sg_norm_b[0], sg_w[0], sg_b[0])
    mix = jnp.concatenate([jnp.concatenate([a_ctx, a_lat], axis=0), gated], axis=1)

    def out_proj(inp, w, resid, resid_row0, n_ctx_rows, name):
        mr = functools.partial(_mod_row, tm=tm, n_ctx_rows=n_ctx_rows, seq=seq)
        (o,) = _matmul(inp, w, w_lead=(0,), n_cols=d, tm=tm, tn=tn, epilogue=_ep_residual,
                       extras=[(resid, (tm, tn), lambda j, i: (resid_row0 // tm + i, j)),
                               (mod, (None, 1, tn), lambda j, i: (mr(i), 0, 2 * d // tn + j))],
                       outs=[(jax.ShapeDtypeStruct((inp.shape[0], d), F32), (tm, tn), lambda j, i: (i, j))],
                       name=name)
        return o

    xs = out_proj(mix, ab_w_out, xs, 0, n_ctx, "ab_out")
    h2, logits = _norm_mod(xs, norm2_g[0], mod, 3, 4, n_ctx, seq, router=(moe_w_router[0], moe_b_router[0]))
    xs = _moe_block(xs, h2, logits, mod, 0, n_ctx, seq, moe_w_gu, moe_b_gu, moe_w_down, moe_b_down)

    mod = _ada_mod(cvec, ada_w, ada_b, 1)
    h = _norm_mod(xs, norm1_g[1], mod, 0, 1, n_ctx, seq)
    w_in_pad = jnp.pad(mla_w_in, ((0, 0), (0, 0), (0, MLA_IN_PAD - mla_w_in.shape[-1])))
    c_q, c_kv, k_pe = _matmul(
        h, w_in_pad, w_lead=(0,), n_cols=MLA_IN_PAD, tm=tm, tn=MLA_IN_PAD, epilogue=_ep_mla_in,
        extras=[(mla_q_norm_g[0].reshape(1, Q_LORA), (1, Q_LORA), lambda j, i: (0, 0)),
                (mla_kv_norm_g[0].reshape(1, KV_LORA), (1, KV_LORA), lambda j, i: (0, 0))],
        outs=[(jax.ShapeDtypeStruct((n_all, Q_LORA), BF16), (tm, Q_LORA), lambda j, i: (i, 0)),
              (jax.ShapeDtypeStruct((n_all, KV_LORA), BF16), (tm, KV_LORA), lambda j, i: (i, 0)),
              (jax.ShapeDtypeStruct((n_all, LANE), F32), (tm, LANE), lambda j, i: (i, 0))],
        name="mla_in")

    rope_tab = _axial_rope_tables(seq, tm)
    n_ctx_tiles = n_ctx // tm
    seq_tiles = seq // tm
    pad_gain = lambda g, s: jnp.pad(g * s, (0, QK_PAD - QK_DIM)).reshape(1, QK_PAD)
    w_uq = jnp.pad(mla_w_uq[0].reshape(Q_LORA, MLA_HEADS, QK_DIM),
                   ((0, 0), (0, 0), (0, QK_PAD - QK_DIM))).reshape(Q_LORA, MLA_HEADS * QK_PAD)
    tn_up = 1024
    (q,) = _matmul(
        c_q, w_uq, row0=n_ctx, n_cols=MLA_HEADS * QK_PAD, tm=tm, tn=tn_up, epilogue=_ep_mla_q,
        extras=[(pad_gain(mla_q_g[0], QK_DIM ** -0.5), (1, QK_PAD), lambda j, i: (0, 0)),
                (rope_tab, (tm, 3 * LANE), lambda j, i: (1 + i % seq_tiles, 0))],
        outs=[(jax.ShapeDtypeStruct((n_lat, MLA_HEADS * QK_PAD), BF16), (tm, tn_up), lambda j, i: (i, j))],
        name="mla_up_q")
    heads_per_tile = tn_up // (QK_NOPE + V_DIM)
    k, v = _matmul(
        c_kv, mla_w_ukv, w_lead=(0,), n_cols=MLA_HEADS * (QK_NOPE + V_DIM), tm=tm, tn=tn_up, epilogue=_ep_mla_kv,
        extras=[(pad_gain(mla_k_g[0], 1.0), (1, QK_PAD), lambda j, i: (0, 0)),
                (rope_tab, (tm, 3 * LANE),
                 lambda j, i: (jnp.where(i < n_ctx_tiles, 0, 1 + (i - n_ctx_tiles) % seq_tiles), 0)),
                (k_pe, (tm, LANE), lambda j, i: (i, 0))],
        outs=[(jax.ShapeDtypeStruct((n_all, MLA_HEADS * QK_PAD), BF16), (tm, heads_per_tile * QK_PAD),
               lambda j, i: (i, j)),
              (jax.ShapeDtypeStruct((n_all, MLA_HEADS * V_DIM), BF16), (tm, heads_per_tile * V_DIM),
               lambda j, i: (i, j))],
        name="mla_up_kv")

    tq = 1024
    attn = _attention(q, k, v, n_batch=n_batch, n_heads=MLA_HEADS, dq=QK_PAD, dv=V_DIM, tq=tq,
                      n_q_tiles=seq // tq, q_blk0=0, t_k=seq, k_blk0=n_ctx // seq, t_ctx=t_ctx, n_sub=4,
                      name="mla_attention")
    x_lat = out_proj(attn, mla_w_out, xs, n_ctx, 0, "mla_out")
    h2, logits = _norm_mod(x_lat, norm2_g[1], mod, 3, 4, 0, seq, router=(moe_w_router[1], moe_b_router[1]))
    x_lat = _moe_block(x_lat, h2, logits, mod, 1, 0, seq, moe_w_gu, moe_b_gu, moe_w_down, moe_b_down)
    return x_lat.reshape(n_batch, seq, d)
```

```python
import functools

import numpy as np
import jax
import jax.numpy as jnp
from jax import lax
from jax.experimental import pallas as pl
from jax.experimental.pallas import tpu as pltpu

F32 = jnp.float32
BF16 = jnp.bfloat16

D_MODEL = 2048
GRID_W = 64
EPS = 1e-6
NEG = -1e30

NA_HEADS = 8
NA_HEAD_DIM = 128
NA_WIDTH = NA_HEADS * NA_HEAD_DIM
NA_WIN_H = 8
NA_WIN_W = 16
SG_GROUPS = 8
SG_WIDTH = D_MODEL // 2
SG_GROUP_DIM = SG_WIDTH // SG_GROUPS
SG_CHUNK = 128

MLA_HEADS = 16
Q_LORA = 512
KV_LORA = 256
QK_NOPE = 128
QK_ROPE = 64
V_DIM = 128
QK_DIM = QK_NOPE + QK_ROPE
QK_PAD = 256
MLA_IN_PAD = Q_LORA + KV_LORA + 128
ROPE_THETA = 10000.0

N_EXPERTS = 32
TOP_K = 4
D_FF = D_MODEL
SWIGLU_ALPHA = 1.702
SWIGLU_LIMIT = 7.0

LANE = 128
V7X_VMEM_LIMIT = 56 * 1024 * 1024

NA_TILE_ROWS = 4
NA_BAND_ROWS = NA_TILE_ROWS + NA_WIN_H - 1
MOE_TM = 512
MOE_TN_GU = 512
MOE_TN_DOWN = 1024


def _params(n_axes):
    return pltpu.CompilerParams(dimension_semantics=("arbitrary",) * n_axes,
                                vmem_limit_bytes=V7X_VMEM_LIMIT)


def _norm_mod_body(x_ref, g_ref, sh_ref, sc_ref):
    x = x_ref[...]
    y = x * lax.rsqrt(jnp.mean(x * x, axis=-1, keepdims=True) + EPS) * g_ref[...]
    return y * (1.0 + sc_ref[...]) + sh_ref[...]


def _norm_mod_kernel(x_ref, g_ref, sh_ref, sc_ref, o_ref):
    o_ref[...] = _norm_mod_body(x_ref, g_ref, sh_ref, sc_ref).astype(o_ref.dtype)


def _norm_mod_router_kernel(x_ref, g_ref, sh_ref, sc_ref, wr_ref, br_ref, o_ref, lg_ref):
    h = _norm_mod_body(x_ref, g_ref, sh_ref, sc_ref)
    o_ref[...] = h.astype(o_ref.dtype)
    lg_ref[...] = jnp.dot(h, wr_ref[...], precision=lax.Precision.HIGHEST,
                          preferred_element_type=F32) + br_ref[...]


def _mod_row(i, tm, n_ctx_rows, seq):
    n_ctx_tiles = n_ctx_rows // tm
    return jnp.where(i < n_ctx_tiles, 0, 1 + (i - n_ctx_tiles) // (seq // tm))


def _norm_mod(x, gain, mod, k_shift, k_scale, n_ctx_rows, seq, router=None, tm=256):
    rows, d = x.shape
    mrow = functools.partial(_mod_row, tm=tm, n_ctx_rows=n_ctx_rows, seq=seq)
    in_specs = [
        pl.BlockSpec((tm, d), lambda i: (i, 0)),
        pl.BlockSpec((1, d), lambda i: (0, 0)),
        pl.BlockSpec((None, 1, d), lambda i: (mrow(i), 0, k_shift)),
        pl.BlockSpec((None, 1, d), lambda i: (mrow(i), 0, k_scale)),
    ]
    args = [x, gain.reshape(1, d), mod, mod]
    if router is None:
        return pl.pallas_call(
            _norm_mod_kernel, grid=(rows // tm,), in_specs=in_specs,
            out_specs=pl.BlockSpec((tm, d), lambda i: (i, 0)),
            out_shape=jax.ShapeDtypeStruct((rows, d), BF16),
            compiler_params=_params(1), name="norm_mod")(*args)
    w_r, b_r = router
    n_e = w_r.shape[1]
    in_specs += [pl.BlockSpec((d, n_e), lambda i: (0, 0)), pl.BlockSpec((1, n_e), lambda i: (0, 0))]
    return pl.pallas_call(
        _norm_mod_router_kernel, grid=(rows // tm,), in_specs=in_specs,
        out_specs=[pl.BlockSpec((tm, d), lambda i: (i, 0)), pl.BlockSpec((tm, n_e), lambda i: (i, 0))],
        out_shape=[jax.ShapeDtypeStruct((rows, d), BF16), jax.ShapeDtypeStruct((rows, n_e), F32)],
        compiler_params=_params(1), name="norm_mod_router")(*args, w_r, b_r.reshape(1, n_e))


def _matmul_kernel(*refs, n_extra, n_out, epilogue):
    x_ref, w_ref = refs[:2]
    extra = refs[2:2 + n_extra]
    outs = refs[2 + n_extra:2 + n_extra + n_out]
    wbf_ref = refs[-1]

    @pl.when(pl.program_id(1) == 0)
    def _():
        wbf_ref[...] = w_ref[...].astype(BF16)

    acc = jnp.dot(x_ref[...], wbf_ref[...], preferred_element_type=F32)
    epilogue(acc, extra, outs)


def _matmul(x, w, *, w_lead=(), col0=0, n_cols, tm, tn, epilogue, extras=(), outs, name, row0=0, m=None):
    k = x.shape[1]
    m = x.shape[0] - row0 if m is None else m
    assert m % tm == 0 and row0 % tm == 0 and n_cols % tn == 0 and col0 % tn == 0
    lead = tuple(w_lead)
    w_spec = pl.BlockSpec((None,) * len(lead) + (k, tn), lambda j, i: lead + (0, col0 // tn + j))
    in_specs = [pl.BlockSpec((tm, k), lambda j, i: (row0 // tm + i, 0)), w_spec]
    in_specs += [pl.BlockSpec(bs, im) for _, bs, im in extras]
    out_specs = [pl.BlockSpec(bs, im) for _, bs, im in outs]
    kern = functools.partial(_matmul_kernel, n_extra=len(extras), n_out=len(outs), epilogue=epilogue)
    res = pl.pallas_call(
        kern, grid=(n_cols // tn, m // tm), in_specs=in_specs, out_specs=out_specs,
        out_shape=[s for s, _, _ in outs],
        scratch_shapes=[pltpu.VMEM((k, tn), BF16)],
        compiler_params=_params(2), name=name)(x, w, *[a for a, _, _ in extras])
    return res


def _ep_bias(acc, extra, outs):
    outs[0][...] = acc + extra[0][...]


def _ep_cast(acc, extra, outs):
    outs[0][...] = acc.astype(outs[0].dtype)


def _ep_gelu(acc, extra, outs):
    outs[0][...] = jax.nn.gelu(acc).astype(outs[0].dtype)


def _ep_head_rms(acc, extra, outs):
    g = extra[0][...]
    for h in range(acc.shape[1] // LANE):
        a = acc[:, h * LANE:(h + 1) * LANE]
        r = lax.rsqrt(jnp.mean(a * a, axis=-1, keepdims=True) + EPS)
        outs[0][:, h * LANE:(h + 1) * LANE] = (a * r * g).astype(outs[0].dtype)


def _ep_residual(acc, extra, outs):
    outs[0][...] = extra[0][...] + extra[1][...] * acc


def _rms(a, g):
    return a * lax.rsqrt(jnp.mean(a * a, axis=-1, keepdims=True) + EPS) * g


def _ep_mla_in(acc, extra, outs):
    outs[0][...] = _rms(acc[:, :Q_LORA], extra[0][...]).astype(BF16)
    outs[1][...] = _rms(acc[:, Q_LORA:Q_LORA + KV_LORA], extra[1][...]).astype(BF16)
    outs[2][...] = acc[:, Q_LORA + KV_LORA:]


def _rope_tail(t, tab_ref):
    c = tab_ref[:, 0:LANE]
    s_left = tab_ref[:, LANE:2 * LANE]
    s_right = tab_ref[:, 2 * LANE:3 * LANE]
    return (t * c + pltpu.roll(t, LANE - QK_ROPE // 2, axis=1) * s_left
            + pltpu.roll(t, QK_ROPE // 2, axis=1) * s_right)


def _ep_mla_q(acc, extra, outs):
    g_ref, tab_ref = extra
    g0 = g_ref[:, :LANE]
    g1 = g_ref[:, LANE:]
    for h in range(acc.shape[1] // QK_PAD):
        a0 = acc[:, h * QK_PAD:h * QK_PAD + LANE]
        a1 = acc[:, h * QK_PAD + LANE:(h + 1) * QK_PAD]
        ss = jnp.sum(a0 * a0, axis=-1, keepdims=True) + jnp.sum(a1 * a1, axis=-1, keepdims=True)
        r = lax.rsqrt(ss * (1.0 / QK_DIM) + EPS)
        outs[0][:, h * QK_PAD:h * QK_PAD + LANE] = (a0 * r * g0).astype(BF16)
        outs[0][:, h * QK_PAD + LANE:(h + 1) * QK_PAD] = _rope_tail(a1 * r * g1, tab_ref).astype(BF16)


def _ep_mla_kv(acc, extra, outs):
    g_ref, tab_ref, pe_ref = extra
    k_out, v_out = outs
    g0 = g_ref[:, :LANE]
    g1 = g_ref[:, LANE:]
    pe = pe_ref[...]
    pe_ss = jnp.sum(pe * pe, axis=-1, keepdims=True)
    for h in range(acc.shape[1] // (QK_NOPE + V_DIM)):
        base = h * (QK_NOPE + V_DIM)
        kn = acc[:, base:base + QK_NOPE]
        r = lax.rsqrt((jnp.sum(kn * kn, axis=-1, keepdims=True) + pe_ss) * (1.0 / QK_DIM) + EPS)
        k_out[:, h * QK_PAD:h * QK_PAD + LANE] = (kn * r * g0).astype(BF16)
        k_out[:, h * QK_PAD + LANE:(h + 1) * QK_PAD] = _rope_tail(pe * r * g1, tab_ref).astype(BF16)
        v_out[:, h * V_DIM:(h + 1) * V_DIM] = acc[:, base + QK_NOPE:base + QK_NOPE + V_DIM].astype(BF16)


def _softmax_pv(s_parts, v_parts):
    m = s_parts[0].max(axis=-1, keepdims=True)
    for s in s_parts[1:]:
        m = jnp.maximum(m, s.max(axis=-1, keepdims=True))
    l = 0.0
    o = 0.0
    for s, v in zip(s_parts, v_parts):
        p = jnp.exp(s - m)
        l = l + p.sum(axis=-1, keepdims=True)
        o = o + jnp.dot(p.astype(BF16), v, preferred_element_type=F32)
    return o / l


def _qk(q, k):
    return lax.dot_general(q, k, (((1,), (1,)), ((), ())), preferred_element_type=F32)


def _attn_kernel(*refs, has_ctx, n_sub):
    if has_ctx:
        q_ref, k_ref, v_ref, kc_ref, vc_ref, o_ref = refs
    else:
        q_ref, k_ref, v_ref, o_ref = refs
    sub = q_ref.shape[0] // n_sub

    def scores(r):
        q = q_ref[r * sub:(r + 1) * sub, :]
        parts = [_qk(q, k_ref[...])]
        if has_ctx:
            parts.append(_qk(q, kc_ref[...]))
        return parts

    s_next = scores(0)
    for r in range(n_sub):
        s_cur = s_next
        if r + 1 < n_sub:
            s_next = scores(r + 1)
        v_parts = [v_ref[...]] + ([vc_ref[...]] if has_ctx else [])
        o_ref[r * sub:(r + 1) * sub, :] = _softmax_pv(s_cur, v_parts).astype(o_ref.dtype)


def _attention(q, k, v, *, n_batch, n_heads, dq, dv, tq, n_q_tiles, q_blk0, t_k, k_blk0, t_ctx=None, n_sub=1, name):
    has_ctx = t_ctx is not None
    in_specs = [
        pl.BlockSpec((tq, dq), lambda b, h, i: (q_blk0 + b * n_q_tiles + i, h)),
        pl.BlockSpec((t_k, dq), lambda b, h, i: (k_blk0 + b, h)),
        pl.BlockSpec((t_k, dv), lambda b, h, i: (k_blk0 + b, h)),
    ]
    args = [q, k, v]
    if has_ctx:
        in_specs += [pl.BlockSpec((t_ctx, dq), lambda b, h, i: (b, h)),
                     pl.BlockSpec((t_ctx, dv), lambda b, h, i: (b, h))]
        args += [k, v]
    return pl.pallas_call(
        functools.partial(_attn_kernel, has_ctx=has_ctx, n_sub=n_sub),
        grid=(n_batch, n_heads, n_q_tiles), in_specs=in_specs,
        out_specs=pl.BlockSpec((tq, dv), lambda b, h, i: (b * n_q_tiles + i, h)),
        out_shape=jax.ShapeDtypeStruct((n_batch * n_q_tiles * tq, n_heads * dv), BF16),
        compiler_params=_params(3), name=name)(*args)


def _na_band_start(t, rows):
    return jnp.clip(t * NA_TILE_ROWS - NA_WIN_H // 2, 0, rows - NA_BAND_ROWS)


def _na_kernel(q_ref, k_ref, v_ref, kc_ref, vc_ref, bias_ref, o_ref, *, rows):
    t = pl.program_id(1)
    ks = pl.multiple_of(_na_band_start(t, rows) * GRID_W, GRID_W)
    band = pl.ds(ks, NA_BAND_ROWS * GRID_W)

    def scores(h):
        cols = slice(h * NA_HEAD_DIM, (h + 1) * NA_HEAD_DIM)
        q = q_ref[:, cols]
        return [_qk(q, k_ref[band, cols]) + bias_ref[h], _qk(q, kc_ref[:, cols])]

    s_next = scores(0)
    for h in range(NA_HEADS):
        s_cur = s_next
        if h + 1 < NA_HEADS:
            s_next = scores(h + 1)
        cols = slice(h * NA_HEAD_DIM, (h + 1) * NA_HEAD_DIM)
        o_ref[:, cols] = _softmax_pv(s_cur, [v_ref[band, cols], vc_ref[:, cols]]).astype(o_ref.dtype)


def _na_bias_tables(rel_bias, rows):
    n_tiles = rows // NA_TILE_ROWS
    n_r, n_c = 2 * NA_WIN_H - 1, 2 * NA_WIN_W - 1
    qr = np.arange(NA_TILE_ROWS)[:, None]
    ur = np.arange(NA_BAND_ROWS)[None, :]
    qc = np.arange(GRID_W)[:, None]
    kc = np.arange(GRID_W)[None, :]
    c0 = np.clip(qc - NA_WIN_W // 2, 0, GRID_W - NA_WIN_W)
    col_valid = (kc >= c0) & (kc < c0 + NA_WIN_W)
    col_onehot = np.eye(n_c, dtype=np.float32)[np.clip(kc - qc + NA_WIN_W - 1, 0, n_c - 1)]
    geoms, ids = [], []
    for t in range(n_tiles):
        u0 = int(np.clip(t * NA_TILE_ROWS - NA_WIN_H // 2, 0, rows - NA_BAND_ROWS))
        r = t * NA_TILE_ROWS + qr
        key_row = u0 + ur
        r0 = np.clip(r - NA_WIN_H // 2, 0, rows - NA_WIN_H)
        geom = ((key_row >= r0) & (key_row < r0 + NA_WIN_H), np.clip(key_row - r + NA_WIN_H - 1, 0, n_r - 1))
        for gi, g in enumerate(geoms):
            if all(np.array_equal(a, b) for a, b in zip(g, geom)):
                ids.append(gi)
                break
        else:
            ids.append(len(geoms))
            geoms.append(geom)
    row_valid = np.stack([g[0] for g in geoms])
    row_onehot = np.eye(n_r, dtype=np.float32)[np.stack([g[1] for g in geoms])]
    bias = jnp.einsum('gaur,hrc,bkc->ghabuk', row_onehot, rel_bias.astype(F32), col_onehot,
                      precision=lax.Precision.HIGHEST)
    valid = row_valid[:, None, :, None, :, None] & col_valid[None, None, None, :, None, :]
    tables = jnp.where(valid, bias, NEG)
    return tables.reshape(len(geoms), NA_HEADS, NA_TILE_ROWS * GRID_W, NA_BAND_ROWS * GRID_W), ids


def _na_attention(q, k, v, rel_bias, *, n_batch, seq, t_ctx):
    rows = seq // GRID_W
    n_tiles = rows // NA_TILE_ROWS
    tq = NA_TILE_ROWS * GRID_W
    tables, ids = _na_bias_tables(rel_bias, rows)
    assert ids == [0] + [1] * (n_tiles - 2) + [2], ids
    ctx_tiles = n_batch * t_ctx // tq
    ctx_units = n_batch * t_ctx // seq

    def table_id(t):
        return jnp.where(t == 0, 0, jnp.where(t == n_tiles - 1, 2, 1))

    return pl.pallas_call(
        functools.partial(_na_kernel, rows=rows),
        grid=(n_batch, n_tiles),
        in_specs=[
            pl.BlockSpec((tq, NA_WIDTH), lambda b, t: (ctx_tiles + b * n_tiles + t, 0)),
            pl.BlockSpec((seq, NA_WIDTH), lambda b, t: (ctx_units + b, 0)),
            pl.BlockSpec((seq, NA_WIDTH), lambda b, t: (ctx_units + b, 0)),
            pl.BlockSpec((t_ctx, NA_WIDTH), lambda b, t: (b, 0)),
            pl.BlockSpec((t_ctx, NA_WIDTH), lambda b, t: (b, 0)),
            pl.BlockSpec((None, NA_HEADS, tq, NA_BAND_ROWS * GRID_W), lambda b, t: (table_id(t), 0, 0, 0)),
        ],
        out_specs=pl.BlockSpec((tq, NA_WIDTH), lambda b, t: (b * n_tiles + t, 0)),
        out_shape=jax.ShapeDtypeStruct((n_batch * seq, NA_WIDTH), BF16),
        compiler_params=_params(2), name="na_attention")(q, k, v, k, v, tables)


def _sg_kernel(uv_ref, g_ref, b_ref, ws_ref, bs_ref, o_ref, *, chunks):
    for c in range(chunks):
        r = slice(c * SG_CHUNK, (c + 1) * SG_CHUNK)
        z = uv_ref[r, SG_WIDTH:].astype(F32)
        mu = jnp.mean(z, axis=-1, keepdims=True)
        zc = z - mu
        var = jnp.mean(zc * zc, axis=-1, keepdims=True)
        zn = (zc * lax.rsqrt(var + EPS) * g_ref[...] + b_ref[...]).astype(BF16)
        for g in range(SG_GROUPS):
            cols = slice(g * SG_GROUP_DIM, (g + 1) * SG_GROUP_DIM)
            mixed = jnp.dot(ws_ref[g], zn[:, cols], preferred_element_type=F32) + bs_ref[g]
            o_ref[r, cols] = (uv_ref[r, cols].astype(F32) * mixed).astype(o_ref.dtype)


def _spatial_gating(uv, ln_g, ln_b, w_s, b_s, tm=512):
    rows = uv.shape[0]
    bs = jnp.broadcast_to(b_s.astype(F32)[:, :, None], (SG_GROUPS, SG_CHUNK, SG_GROUP_DIM))
    return pl.pallas_call(
        functools.partial(_sg_kernel, chunks=tm // SG_CHUNK),
        grid=(rows // tm,),
        in_specs=[
            pl.BlockSpec((tm, 2 * SG_WIDTH), lambda i: (i, 0)),
            pl.BlockSpec((1, SG_WIDTH), lambda i: (0, 0)),
            pl.BlockSpec((1, SG_WIDTH), lambda i: (0, 0)),
            pl.BlockSpec((SG_GROUPS, SG_CHUNK, SG_CHUNK), lambda i: (0, 0, 0)),
            pl.BlockSpec((SG_GROUPS, SG_CHUNK, SG_GROUP_DIM), lambda i: (0, 0, 0)),
        ],
        out_specs=pl.BlockSpec((tm, SG_WIDTH), lambda i: (i, 0)),
        out_shape=jax.ShapeDtypeStruct((rows, SG_WIDTH), BF16),
        compiler_params=_params(1), name="spatial_gating")(
            uv, ln_g.reshape(1, SG_WIDTH), ln_b.reshape(1, SG_WIDTH), w_s.astype(BF16), bs)


def _tile_weights_changed(i, te_ref):
    return (i == 0) | (te_ref[i] != te_ref[jnp.maximum(i - 1, 0)])


def _swiglu(g, u):
    g = jnp.minimum(g, SWIGLU_LIMIT)
    u = jnp.clip(u, -SWIGLU_LIMIT, SWIGLU_LIMIT)
    return (u + 1.0) * (g * jax.nn.sigmoid(SWIGLU_ALPHA * g))


def _for_each_valid_half(i, tv_ref, o_ref, compute):
    half = o_ref.shape[0] // 2
    for s in range(2):
        rows = slice(s * half, (s + 1) * half)

        @pl.when(tv_ref[i] > s * half)
        def _(rows=rows):
            o_ref[rows, :] = compute(rows).astype(o_ref.dtype)

        @pl.when(tv_ref[i] <= s * half)
        def _(rows=rows):
            o_ref[rows, :] = jnp.zeros((half, o_ref.shape[1]), o_ref.dtype)


def _moe_gu_kernel(te_ref, tv_ref, ts_ref, x_ref, wg_ref, wu_ref, bg_ref, bu_ref, o_ref, wg_bf, wu_bf):
    i = pl.program_id(1)

    @pl.when(_tile_weights_changed(i, te_ref))
    def _():
        wg_bf[...] = wg_ref[...].astype(BF16)
        wu_bf[...] = wu_ref[...].astype(BF16)

    def compute(rows):
        x = x_ref[rows, :]
        g = jnp.dot(x, wg_bf[...], preferred_element_type=F32) + bg_ref[...]
        u = jnp.dot(x, wu_bf[...], preferred_element_type=F32) + bu_ref[...]
        return _swiglu(g, u)

    _for_each_valid_half(i, tv_ref, o_ref, compute)


def _moe_down_kernel(te_ref, tv_ref, ts_ref, h_ref, w_ref, b_ref, o_ref, w_bf):
    i = pl.program_id(1)

    @pl.when(_tile_weights_changed(i, te_ref))
    def _():
        w_bf[...] = w_ref[...].astype(BF16)

    def compute(rows):
        return jnp.dot(h_ref[rows, :], w_bf[...], preferred_element_type=F32) + b_ref[...]

    _for_each_valid_half(i, tv_ref, o_ref, compute)


def _moe_experts(x_sorted, tile_e, tile_valid, tile_src, layer, w_gu, b_gu, w_down, b_down):
    r_pad, d = x_sorted.shape
    tm, tn, tn_down = MOE_TM, MOE_TN_GU, MOE_TN_DOWN
    n_tiles = r_pad // tm
    n_j = D_FF // tn
    x_map = lambda j, i, te, tv, ts: (ts[i], 0)
    out_map = lambda j, i, te, tv, ts: (i, j)

    hid = pl.pallas_call(
        _moe_gu_kernel,
        grid_spec=pltpu.PrefetchScalarGridSpec(
            num_scalar_prefetch=3, grid=(n_j, n_tiles),
            in_specs=[
                pl.BlockSpec((tm, d), x_map),
                pl.BlockSpec((None, None, d, tn), lambda j, i, te, tv, ts: (layer, te[i], 0, j)),
                pl.BlockSpec((None, None, d, tn), lambda j, i, te, tv, ts: (layer, te[i], 0, n_j + j)),
                pl.BlockSpec((None, None, 1, tn), lambda j, i, te, tv, ts: (layer, te[i], 0, j)),
                pl.BlockSpec((None, None, 1, tn), lambda j, i, te, tv, ts: (layer, te[i], 0, n_j + j)),
            ],
            out_specs=pl.BlockSpec((tm, tn), out_map),
            scratch_shapes=[pltpu.VMEM((d, tn), BF16), pltpu.VMEM((d, tn), BF16)]),
        out_shape=jax.ShapeDtypeStruct((r_pad, D_FF), BF16),
        compiler_params=_params(2), name="moe_gate_up")(
            tile_e, tile_valid, tile_src, x_sorted, w_gu, w_gu, b_gu[:, :, None, :], b_gu[:, :, None, :])

    return pl.pallas_call(
        _moe_down_kernel,
        grid_spec=pltpu.PrefetchScalarGridSpec(
            num_scalar_prefetch=3, grid=(d // tn_down, n_tiles),
            in_specs=[
                pl.BlockSpec((tm, D_FF), x_map),
                pl.BlockSpec((None, None, D_FF, tn_down), lambda j, i, te, tv, ts: (layer, te[i], 0, j)),
                pl.BlockSpec((None, None, 1, tn_down), lambda j, i, te, tv, ts: (layer, te[i], 0, j)),
            ],
            out_specs=pl.BlockSpec((tm, tn_down), out_map),
            scratch_shapes=[pltpu.VMEM((D_FF, tn_down), BF16)]),
        out_shape=jax.ShapeDtypeStruct((r_pad, d), BF16),
        compiler_params=_params(2), name="moe_down")(
            tile_e, tile_valid, tile_src, hid, w_down, b_down[:, :, None, :])


def _moe_route(logits):
    n = logits.shape[0]
    nk = n * TOP_K
    tm = MOE_TM
    i32 = jnp.int32
    top_val, top_idx = lax.top_k(logits, TOP_K)
    gate = jax.nn.softmax(top_val, axis=-1)
    flat_e = top_idx.reshape(nk).astype(i32)
    iota = jnp.arange(nk, dtype=i32)
    _, order = lax.sort((flat_e, iota), num_keys=1)
    _, inv_order = lax.sort((order, iota), num_keys=1)
    onehot = flat_e[:, None] == jnp.arange(N_EXPERTS, dtype=i32)[None, :]
    counts = jnp.sum(onehot, axis=0, dtype=i32)
    padded = (counts + tm - 1) // tm * tm
    pad_end = jnp.cumsum(padded)
    pad_start = pad_end - padded
    start = jnp.cumsum(counts) - counts
    pos = inv_order + jnp.sum(jnp.where(onehot, (pad_start - start)[None, :], 0), axis=1)
    pos = pos.reshape(n, TOP_K).T.reshape(nk)

    n_tiles = nk // tm + N_EXPERTS
    n_used = pad_end[-1] // tm
    tile_src = jnp.minimum(jnp.arange(n_tiles, dtype=i32), n_used - 1)
    tile_e = jnp.minimum(jnp.sum(pad_end[None, :] <= (tile_src * tm)[:, None], axis=1, dtype=i32), N_EXPERTS - 1)
    in_use = jnp.arange(n_tiles, dtype=i32) < n_used
    rank0 = tile_src * tm - pad_start[tile_e]
    tile_valid = jnp.where(in_use, jnp.clip(counts[tile_e] - rank0, 0, tm), 0).astype(i32)
    within = jnp.arange(tm, dtype=i32)[None, :]
    valid = within < tile_valid[:, None]
    src = jnp.clip((start[tile_e] + rank0)[:, None] + within, 0, nk - 1)
    filler = (jnp.arange(n_tiles * tm, dtype=i32) % n).reshape(n_tiles, tm)
    rows_tok = jnp.where(valid, order.at[src].get(mode="promise_in_bounds") // TOP_K, filler).reshape(n_tiles * tm)
    return gate, rows_tok, pos, tile_e, tile_valid, tile_src.astype(i32)


def _moe_combine_kernel(x_ref, g2_ref, gate_ref, y_ref, o_ref):
    gate = gate_ref[...]
    f = gate[:, 0:1] * y_ref[0].astype(F32)
    for k in range(1, TOP_K):
        f = f + gate[:, k:k + 1] * y_ref[k].astype(F32)
    o_ref[...] = x_ref[...] + g2_ref[...] * f


def _moe_block(xs, h2, logits, mod, layer, n_ctx_rows, seq, w_gu, b_gu, w_down, b_down, tm=256):
    n, d = h2.shape
    gate, rows_tok, pos, tile_e, tile_valid, tile_src = _moe_route(logits)
    x_sorted = h2.at[rows_tok].get(mode="promise_in_bounds")
    y = _moe_experts(x_sorted, tile_e, tile_valid, tile_src, layer, w_gu, b_gu, w_down, b_down)
    y_k = y.at[pos].get(mode="promise_in_bounds").reshape(TOP_K, n, d)
    mrow = functools.partial(_mod_row, tm=tm, n_ctx_rows=n_ctx_rows, seq=seq)
    return pl.pallas_call(
        _moe_combine_kernel, grid=(n // tm,),
        in_specs=[
            pl.BlockSpec((tm, d), lambda i: (i, 0)),
            pl.BlockSpec((None, 1, d), lambda i: (mrow(i), 0, 5)),
            pl.BlockSpec((tm, TOP_K), lambda i: (i, 0)),
            pl.BlockSpec((TOP_K, tm, d), lambda i: (0, i, 0)),
        ],
        out_specs=pl.BlockSpec((tm, d), lambda i: (i, 0)),
        out_shape=jax.ShapeDtypeStruct((n, d), F32),
        compiler_params=_params(1), name="moe_combine")(xs, mod, gate, y_k)


def _axial_rope_tables(seq, tm):
    half = QK_ROPE // 2
    t = jnp.arange(seq)
    row = (t // GRID_W).astype(F32)
    col = (t % GRID_W).astype(F32)
    n_freq = QK_ROPE // 4
    inv = ROPE_THETA ** (-jnp.arange(n_freq, dtype=F32) / n_freq)
    ang = jnp.concatenate([row[:, None] * inv, col[:, None] * inv], axis=-1)
    cos, sin = jnp.cos(ang), jnp.sin(ang)
    z = jnp.zeros((seq, half), F32)
    zpad = jnp.zeros((seq, LANE - 2 * half), F32)
    c = jnp.concatenate([cos, cos, zpad], axis=-1)
    s_left = jnp.concatenate([-sin, z, zpad], axis=-1)
    s_right = jnp.concatenate([z, sin, zpad], axis=-1)
    tab = jnp.concatenate([c, s_left, s_right], axis=-1)
    ident = jnp.concatenate([jnp.ones((tm, 2 * half), F32), jnp.zeros((tm, 3 * LANE - 2 * half), F32)], axis=-1)
    return jnp.concatenate([ident, tab], axis=0)


def _ada_mod(cvec, ada_w, ada_b, layer):
    n = cvec.shape[0]
    m_pad = 16
    a = jnp.zeros((m_pad, D_MODEL), F32).at[:n].set(jax.nn.silu(cvec)).astype(BF16)
    n_cols = ada_w.shape[-1]
    tn = 1024
    (mod,) = _matmul(
        a, ada_w, w_lead=(layer,), n_cols=n_cols, tm=m_pad, tn=tn, epilogue=_ep_bias,
        extras=[(ada_b[:, None, :], (None, 1, tn), lambda j, i: (layer, 0, j))],
        outs=[(jax.ShapeDtypeStruct((m_pad, n_cols), F32), (m_pad, tn), lambda j, i: (i, j))],
        name="ada_mod")
    return mod[:n, None, :]


def kernel(x, c, ctx, c_ctx, ada_w, ada_b, norm1_g, norm2_g, ab_w_in, ab_w_out, na_q_g, na_k_g, na_rel_bias,
           sg_norm_g, sg_norm_b, sg_w, sg_b, mla_w_in, mla_q_norm_g, mla_kv_norm_g, mla_w_uq, mla_w_ukv,
           mla_q_g, mla_k_g, mla_w_out, moe_w_router, moe_b_router, moe_w_gu, moe_b_gu, moe_w_down, moe_b_down):
    n_batch, seq, d = x.shape
    t_ctx = ctx.shape[1]
    n_ctx = n_batch * t_ctx
    n_lat = n_batch * seq
    n_all = n_ctx + n_lat
    tm = 512
    tn = 512
    cvec = jnp.concatenate([c_ctx[None, :], c], axis=0)
    xs = jnp.concatenate([ctx.reshape(n_ctx, d), x.reshape(n_lat, d)], axis=0)
    mrow = functools.partial(_mod_row, tm=tm, n_ctx_rows=n_ctx, seq=seq)

    mod = _ada_mod(cvec, ada_w, ada_b, 0)
    h = _norm_mod(xs, norm1_g[0], mod, 0, 1, n_ctx, seq)
    w_in = ab_w_in
    q_gain = (na_q_g[0] * (NA_HEAD_DIM ** -0.5)).reshape(1, NA_HEAD_DIM)
    k_gain = na_k_g[0].reshape(1, NA_HEAD_DIM)

    def in_proj(col0, n_cols, epilogue, extras, name):
        (o,) = _matmul(h, w_in, w_lead=(0,), col0=col0, n_cols=n_cols, tm=tm, tn=tn, epilogue=epilogue,
                       extras=extras,
                       outs=[(jax.ShapeDtypeStruct((n_all, n_cols), BF16), (tm, tn), lambda j, i: (i, j))],
                       name=name)
        return o

    gain_spec = lambda g: [(g, (1, NA_HEAD_DIM), lambda j, i: (0, 0))]
    q = in_proj(0, NA_WIDTH, _ep_head_rms, gain_spec(q_gain), "ab_in_q")
    k = in_proj(NA_WIDTH, NA_WIDTH, _ep_head_rms, gain_spec(k_gain), "ab_in_k")
    v = in_proj(2 * NA_WIDTH, NA_WIDTH, _ep_cast, [], "ab_in_v")
    uv = in_proj(3 * NA_WIDTH, 2 * SG_WIDTH, _ep_gelu, [], "ab_in_uv")

    a_lat = _na_attention(q, k, v, na_rel_bias[0], n_batch=n_batch, seq=seq, t_ctx=t_ctx)
    a_ctx = _attention(q, k, v, n_batch=n_batch, n_heads=NA_HEADS, dq=NA_HEAD_DIM, dv=NA_HEAD_DIM,
                       tq=t_ctx, n_q_tiles=1, q_blk0=0, t_k=t_ctx, k_blk0=0, name="ctx_attention")
    gated = _spatial_gating(uv, sg_norm_g[0], sg_norm_b[0], sg_w[0], sg_b[0])
    mix = jnp.concatenate([jnp.concatenate([a_ctx, a_lat], axis=0), gated], axis=1)

    def out_proj(inp, w, resid, resid_row0, n_ctx_rows, name):
        mr = functools.partial(_mod_row, tm=tm, n_ctx_rows=n_ctx_rows, seq=seq)
        (o,) = _matmul(inp, w, w_lead=(0,), n_cols=d, tm=tm, tn=tn, epilogue=_ep_residual,
                       extras=[(resid, (tm, tn), lambda j, i: (resid_row0 // tm + i, j)),
                               (mod, (None, 1, tn), lambda j, i: (mr(i), 0, 2 * d // tn + j))],
                       outs=[(jax.ShapeDtypeStruct((inp.shape[0], d), F32), (tm, tn), lambda j, i: (i, j))],
                       name=name)
        return o

    xs = out_proj(mix, ab_w_out, xs, 0, n_ctx, "ab_out")
    h2, logits = _norm_mod(xs, norm2_g[0], mod, 3, 4, n_ctx, seq, router=(moe_w_router[0], moe_b_router[0]))
    xs = _moe_block(xs, h2, logits, mod, 0, n_ctx, seq, moe_w_gu, moe_b_gu, moe_w_down, moe_b_down)

    mod = _ada_mod(cvec, ada_w, ada_b, 1)
    h = _norm_mod(xs, norm1_g[1], mod, 0, 1, n_ctx, seq)
    w_in_pad = jnp.pad(mla_w_in, ((0, 0), (0, 0), (0, MLA_IN_PAD - mla_w_in.shape[-1])))
    c_q, c_kv, k_pe = _matmul(
        h, w_in_pad, w_lead=(0,), n_cols=MLA_IN_PAD, tm=tm, tn=MLA_IN_PAD, epilogue=_ep_mla_in,
        extras=[(mla_q_norm_g[0].reshape(1, Q_LORA), (1, Q_LORA), lambda j, i: (0, 0)),
                (mla_kv_norm_g[0].reshape(1, KV_LORA), (1, KV_LORA), lambda j, i: (0, 0))],
        outs=[(jax.ShapeDtypeStruct((n_all, Q_LORA), BF16), (tm, Q_LORA), lambda j, i: (i, 0)),
              (jax.ShapeDtypeStruct((n_all, KV_LORA), BF16), (tm, KV_LORA), lambda j, i: (i, 0)),
              (jax.ShapeDtypeStruct((n_all, LANE), F32), (tm, LANE), lambda j, i: (i, 0))],
        name="mla_in")

    rope_tab = _axial_rope_tables(seq, tm)
    n_ctx_tiles = n_ctx // tm
    seq_tiles = seq // tm
    pad_gain = lambda g, s: jnp.pad(g * s, (0, QK_PAD - QK_DIM)).reshape(1, QK_PAD)
    w_uq = jnp.pad(mla_w_uq[0].reshape(Q_LORA, MLA_HEADS, QK_DIM),
                   ((0, 0), (0, 0), (0, QK_PAD - QK_DIM))).reshape(Q_LORA, MLA_HEADS * QK_PAD)
    tn_up = 1024
    (q,) = _matmul(
        c_q, w_uq, row0=n_ctx, n_cols=MLA_HEADS * QK_PAD, tm=tm, tn=tn_up, epilogue=_ep_mla_q,
        extras=[(pad_gain(mla_q_g[0], QK_DIM ** -0.5), (1, QK_PAD), lambda j, i: (0, 0)),
                (rope_tab, (tm, 3 * LANE), lambda j, i: (1 + i % seq_tiles, 0))],
        outs=[(jax.ShapeDtypeStruct((n_lat, MLA_HEADS * QK_PAD), BF16), (tm, tn_up), lambda j, i: (i, j))],
        name="mla_up_q")
    heads_per_tile = tn_up // (QK_NOPE + V_DIM)
    k, v = _matmul(
        c_kv, mla_w_ukv, w_lead=(0,), n_cols=MLA_HEADS * (QK_NOPE + V_DIM), tm=tm, tn=tn_up, epilogue=_ep_mla_kv,
        extras=[(pad_gain(mla_k_g[0], 1.0), (1, QK_PAD), lambda j, i: (0, 0)),
                (rope_tab, (tm, 3 * LANE),
                 lambda j, i: (jnp.where(i < n_ctx_tiles, 0, 1 + (i - n_ctx_tiles) % seq_tiles), 0)),
                (k_pe, (tm, LANE), lambda j, i: (i, 0))],
        outs=[(jax.ShapeDtypeStruct((n_all, MLA_HEADS * QK_PAD), BF16), (tm, heads_per_tile * QK_PAD),
               lambda j, i: (i, j)),
              (jax.ShapeDtypeStruct((n_all, MLA_HEADS * V_DIM), BF16), (tm, heads_per_tile * V_DIM),
               lambda j, i: (i, j))],
        name="mla_up_kv")

    tq = 1024
    attn = _attention(q, k, v, n_batch=n_batch, n_heads=MLA_HEADS, dq=QK_PAD, dv=V_DIM, tq=tq,
                      n_q_tiles=seq // tq, q_blk0=0, t_k=seq, k_blk0=n_ctx // seq, t_ctx=t_ctx, n_sub=4,
                      name="mla_attention")
    x_lat = out_proj(attn, mla_w_out, xs, n_ctx, 0, "mla_out")
    h2, logits = _norm_mod(x_lat, norm2_g[1], mod, 3, 4, 0, seq, router=(moe_w_router[1], moe_b_router[1]))
    x_lat = _moe_block(x_lat, h2, logits, mod, 1, 0, seq, moe_w_gu, moe_b_gu, moe_w_down, moe_b_down)
    return x_lat.reshape(n_batch, seq, d)
```

```python
import functools

import numpy as np
import jax
import jax.numpy as jnp
from jax import lax
from jax.experimental import pallas as pl
from jax.experimental.pallas import tpu as pltpu

F32 = jnp.float32
BF16 = jnp.bfloat16

D_MODEL = 2048
GRID_W = 64
EPS = 1e-6
NEG = -1e30

NA_HEADS = 8
NA_HEAD_DIM = 128
NA_WIDTH = NA_HEADS * NA_HEAD_DIM
NA_WIN_H = 8
NA_WIN_W = 16
SG_GROUPS = 8
SG_WIDTH = D_MODEL // 2
SG_GROUP_DIM = SG_WIDTH // SG_GROUPS
SG_CHUNK = 128

MLA_HEADS = 16
Q_LORA = 512
KV_LORA = 256
QK_NOPE = 128
QK_ROPE = 64
V_DIM = 128
QK_DIM = QK_NOPE + QK_ROPE
QK_PAD = 256
MLA_IN_PAD = Q_LORA + KV_LORA + 128
ROPE_THETA = 10000.0

N_EXPERTS = 32
TOP_K = 4
D_FF = D_MODEL
SWIGLU_ALPHA = 1.702
SWIGLU_LIMIT = 7.0

LANE = 128
V7X_VMEM_LIMIT = 56 * 1024 * 1024

MATMUL_TM = 1024
MATMUL_SUB_ROWS = 256
NA_TILE_ROWS = 4
NA_BAND_ROWS = NA_TILE_ROWS + NA_WIN_H - 1
MOE_TM = 1024
MOE_PARTS = 4
MOE_TN_GU = 512
MOE_TN_DOWN = 1024


def _params(n_axes):
    return pltpu.CompilerParams(dimension_semantics=("arbitrary",) * n_axes,
                                vmem_limit_bytes=V7X_VMEM_LIMIT)


def _norm_mod_body(x_ref, g_ref, sh_ref, sc_ref):
    x = x_ref[...]
    y = x * lax.rsqrt(jnp.mean(x * x, axis=-1, keepdims=True) + EPS) * g_ref[...]
    return y * (1.0 + sc_ref[...]) + sh_ref[...]


def _norm_mod_kernel(x_ref, g_ref, sh_ref, sc_ref, o_ref):
    o_ref[...] = _norm_mod_body(x_ref, g_ref, sh_ref, sc_ref).astype(o_ref.dtype)


def _norm_mod_router_kernel(x_ref, g_ref, sh_ref, sc_ref, wr_ref, br_ref, o_ref, lg_ref):
    h = _norm_mod_body(x_ref, g_ref, sh_ref, sc_ref)
    o_ref[...] = h.astype(o_ref.dtype)
    lg_ref[...] = jnp.dot(h, wr_ref[...], precision=lax.Precision.HIGHEST,
                          preferred_element_type=F32) + br_ref[...]


def _mod_row(i, tm, n_ctx_rows, seq):
    n_ctx_tiles = n_ctx_rows // tm
    return jnp.where(i < n_ctx_tiles, 0, 1 + (i - n_ctx_tiles) // (seq // tm))


def _norm_mod(x, gain, mod, k_shift, k_scale, n_ctx_rows, seq, router=None, tm=256):
    rows, d = x.shape
    mrow = functools.partial(_mod_row, tm=tm, n_ctx_rows=n_ctx_rows, seq=seq)
    in_specs = [
        pl.BlockSpec((tm, d), lambda i: (i, 0)),
        pl.BlockSpec((1, d), lambda i: (0, 0)),
        pl.BlockSpec((None, 1, d), lambda i: (mrow(i), 0, k_shift)),
        pl.BlockSpec((None, 1, d), lambda i: (mrow(i), 0, k_scale)),
    ]
    args = [x, gain.reshape(1, d), mod, mod]
    if router is None:
        return pl.pallas_call(
            _norm_mod_kernel, grid=(rows // tm,), in_specs=in_specs,
            out_specs=pl.BlockSpec((tm, d), lambda i: (i, 0)),
            out_shape=jax.ShapeDtypeStruct((rows, d), BF16),
            compiler_params=_params(1), name="norm_mod")(*args)
    w_r, b_r = router
    n_e = w_r.shape[1]
    in_specs += [pl.BlockSpec((d, n_e), lambda i: (0, 0)), pl.BlockSpec((1, n_e), lambda i: (0, 0))]
    return pl.pallas_call(
        _norm_mod_router_kernel, grid=(rows // tm,), in_specs=in_specs,
        out_specs=[pl.BlockSpec((tm, d), lambda i: (i, 0)), pl.BlockSpec((tm, n_e), lambda i: (i, 0))],
        out_shape=[jax.ShapeDtypeStruct((rows, d), BF16), jax.ShapeDtypeStruct((rows, n_e), F32)],
        compiler_params=_params(1), name="norm_mod_router")(*args, w_r, b_r.reshape(1, n_e))


def _matmul_kernel(*refs, n_extra, n_out, epilogue, n_sub):
    x_ref, w_ref = refs[:2]
    extra = refs[2:2 + n_extra]
    outs = refs[2 + n_extra:2 + n_extra + n_out]
    wbf_ref = refs[-1]

    @pl.when(pl.program_id(1) == 0)
    def _():
        wbf_ref[...] = w_ref[...].astype(BF16)

    sub = x_ref.shape[0] // n_sub

    def dot(r):
        return jnp.dot(x_ref[r * sub:(r + 1) * sub, :], wbf_ref[...], preferred_element_type=F32)

    acc_next = dot(0)
    for r in range(n_sub):
        acc = acc_next
        if r + 1 < n_sub:
            acc_next = dot(r + 1)
        rows = slice(r * sub, (r + 1) * sub)
        epilogue(acc, [_RowView(e, rows) if e.shape[0] == x_ref.shape[0] else e for e in extra],
                 [_RowView(o, rows) for o in outs])


class _RowView:
    def __init__(self, ref, rows):
        self.ref, self.rows = ref, rows
        self.shape = (rows.stop - rows.start,) + tuple(ref.shape[1:])
        self.dtype = ref.dtype

    def _key(self, key):
        if key is Ellipsis:
            return (self.rows, slice(None))
        assert isinstance(key, tuple) and key[0] == slice(None), key
        return (self.rows,) + tuple(key[1:])

    def __getitem__(self, key):
        return self.ref[self._key(key)]

    def __setitem__(self, key, value):
        self.ref[self._key(key)] = value


def _matmul(x, w, *, w_lead=(), col0=0, n_cols, tm, tn, epilogue, extras=(), outs, name, row0=0, m=None):
    k = x.shape[1]
    m = x.shape[0] - row0 if m is None else m
    assert m % tm == 0 and row0 % tm == 0 and n_cols % tn == 0 and col0 % tn == 0
    lead = tuple(w_lead)
    w_spec = pl.BlockSpec((None,) * len(lead) + (k, tn), lambda j, i: lead + (0, col0 // tn + j))
    in_specs = [pl.BlockSpec((tm, k), lambda j, i: (row0 // tm + i, 0)), w_spec]
    in_specs += [pl.BlockSpec(bs, im) for _, bs, im in extras]
    out_specs = [pl.BlockSpec(bs, im) for _, bs, im in outs]
    kern = functools.partial(_matmul_kernel, n_extra=len(extras), n_out=len(outs), epilogue=epilogue,
                             n_sub=max(1, tm // MATMUL_SUB_ROWS))
    res = pl.pallas_call(
        kern, grid=(n_cols // tn, m // tm), in_specs=in_specs, out_specs=out_specs,
        out_shape=[s for s, _, _ in outs],
        scratch_shapes=[pltpu.VMEM((k, tn), BF16)],
        compiler_params=_params(2), name=name)(x, w, *[a for a, _, _ in extras])
    return res


def _ep_bias(acc, extra, outs):
    outs[0][...] = acc + extra[0][...]


def _ep_cast(acc, extra, outs):
    outs[0][...] = acc.astype(outs[0].dtype)


def _ep_gelu(acc, extra, outs):
    outs[0][...] = jax.nn.gelu(acc).astype(outs[0].dtype)


def _ep_head_rms(acc, extra, outs):
    g = extra[0][...]
    for h in range(acc.shape[1] // LANE):
        a = acc[:, h * LANE:(h + 1) * LANE]
        r = lax.rsqrt(jnp.mean(a * a, axis=-1, keepdims=True) + EPS)
        outs[0][:, h * LANE:(h + 1) * LANE] = (a * r * g).astype(outs[0].dtype)


def _ep_residual(acc, extra, outs):
    outs[0][...] = extra[0][...] + extra[1][...] * acc


def _rms(a, g):
    return a * lax.rsqrt(jnp.mean(a * a, axis=-1, keepdims=True) + EPS) * g


def _ep_mla_in(acc, extra, outs):
    outs[0][...] = _rms(acc[:, :Q_LORA], extra[0][...]).astype(BF16)
    outs[1][...] = _rms(acc[:, Q_LORA:Q_LORA + KV_LORA], extra[1][...]).astype(BF16)
    outs[2][...] = acc[:, Q_LORA + KV_LORA:]


def _rope_tail(t, tab_ref):
    c = tab_ref[:, 0:LANE]
    s_left = tab_ref[:, LANE:2 * LANE]
    s_right = tab_ref[:, 2 * LANE:3 * LANE]
    return (t * c + pltpu.roll(t, LANE - QK_ROPE // 2, axis=1) * s_left
            + pltpu.roll(t, QK_ROPE // 2, axis=1) * s_right)


def _ep_mla_q(acc, extra, outs):
    g_ref, tab_ref = extra
    g0 = g_ref[:, :LANE]
    g1 = g_ref[:, LANE:]
    for h in range(acc.shape[1] // QK_PAD):
        a0 = acc[:, h * QK_PAD:h * QK_PAD + LANE]
        a1 = acc[:, h * QK_PAD + LANE:(h + 1) * QK_PAD]
        ss = jnp.sum(a0 * a0, axis=-1, keepdims=True) + jnp.sum(a1 * a1, axis=-1, keepdims=True)
        r = lax.rsqrt(ss * (1.0 / QK_DIM) + EPS)
        outs[0][:, h * QK_PAD:h * QK_PAD + LANE] = (a0 * r * g0).astype(BF16)
        outs[0][:, h * QK_PAD + LANE:(h + 1) * QK_PAD] = _rope_tail(a1 * r * g1, tab_ref).astype(BF16)


def _ep_mla_kv(acc, extra, outs):
    g_ref, tab_ref, pe_ref = extra
    k_out, v_out = outs
    g0 = g_ref[:, :LANE]
    g1 = g_ref[:, LANE:]
    pe = pe_ref[...]
    pe_ss = jnp.sum(pe * pe, axis=-1, keepdims=True)
    for h in range(acc.shape[1] // (QK_NOPE + V_DIM)):
        base = h * (QK_NOPE + V_DIM)
        kn = acc[:, base:base + QK_NOPE]
        r = lax.rsqrt((jnp.sum(kn * kn, axis=-1, keepdims=True) + pe_ss) * (1.0 / QK_DIM) + EPS)
        k_out[:, h * QK_PAD:h * QK_PAD + LANE] = (kn * r * g0).astype(BF16)
        k_out[:, h * QK_PAD + LANE:(h + 1) * QK_PAD] = _rope_tail(pe * r * g1, tab_ref).astype(BF16)
        v_out[:, h * V_DIM:(h + 1) * V_DIM] = acc[:, base + QK_NOPE:base + QK_NOPE + V_DIM].astype(BF16)


def _softmax_pv(s_parts, v_parts):
    m = s_parts[0].max(axis=-1, keepdims=True)
    for s in s_parts[1:]:
        m = jnp.maximum(m, s.max(axis=-1, keepdims=True))
    l = 0.0
    o = 0.0
    for s, v in zip(s_parts, v_parts):
        p = jnp.exp(s - m)
        l = l + p.sum(axis=-1, keepdims=True)
        o = o + jnp.dot(p.astype(BF16), v, preferred_element_type=F32)
    return o / l


def _qk(q, k):
    return lax.dot_general(q, k, (((1,), (1,)), ((), ())), preferred_element_type=F32)


def _attn_kernel(*refs, has_ctx, n_sub):
    if has_ctx:
        q_ref, k_ref, v_ref, kc_ref, vc_ref, o_ref = refs
    else:
        q_ref, k_ref, v_ref, o_ref = refs
    sub = q_ref.shape[0] // n_sub

    def scores(r):
        q = q_ref[r * sub:(r + 1) * sub, :]
        parts = [_qk(q, k_ref[...])]
        if has_ctx:
            parts.append(_qk(q, kc_ref[...]))
        return parts

    s_next = scores(0)
    for r in range(n_sub):
        s_cur = s_next
        if r + 1 < n_sub:
            s_next = scores(r + 1)
        v_parts = [v_ref[...]] + ([vc_ref[...]] if has_ctx else [])
        o_ref[r * sub:(r + 1) * sub, :] = _softmax_pv(s_cur, v_parts).astype(o_ref.dtype)


def _attention(q, k, v, *, n_batch, n_heads, dq, dv, tq, n_q_tiles, q_blk0, t_k, k_blk0, t_ctx=None, n_sub=1, name):
    has_ctx = t_ctx is not None
    in_specs = [
        pl.BlockSpec((tq, dq), lambda b, h, i: (q_blk0 + b * n_q_tiles + i, h)),
        pl.BlockSpec((t_k, dq), lambda b, h, i: (k_blk0 + b, h)),
        pl.BlockSpec((t_k, dv), lambda b, h, i: (k_blk0 + b, h)),
    ]
    args = [q, k, v]
    if has_ctx:
        in_specs += [pl.BlockSpec((t_ctx, dq), lambda b, h, i: (b, h)),
                     pl.BlockSpec((t_ctx, dv), lambda b, h, i: (b, h))]
        args += [k, v]
    return pl.pallas_call(
        functools.partial(_attn_kernel, has_ctx=has_ctx, n_sub=n_sub),
        grid=(n_batch, n_heads, n_q_tiles), in_specs=in_specs,
        out_specs=pl.BlockSpec((tq, dv), lambda b, h, i: (b * n_q_tiles + i, h)),
        out_shape=jax.ShapeDtypeStruct((n_batch * n_q_tiles * tq, n_heads * dv), BF16),
        compiler_params=_params(3), name=name)(*args)


def _na_band_start(t, rows):
    return jnp.clip(t * NA_TILE_ROWS - NA_WIN_H // 2, 0, rows - NA_BAND_ROWS)


def _na_kernel(q_ref, k_ref, v_ref, kc_ref, vc_ref, bias_ref, o_ref, *, rows):
    t = pl.program_id(1)
    ks = pl.multiple_of(_na_band_start(t, rows) * GRID_W, GRID_W)
    band = pl.ds(ks, NA_BAND_ROWS * GRID_W)

    def scores(h):
        cols = slice(h * NA_HEAD_DIM, (h + 1) * NA_HEAD_DIM)
        q = q_ref[:, cols]
        return [_qk(q, k_ref[band, cols]) + bias_ref[h], _qk(q, kc_ref[:, cols])]

    s_next = scores(0)
    for h in range(NA_HEADS):
        s_cur = s_next
        if h + 1 < NA_HEADS:
            s_next = scores(h + 1)
        cols = slice(h * NA_HEAD_DIM, (h + 1) * NA_HEAD_DIM)
        o_ref[:, cols] = _softmax_pv(s_cur, [v_ref[band, cols], vc_ref[:, cols]]).astype(o_ref.dtype)


def _na_bias_tables(rel_bias, rows):
    n_tiles = rows // NA_TILE_ROWS
    n_r, n_c = 2 * NA_WIN_H - 1, 2 * NA_WIN_W - 1
    qr = np.arange(NA_TILE_ROWS)[:, None]
    ur = np.arange(NA_BAND_ROWS)[None, :]
    qc = np.arange(GRID_W)[:, None]
    kc = np.arange(GRID_W)[None, :]
    c0 = np.clip(qc - NA_WIN_W // 2, 0, GRID_W - NA_WIN_W)
    col_valid = (kc >= c0) & (kc < c0 + NA_WIN_W)
    col_onehot = np.eye(n_c, dtype=np.float32)[np.clip(kc - qc + NA_WIN_W - 1, 0, n_c - 1)]
    geoms, ids = [], []
    for t in range(n_tiles):
        u0 = int(np.clip(t * NA_TILE_ROWS - NA_WIN_H // 2, 0, rows - NA_BAND_ROWS))
        r = t * NA_TILE_ROWS + qr
        key_row = u0 + ur
        r0 = np.clip(r - NA_WIN_H // 2, 0, rows - NA_WIN_H)
        geom = ((key_row >= r0) & (key_row < r0 + NA_WIN_H), np.clip(key_row - r + NA_WIN_H - 1, 0, n_r - 1))
        for gi, g in enumerate(geoms):
            if all(np.array_equal(a, b) for a, b in zip(g, geom)):
                ids.append(gi)
                break
        else:
            ids.append(len(geoms))
            geoms.append(geom)
    row_valid = np.stack([g[0] for g in geoms])
    row_onehot = np.eye(n_r, dtype=np.float32)[np.stack([g[1] for g in geoms])]
    bias = jnp.einsum('gaur,hrc,bkc->ghabuk', row_onehot, rel_bias.astype(F32), col_onehot,
                      precision=lax.Precision.HIGHEST)
    valid = row_valid[:, None, :, None, :, None] & col_valid[None, None, None, :, None, :]
    tables = jnp.where(valid, bias, NEG)
    return tables.reshape(len(geoms), NA_HEADS, NA_TILE_ROWS * GRID_W, NA_BAND_ROWS * GRID_W), ids


def _na_attention(q, k, v, rel_bias, *, n_batch, seq, t_ctx):
    rows = seq // GRID_W
    n_tiles = rows // NA_TILE_ROWS
    tq = NA_TILE_ROWS * GRID_W
    tables, ids = _na_bias_tables(rel_bias, rows)
    assert ids == [0] + [1] * (n_tiles - 2) + [2], ids
    ctx_tiles = n_batch * t_ctx // tq
    ctx_units = n_batch * t_ctx // seq

    def table_id(t):
        return jnp.where(t == 0, 0, jnp.where(t == n_tiles - 1, 2, 1))

    return pl.pallas_call(
        functools.partial(_na_kernel, rows=rows),
        grid=(n_batch, n_tiles),
        in_specs=[
            pl.BlockSpec((tq, NA_WIDTH), lambda b, t: (ctx_tiles + b * n_tiles + t, 0)),
            pl.BlockSpec((seq, NA_WIDTH), lambda b, t: (ctx_units + b, 0)),
            pl.BlockSpec((seq, NA_WIDTH), lambda b, t: (ctx_units + b, 0)),
            pl.BlockSpec((t_ctx, NA_WIDTH), lambda b, t: (b, 0)),
            pl.BlockSpec((t_ctx, NA_WIDTH), lambda b, t: (b, 0)),
            pl.BlockSpec((None, NA_HEADS, tq, NA_BAND_ROWS * GRID_W), lambda b, t: (table_id(t), 0, 0, 0)),
        ],
        out_specs=pl.BlockSpec((tq, NA_WIDTH), lambda b, t: (b * n_tiles + t, 0)),
        out_shape=jax.ShapeDtypeStruct((n_batch * seq, NA_WIDTH), BF16),
        compiler_params=_params(2), name="na_attention")(q, k, v, k, v, tables)


def _sg_kernel(uv_ref, g_ref, b_ref, ws_ref, bs_ref, o_ref, *, chunks):
    for c in range(chunks):
        r = slice(c * SG_CHUNK, (c + 1) * SG_CHUNK)
        z = uv_ref[r, SG_WIDTH:].astype(F32)
        mu = jnp.mean(z, axis=-1, keepdims=True)
        zc = z - mu
        var = jnp.mean(zc * zc, axis=-1, keepdims=True)
        zn = (zc * lax.rsqrt(var + EPS) * g_ref[...] + b_ref[...]).astype(BF16)
        for g in range(SG_GROUPS):
            cols = slice(g * SG_GROUP_DIM, (g + 1) * SG_GROUP_DIM)
            mixed = jnp.dot(ws_ref[g], zn[:, cols], preferred_element_type=F32) + bs_ref[g]
            o_ref[r, cols] = (uv_ref[r, cols].astype(F32) * mixed).astype(o_ref.dtype)


def _spatial_gating(uv, ln_g, ln_b, w_s, b_s, tm=512):
    rows = uv.shape[0]
    bs = jnp.broadcast_to(b_s.astype(F32)[:, :, None], (SG_GROUPS, SG_CHUNK, SG_GROUP_DIM))
    return pl.pallas_call(
        functools.partial(_sg_kernel, chunks=tm // SG_CHUNK),
        grid=(rows // tm,),
        in_specs=[
            pl.BlockSpec((tm, 2 * SG_WIDTH), lambda i: (i, 0)),
            pl.BlockSpec((1, SG_WIDTH), lambda i: (0, 0)),
            pl.BlockSpec((1, SG_WIDTH), lambda i: (0, 0)),
            pl.BlockSpec((SG_GROUPS, SG_CHUNK, SG_CHUNK), lambda i: (0, 0, 0)),
            pl.BlockSpec((SG_GROUPS, SG_CHUNK, SG_GROUP_DIM), lambda i: (0, 0, 0)),
        ],
        out_specs=pl.BlockSpec((tm, SG_WIDTH), lambda i: (i, 0)),
        out_shape=jax.ShapeDtypeStruct((rows, SG_WIDTH), BF16),
        compiler_params=_params(1), name="spatial_gating")(
            uv, ln_g.reshape(1, SG_WIDTH), ln_b.reshape(1, SG_WIDTH), w_s.astype(BF16), bs)


def _tile_weights_changed(i, te_ref):
    return (i == 0) | (te_ref[i] != te_ref[jnp.maximum(i - 1, 0)])


def _swiglu(g, u):
    g = jnp.minimum(g, SWIGLU_LIMIT)
    u = jnp.clip(u, -SWIGLU_LIMIT, SWIGLU_LIMIT)
    return (u + 1.0) * (g * jax.nn.sigmoid(SWIGLU_ALPHA * g))


def _for_each_valid_part(i, tv_ref, o_ref, compute):
    part = o_ref.shape[0] // MOE_PARTS
    n_valid = (tv_ref[i] + part - 1) // part
    rows = [slice(p * part, (p + 1) * part) for p in range(MOE_PARTS)]
    for n in range(MOE_PARTS + 1):

        @pl.when(n_valid == n)
        def _(n=n):
            acc_next = compute.matmul(rows[0]) if n else None
            for p in range(n):
                acc = acc_next
                if p + 1 < n:
                    acc_next = compute.matmul(rows[p + 1])
                o_ref[rows[p], :] = compute.epilogue(acc).astype(o_ref.dtype)
            for p in range(n, MOE_PARTS):
                o_ref[rows[p], :] = jnp.zeros((part, o_ref.shape[1]), o_ref.dtype)


class _GateUp:
    def __init__(self, x_ref, wg_bf, wu_bf, bg_ref, bu_ref):
        self.x_ref, self.wg_bf, self.wu_bf, self.bg_ref, self.bu_ref = x_ref, wg_bf, wu_bf, bg_ref, bu_ref

    def matmul(self, rows):
        x = self.x_ref[rows, :]
        return (jnp.dot(x, self.wg_bf[...], preferred_element_type=F32),
                jnp.dot(x, self.wu_bf[...], preferred_element_type=F32))

    def epilogue(self, acc):
        return _swiglu(acc[0] + self.bg_ref[...], acc[1] + self.bu_ref[...])


class _Down:
    def __init__(self, h_ref, w_bf, b_ref):
        self.h_ref, self.w_bf, self.b_ref = h_ref, w_bf, b_ref

    def matmul(self, rows):
        return jnp.dot(self.h_ref[rows, :], self.w_bf[...], preferred_element_type=F32)

    def epilogue(self, acc):
        return acc + self.b_ref[...]


def _moe_gu_kernel(te_ref, tv_ref, ts_ref, x_ref, wg_ref, wu_ref, bg_ref, bu_ref, o_ref, wg_bf, wu_bf):
    i = pl.program_id(1)

    @pl.when(_tile_weights_changed(i, te_ref))
    def _():
        wg_bf[...] = wg_ref[...].astype(BF16)
        wu_bf[...] = wu_ref[...].astype(BF16)

    _for_each_valid_part(i, tv_ref, o_ref, _GateUp(x_ref, wg_bf, wu_bf, bg_ref, bu_ref))


def _moe_down_kernel(te_ref, tv_ref, ts_ref, h_ref, w_ref, b_ref, o_ref, w_bf):
    i = pl.program_id(1)

    @pl.when(_tile_weights_changed(i, te_ref))
    def _():
        w_bf[...] = w_ref[...].astype(BF16)

    _for_each_valid_part(i, tv_ref, o_ref, _Down(h_ref, w_bf, b_ref))


def _moe_experts(x_sorted, tile_e, tile_valid, tile_src, layer, w_gu, b_gu, w_down, b_down):
    r_pad, d = x_sorted.shape
    tm, tn, tn_down = MOE_TM, MOE_TN_GU, MOE_TN_DOWN
    n_tiles = r_pad // tm
    n_j = D_FF // tn
    x_map = lambda j, i, te, tv, ts: (ts[i], 0)
    out_map = lambda j, i, te, tv, ts: (i, j)

    hid = pl.pallas_call(
        _moe_gu_kernel,
        grid_spec=pltpu.PrefetchScalarGridSpec(
            num_scalar_prefetch=3, grid=(n_j, n_tiles),
            in_specs=[
                pl.BlockSpec((tm, d), x_map),
                pl.BlockSpec((None, None, d, tn), lambda j, i, te, tv, ts: (layer, te[i], 0, j)),
                pl.BlockSpec((None, None, d, tn), lambda j, i, te, tv, ts: (layer, te[i], 0, n_j + j)),
                pl.BlockSpec((None, None, 1, tn), lambda j, i, te, tv, ts: (layer, te[i], 0, j)),
                pl.BlockSpec((None, None, 1, tn), lambda j, i, te, tv, ts: (layer, te[i], 0, n_j + j)),
            ],
            out_specs=pl.BlockSpec((tm, tn), out_map),
            scratch_shapes=[pltpu.VMEM((d, tn), BF16), pltpu.VMEM((d, tn), BF16)]),
        out_shape=jax.ShapeDtypeStruct((r_pad, D_FF), BF16),
        compiler_params=_params(2), name="moe_gate_up")(
            tile_e, tile_valid, tile_src, x_sorted, w_gu, w_gu, b_gu[:, :, None, :], b_gu[:, :, None, :])

    return pl.pallas_call(
        _moe_down_kernel,
        grid_spec=pltpu.PrefetchScalarGridSpec(
            num_scalar_prefetch=3, grid=(d // tn_down, n_tiles),
            in_specs=[
                pl.BlockSpec((tm, D_FF), x_map),
                pl.BlockSpec((None, None, D_FF, tn_down), lambda j, i, te, tv, ts: (layer, te[i], 0, j)),
                pl.BlockSpec((None, None, 1, tn_down), lambda j, i, te, tv, ts: (layer, te[i], 0, j)),
            ],
            out_specs=pl.BlockSpec((tm, tn_down), out_map),
            scratch_shapes=[pltpu.VMEM((D_FF, tn_down), BF16)]),
        out_shape=jax.ShapeDtypeStruct((r_pad, d), BF16),
        compiler_params=_params(2), name="moe_down")(
            tile_e, tile_valid, tile_src, hid, w_down, b_down[:, :, None, :])


def _moe_route(logits):
    n = logits.shape[0]
    nk = n * TOP_K
    tm = MOE_TM
    i32 = jnp.int32
    top_val, top_idx = lax.top_k(logits, TOP_K)
    gate = jax.nn.softmax(top_val, axis=-1)
    flat_e = top_idx.reshape(nk).astype(i32)
    iota = jnp.arange(nk, dtype=i32)
    _, order = lax.sort((flat_e, iota), num_keys=1)
    _, inv_order = lax.sort((order, iota), num_keys=1)
    onehot = flat_e[:, None] == jnp.arange(N_EXPERTS, dtype=i32)[None, :]
    counts = jnp.sum(onehot, axis=0, dtype=i32)
    padded = (counts + tm - 1) // tm * tm
    pad_end = jnp.cumsum(padded)
    pad_start = pad_end - padded
    start = jnp.cumsum(counts) - counts
    pos = inv_order + jnp.sum(jnp.where(onehot, (pad_start - start)[None, :], 0), axis=1)
    pos = pos.reshape(n, TOP_K).T.reshape(nk)

    n_tiles = nk // tm + N_EXPERTS
    n_used = pad_end[-1] // tm
    tile_src = jnp.minimum(jnp.arange(n_tiles, dtype=i32), n_used - 1)
    tile_e = jnp.minimum(jnp.sum(pad_end[None, :] <= (tile_src * tm)[:, None], axis=1, dtype=i32), N_EXPERTS - 1)
    in_use = jnp.arange(n_tiles, dtype=i32) < n_used
    rank0 = tile_src * tm - pad_start[tile_e]
    tile_valid = jnp.where(in_use, jnp.clip(counts[tile_e] - rank0, 0, tm), 0).astype(i32)
    within = jnp.arange(tm, dtype=i32)[None, :]
    valid = within < tile_valid[:, None]
    src = jnp.clip((start[tile_e] + rank0)[:, None] + within, 0, nk - 1)
    filler = (jnp.arange(n_tiles * tm, dtype=i32) % n).reshape(n_tiles, tm)
    rows_tok = jnp.where(valid, order.at[src].get(mode="promise_in_bounds") // TOP_K, filler).reshape(n_tiles * tm)
    return gate, rows_tok, pos, tile_e, tile_valid, tile_src.astype(i32)


def _moe_combine_kernel(x_ref, g2_ref, gate_ref, y_ref, o_ref):
    gate = gate_ref[...]
    f = gate[:, 0:1] * y_ref[0].astype(F32)
    for k in range(1, TOP_K):
        f = f + gate[:, k:k + 1] * y_ref[k].astype(F32)
    o_ref[...] = x_ref[...] + g2_ref[...] * f


def _moe_block(xs, h2, logits, mod, layer, n_ctx_rows, seq, w_gu, b_gu, w_down, b_down, tm=256):
    n, d = h2.shape
    gate, rows_tok, pos, tile_e, tile_valid, tile_src = _moe_route(logits)
    x_sorted = h2.at[rows_tok].get(mode="promise_in_bounds")
    y = _moe_experts(x_sorted, tile_e, tile_valid, tile_src, layer, w_gu, b_gu, w_down, b_down)
    y_k = y.at[pos].get(mode="promise_in_bounds").reshape(TOP_K, n, d)
    mrow = functools.partial(_mod_row, tm=tm, n_ctx_rows=n_ctx_rows, seq=seq)
    return pl.pallas_call(
        _moe_combine_kernel, grid=(n // tm,),
        in_specs=[
            pl.BlockSpec((tm, d), lambda i: (i, 0)),
            pl.BlockSpec((None, 1, d), lambda i: (mrow(i), 0, 5)),
            pl.BlockSpec((tm, TOP_K), lambda i: (i, 0)),
            pl.BlockSpec((TOP_K, tm, d), lambda i: (0, i, 0)),
        ],
        out_specs=pl.BlockSpec((tm, d), lambda i: (i, 0)),
        out_shape=jax.ShapeDtypeStruct((n, d), F32),
        compiler_params=_params(1), name="moe_combine")(xs, mod, gate, y_k)


def _axial_rope_tables(seq, tm):
    half = QK_ROPE // 2
    t = jnp.arange(seq)
    row = (t // GRID_W).astype(F32)
    col = (t % GRID_W).astype(F32)
    n_freq = QK_ROPE // 4
    inv = ROPE_THETA ** (-jnp.arange(n_freq, dtype=F32) / n_freq)
    ang = jnp.concatenate([row[:, None] * inv, col[:, None] * inv], axis=-1)
    cos, sin = jnp.cos(ang), jnp.sin(ang)
    z = jnp.zeros((seq, half), F32)
    zpad = jnp.zeros((seq, LANE - 2 * half), F32)
    c = jnp.concatenate([cos, cos, zpad], axis=-1)
    s_left = jnp.concatenate([-sin, z, zpad], axis=-1)
    s_right = jnp.concatenate([z, sin, zpad], axis=-1)
    tab = jnp.concatenate([c, s_left, s_right], axis=-1)
    ident = jnp.concatenate([jnp.ones((tm, 2 * half), F32), jnp.zeros((tm, 3 * LANE - 2 * half), F32)], axis=-1)
    return jnp.concatenate([ident, tab], axis=0)


def _ada_mod(cvec, ada_w, ada_b, layer):
    n = cvec.shape[0]
    m_pad = 16
    a = jnp.zeros((m_pad, D_MODEL), F32).at[:n].set(jax.nn.silu(cvec)).astype(BF16)
    n_cols = ada_w.shape[-1]
    tn = 1024
    (mod,) = _matmul(
        a, ada_w, w_lead=(layer,), n_cols=n_cols, tm=m_pad, tn=tn, epilogue=_ep_bias,
        extras=[(ada_b[:, None, :], (None, 1, tn), lambda j, i: (layer, 0, j))],
        outs=[(jax.ShapeDtypeStruct((m_pad, n_cols), F32), (m_pad, tn), lambda j, i: (i, j))],
        name="ada_mod")
    return mod[:n, None, :]


def kernel(x, c, ctx, c_ctx, ada_w, ada_b, norm1_g, norm2_g, ab_w_in, ab_w_out, na_q_g, na_k_g, na_rel_bias,
           sg_norm_g, sg_norm_b, sg_w, sg_b, mla_w_in, mla_q_norm_g, mla_kv_norm_g, mla_w_uq, mla_w_ukv,
           mla_q_g, mla_k_g, mla_w_out, moe_w_router, moe_b_router, moe_w_gu, moe_b_gu, moe_w_down, moe_b_down):
    n_batch, seq, d = x.shape
    t_ctx = ctx.shape[1]
    n_ctx = n_batch * t_ctx
    n_lat = n_batch * seq
    n_all = n_ctx + n_lat
    tm = MATMUL_TM
    tn = 512
    cvec = jnp.concatenate([c_ctx[None, :], c], axis=0)
    xs = jnp.concatenate([ctx.reshape(n_ctx, d), x.reshape(n_lat, d)], axis=0)
    mrow = functools.partial(_mod_row, tm=tm, n_ctx_rows=n_ctx, seq=seq)

    mod = _ada_mod(cvec, ada_w, ada_b, 0)
    h = _norm_mod(xs, norm1_g[0], mod, 0, 1, n_ctx, seq)
    w_in = ab_w_in
    q_gain = (na_q_g[0] * (NA_HEAD_DIM ** -0.5)).reshape(1, NA_HEAD_DIM)
    k_gain = na_k_g[0].reshape(1, NA_HEAD_DIM)

    def in_proj(col0, n_cols, epilogue, extras, name):
        (o,) = _matmul(h, w_in, w_lead=(0,), col0=col0, n_cols=n_cols, tm=tm, tn=tn, epilogue=epilogue,
                       extras=extras,
                       outs=[(jax.ShapeDtypeStruct((n_all, n_cols), BF16), (tm, tn), lambda j, i: (i, j))],
                       name=name)
        return o

    gain_spec = lambda g: [(g, (1, NA_HEAD_DIM), lambda j, i: (0, 0))]
    q = in_proj(0, NA_WIDTH, _ep_head_rms, gain_spec(q_gain), "ab_in_q")
    k = in_proj(NA_WIDTH, NA_WIDTH, _ep_head_rms, gain_spec(k_gain), "ab_in_k")
    v = in_proj(2 * NA_WIDTH, NA_WIDTH, _ep_cast, [], "ab_in_v")
    uv = in_proj(3 * NA_WIDTH, 2 * SG_WIDTH, _ep_gelu, [], "ab_in_uv")

    a_lat = _na_attention(q, k, v, na_rel_bias[0], n_batch=n_batch, seq=seq, t_ctx=t_ctx)
    a_ctx = _attention(q, k, v, n_batch=n_batch, n_heads=NA_HEADS, dq=NA_HEAD_DIM, dv=NA_HEAD_DIM,
                       tq=t_ctx, n_q_tiles=1, q_blk0=0, t_k=t_ctx, k_blk0=0, name="ctx_attention")
    gated = _spatial_gating(uv, sg_norm_g[0], sg_norm_b[0], sg_w[0], sg_b[0])
    mix = jnp.concatenate([jnp.concatenate([a_ctx, a_lat], axis=0), gated], axis=1)

    def out_proj(inp, w, resid, resid_row0, n_ctx_rows, name):
        mr = functools.partial(_mod_row, tm=tm, n_ctx_rows=n_ctx_rows, seq=seq)
        (o,) = _matmul(inp, w, w_lead=(0,), n_cols=d, tm=tm, tn=tn, epilogue=_ep_residual,
                       extras=[(resid, (tm, tn), lambda j, i: (resid_row0 // tm + i, j)),
                               (mod, (None, 1, tn), lambda j, i: (mr(i), 0, 2 * d // tn + j))],
                       outs=[(jax.ShapeDtypeStruct((inp.shape[0], d), F32), (tm, tn), lambda j, i: (i, j))],
                       name=name)
        return o

    xs = out_proj(mix, ab_w_out, xs, 0, n_ctx, "ab_out")
    h2, logits = _norm_mod(xs, norm2_g[0], mod, 3, 4, n_ctx, seq, router=(moe_w_router[0], moe_b_router[0]))
    xs = _moe_block(xs, h2, logits, mod, 0, n_ctx, seq, moe_w_gu, moe_b_gu, moe_w_down, moe_b_down)

    mod = _ada_mod(cvec, ada_w, ada_b, 1)
    h = _norm_mod(xs, norm1_g[1], mod, 0, 1, n_ctx, seq)
    w_in_pad = jnp.pad(mla_w_in, ((0, 0), (0, 0), (0, MLA_IN_PAD - mla_w_in.shape[-1])))
    c_q, c_kv, k_pe = _matmul(
        h, w_in_pad, w_lead=(0,), n_cols=MLA_IN_PAD, tm=tm, tn=MLA_IN_PAD, epilogue=_ep_mla_in,
        extras=[(mla_q_norm_g[0].reshape(1, Q_LORA), (1, Q_LORA), lambda j, i: (0, 0)),
                (mla_kv_norm_g[0].reshape(1, KV_LORA), (1, KV_LORA), lambda j, i: (0, 0))],
        outs=[(jax.ShapeDtypeStruct((n_all, Q_LORA), BF16), (tm, Q_LORA), lambda j, i: (i, 0)),
              (jax.ShapeDtypeStruct((n_all, KV_LORA), BF16), (tm, KV_LORA), lambda j, i: (i, 0)),
              (jax.ShapeDtypeStruct((n_all, LANE), F32), (tm, LANE), lambda j, i: (i, 0))],
        name="mla_in")

    rope_tab = _axial_rope_tables(seq, tm)
    n_ctx_tiles = n_ctx // tm
    seq_tiles = seq // tm
    pad_gain = lambda g, s: jnp.pad(g * s, (0, QK_PAD - QK_DIM)).reshape(1, QK_PAD)
    w_uq = jnp.pad(mla_w_uq[0].reshape(Q_LORA, MLA_HEADS, QK_DIM),
                   ((0, 0), (0, 0), (0, QK_PAD - QK_DIM))).reshape(Q_LORA, MLA_HEADS * QK_PAD)
    tn_up = 1024
    (q,) = _matmul(
        c_q, w_uq, row0=n_ctx, n_cols=MLA_HEADS * QK_PAD, tm=tm, tn=tn_up, epilogue=_ep_mla_q,
        extras=[(pad_gain(mla_q_g[0], QK_DIM ** -0.5), (1, QK_PAD), lambda j, i: (0, 0)),
                (rope_tab, (tm, 3 * LANE), lambda j, i: (1 + i % seq_tiles, 0))],
        outs=[(jax.ShapeDtypeStruct((n_lat, MLA_HEADS * QK_PAD), BF16), (tm, tn_up), lambda j, i: (i, j))],
        name="mla_up_q")
    heads_per_tile = tn_up // (QK_NOPE + V_DIM)
    k, v = _matmul(
        c_kv, mla_w_ukv, w_lead=(0,), n_cols=MLA_HEADS * (QK_NOPE + V_DIM), tm=tm, tn=tn_up, epilogue=_ep_mla_kv,
        extras=[(pad_gain(mla_k_g[0], 1.0), (1, QK_PAD), lambda j, i: (0, 0)),
                (rope_tab, (tm, 3 * LANE),
                 lambda j, i: (jnp.where(i < n_ctx_tiles, 0, 1 + (i - n_ctx_tiles) % seq_tiles), 0)),
                (k_pe, (tm, LANE), lambda j, i: (i, 0))],
        outs=[(jax.ShapeDtypeStruct((n_all, MLA_HEADS * QK_PAD), BF16), (tm, heads_per_tile * QK_PAD),
               lambda j, i: (i, j)),
              (jax.ShapeDtypeStruct((n_all, MLA_HEADS * V_DIM), BF16), (tm, heads_per_tile * V_DIM),
               lambda j, i: (i, j))],
        name="mla_up_kv")

    tq = 1024
    attn = _attention(q, k, v, n_batch=n_batch, n_heads=MLA_HEADS, dq=QK_PAD, dv=V_DIM, tq=tq,
                      n_q_tiles=seq // tq, q_blk0=0, t_k=seq, k_blk0=n_ctx // seq, t_ctx=t_ctx, n_sub=4,
                      name="mla_attention")
    x_lat = out_proj(attn, mla_w_out, xs, n_ctx, 0, "mla_out")
    h2, logits = _norm_mod(x_lat, norm2_g[1], mod, 3, 4, 0, seq, router=(moe_w_router[1], moe_b_router[1]))
    x_lat = _moe_block(x_lat, h2, logits, mod, 1, 0, seq, moe_w_gu, moe_b_gu, moe_w_down, moe_b_down)
    return x_lat.reshape(n_batch, seq, d)
```

```python
import functools

import numpy as np
import jax
import jax.numpy as jnp
from jax import lax
from jax.experimental import pallas as pl
from jax.experimental.pallas import tpu as pltpu

F32 = jnp.float32
BF16 = jnp.bfloat16

D_MODEL = 2048
GRID_W = 64
EPS = 1e-6
NEG = -1e30

NA_HEADS = 8
NA_HEAD_DIM = 128
NA_WIDTH = NA_HEADS * NA_HEAD_DIM
NA_WIN_H = 8
NA_WIN_W = 16
SG_GROUPS = 8
SG_WIDTH = D_MODEL // 2
SG_GROUP_DIM = SG_WIDTH // SG_GROUPS
SG_CHUNK = 128

MLA_HEADS = 16
Q_LORA = 512
KV_LORA = 256
QK_NOPE = 128
QK_ROPE = 64
V_DIM = 128
QK_DIM = QK_NOPE + QK_ROPE
QK_PAD = 256
MLA_IN_PAD = Q_LORA + KV_LORA + 128
ROPE_THETA = 10000.0

N_EXPERTS = 32
TOP_K = 4
D_FF = D_MODEL
SWIGLU_ALPHA = 1.702
SWIGLU_LIMIT = 7.0

LANE = 128
V7X_VMEM_LIMIT = 56 * 1024 * 1024

MATMUL_TM = 1024
MATMUL_SUB_ROWS = 256
NA_TILE_ROWS = 4
NA_BAND_ROWS = NA_TILE_ROWS + NA_WIN_H - 1
MOE_TM = 1024
MOE_PARTS = 4
MOE_TN_GU = 1024
MOE_TN_DOWN = 2048


def _params(n_axes):
    return pltpu.CompilerParams(dimension_semantics=("arbitrary",) * n_axes,
                                vmem_limit_bytes=V7X_VMEM_LIMIT)


def _norm_mod_body(x_ref, g_ref, sh_ref, sc_ref):
    x = x_ref[...]
    y = x * lax.rsqrt(jnp.mean(x * x, axis=-1, keepdims=True) + EPS) * g_ref[...]
    return y * (1.0 + sc_ref[...]) + sh_ref[...]


def _norm_mod_kernel(x_ref, g_ref, sh_ref, sc_ref, o_ref):
    o_ref[...] = _norm_mod_body(x_ref, g_ref, sh_ref, sc_ref).astype(o_ref.dtype)


def _norm_mod_router_kernel(x_ref, g_ref, sh_ref, sc_ref, wcat_ref, br_ref, o_ref, lg_ref):
    h = _norm_mod_body(x_ref, g_ref, sh_ref, sc_ref)
    h_hi = h.astype(BF16)
    o_ref[...] = h_hi
    n_e = lg_ref.shape[1]
    h_lo = (h - h_hi.astype(F32)).astype(BF16)
    a = jnp.dot(h_hi, wcat_ref[...], preferred_element_type=F32)
    b = jnp.dot(h_lo, wcat_ref[:, :n_e], preferred_element_type=F32)
    lg_ref[...] = a[:, :n_e] + a[:, n_e:] + b + br_ref[...]


def _mod_row(i, tm, n_ctx_rows, seq):
    n_ctx_tiles = n_ctx_rows // tm
    return jnp.where(i < n_ctx_tiles, 0, 1 + (i - n_ctx_tiles) // (seq // tm))


def _norm_mod(x, gain, mod, k_shift, k_scale, n_ctx_rows, seq, router=None, tm=256):
    rows, d = x.shape
    mrow = functools.partial(_mod_row, tm=tm, n_ctx_rows=n_ctx_rows, seq=seq)
    in_specs = [
        pl.BlockSpec((tm, d), lambda i: (i, 0)),
        pl.BlockSpec((1, d), lambda i: (0, 0)),
        pl.BlockSpec((None, 1, d), lambda i: (mrow(i), 0, k_shift)),
        pl.BlockSpec((None, 1, d), lambda i: (mrow(i), 0, k_scale)),
    ]
    args = [x, gain.reshape(1, d), mod, mod]
    if router is None:
        return pl.pallas_call(
            _norm_mod_kernel, grid=(rows // tm,), in_specs=in_specs,
            out_specs=pl.BlockSpec((tm, d), lambda i: (i, 0)),
            out_shape=jax.ShapeDtypeStruct((rows, d), BF16),
            compiler_params=_params(1), name="norm_mod")(*args)
    w_r, b_r = router
    n_e = w_r.shape[1]
    w_hi = w_r.astype(BF16)
    w_cat = jnp.concatenate([w_hi, (w_r - w_hi.astype(F32)).astype(BF16)], axis=1)
    in_specs += [pl.BlockSpec((d, 2 * n_e), lambda i: (0, 0)), pl.BlockSpec((1, n_e), lambda i: (0, 0))]
    return pl.pallas_call(
        _norm_mod_router_kernel, grid=(rows // tm,), in_specs=in_specs,
        out_specs=[pl.BlockSpec((tm, d), lambda i: (i, 0)), pl.BlockSpec((tm, n_e), lambda i: (i, 0))],
        out_shape=[jax.ShapeDtypeStruct((rows, d), BF16), jax.ShapeDtypeStruct((rows, n_e), F32)],
        compiler_params=_params(1), name="norm_mod_router")(*args, w_cat, b_r.reshape(1, n_e))


def _matmul_kernel(*refs, n_extra, n_out, epilogue, n_sub):
    x_ref, w_ref = refs[:2]
    extra = refs[2:2 + n_extra]
    outs = refs[2 + n_extra:2 + n_extra + n_out]
    wbf_ref = refs[-1]

    @pl.when(pl.program_id(1) == 0)
    def _():
        wbf_ref[...] = w_ref[...].astype(BF16)

    sub = x_ref.shape[0] // n_sub

    def dot(r):
        return jnp.dot(x_ref[r * sub:(r + 1) * sub, :], wbf_ref[...], preferred_element_type=F32)

    acc_next = dot(0)
    for r in range(n_sub):
        acc = acc_next
        if r + 1 < n_sub:
            acc_next = dot(r + 1)
        rows = slice(r * sub, (r + 1) * sub)
        epilogue(acc, [_RowView(e, rows) if e.shape[0] == x_ref.shape[0] else e for e in extra],
                 [_RowView(o, rows) for o in outs])


class _RowView:
    def __init__(self, ref, rows):
        self.ref, self.rows = ref, rows
        self.shape = (rows.stop - rows.start,) + tuple(ref.shape[1:])
        self.dtype = ref.dtype

    def _key(self, key):
        if key is Ellipsis:
            return (self.rows, slice(None))
        assert isinstance(key, tuple) and key[0] == slice(None), key
        return (self.rows,) + tuple(key[1:])

    def __getitem__(self, key):
        return self.ref[self._key(key)]

    def __setitem__(self, key, value):
        self.ref[self._key(key)] = value


def _matmul(x, w, *, w_lead=(), col0=0, n_cols, tm, tn, epilogue, extras=(), outs, name, row0=0, m=None):
    k = x.shape[1]
    m = x.shape[0] - row0 if m is None else m
    assert m % tm == 0 and row0 % tm == 0 and n_cols % tn == 0 and col0 % tn == 0
    lead = tuple(w_lead)
    w_spec = pl.BlockSpec((None,) * len(lead) + (k, tn), lambda j, i: lead + (0, col0 // tn + j))
    in_specs = [pl.BlockSpec((tm, k), lambda j, i: (row0 // tm + i, 0)), w_spec]
    in_specs += [pl.BlockSpec(bs, im) for _, bs, im in extras]
    out_specs = [pl.BlockSpec(bs, im) for _, bs, im in outs]
    kern = functools.partial(_matmul_kernel, n_extra=len(extras), n_out=len(outs), epilogue=epilogue,
                             n_sub=max(1, tm // MATMUL_SUB_ROWS))
    res = pl.pallas_call(
        kern, grid=(n_cols // tn, m // tm), in_specs=in_specs, out_specs=out_specs,
        out_shape=[s for s, _, _ in outs],
        scratch_shapes=[pltpu.VMEM((k, tn), BF16)],
        compiler_params=_params(2), name=name)(x, w, *[a for a, _, _ in extras])
    return res


def _ep_bias(acc, extra, outs):
    outs[0][...] = acc + extra[0][...]


def _ep_cast(acc, extra, outs):
    outs[0][...] = acc.astype(outs[0].dtype)


def _ep_gelu(acc, extra, outs):
    outs[0][...] = jax.nn.gelu(acc).astype(outs[0].dtype)


def _ep_head_rms(acc, extra, outs):
    g = extra[0][...]
    for h in range(acc.shape[1] // LANE):
        a = acc[:, h * LANE:(h + 1) * LANE]
        r = lax.rsqrt(jnp.mean(a * a, axis=-1, keepdims=True) + EPS)
        outs[0][:, h * LANE:(h + 1) * LANE] = (a * r * g).astype(outs[0].dtype)


def _ep_residual(acc, extra, outs):
    outs[0][...] = extra[0][...] + extra[1][...] * acc


def _rms(a, g):
    return a * lax.rsqrt(jnp.mean(a * a, axis=-1, keepdims=True) + EPS) * g


def _ep_mla_in(acc, extra, outs):
    outs[0][...] = _rms(acc[:, :Q_LORA], extra[0][...]).astype(BF16)
    outs[1][...] = _rms(acc[:, Q_LORA:Q_LORA + KV_LORA], extra[1][...]).astype(BF16)
    outs[2][...] = acc[:, Q_LORA + KV_LORA:]


def _rope_tail(t, tab_ref):
    c = tab_ref[:, 0:LANE]
    s_left = tab_ref[:, LANE:2 * LANE]
    s_right = tab_ref[:, 2 * LANE:3 * LANE]
    return (t * c + pltpu.roll(t, LANE - QK_ROPE // 2, axis=1) * s_left
            + pltpu.roll(t, QK_ROPE // 2, axis=1) * s_right)


def _ep_mla_q(acc, extra, outs):
    g_ref, tab_ref = extra
    g0 = g_ref[:, :LANE]
    g1 = g_ref[:, LANE:]
    for h in range(acc.shape[1] // QK_PAD):
        a0 = acc[:, h * QK_PAD:h * QK_PAD + LANE]
        a1 = acc[:, h * QK_PAD + LANE:(h + 1) * QK_PAD]
        ss = jnp.sum(a0 * a0, axis=-1, keepdims=True) + jnp.sum(a1 * a1, axis=-1, keepdims=True)
        r = lax.rsqrt(ss * (1.0 / QK_DIM) + EPS)
        outs[0][:, h * QK_PAD:h * QK_PAD + LANE] = (a0 * r * g0).astype(BF16)
        outs[0][:, h * QK_PAD + LANE:(h + 1) * QK_PAD] = _rope_tail(a1 * r * g1, tab_ref).astype(BF16)


def _ep_mla_kv(acc, extra, outs):
    g_ref, tab_ref, pe_ref = extra
    k_out, v_out = outs
    g0 = g_ref[:, :LANE]
    g1 = g_ref[:, LANE:]
    pe = pe_ref[...]
    pe_ss = jnp.sum(pe * pe, axis=-1, keepdims=True)
    for h in range(acc.shape[1] // (QK_NOPE + V_DIM)):
        base = h * (QK_NOPE + V_DIM)
        kn = acc[:, base:base + QK_NOPE]
        r = lax.rsqrt((jnp.sum(kn * kn, axis=-1, keepdims=True) + pe_ss) * (1.0 / QK_DIM) + EPS)
        k_out[:, h * QK_PAD:h * QK_PAD + LANE] = (kn * r * g0).astype(BF16)
        k_out[:, h * QK_PAD + LANE:(h + 1) * QK_PAD] = _rope_tail(pe * r * g1, tab_ref).astype(BF16)
        v_out[:, h * V_DIM:(h + 1) * V_DIM] = acc[:, base + QK_NOPE:base + QK_NOPE + V_DIM].astype(BF16)


def _softmax_pv(s_parts, v_parts):
    m = s_parts[0].max(axis=-1, keepdims=True)
    for s in s_parts[1:]:
        m = jnp.maximum(m, s.max(axis=-1, keepdims=True))
    l = 0.0
    o = 0.0
    for s, v in zip(s_parts, v_parts):
        p = jnp.exp(s - m)
        l = l + p.sum(axis=-1, keepdims=True)
        o = o + jnp.dot(p.astype(BF16), v, preferred_element_type=F32)
    return o / l


def _qk(q, k):
    return lax.dot_general(q, k, (((1,), (1,)), ((), ())), preferred_element_type=F32)


def _attn_kernel(*refs, has_ctx, n_sub):
    if has_ctx:
        q_ref, k_ref, v_ref, kc_ref, vc_ref, o_ref = refs
    else:
        q_ref, k_ref, v_ref, o_ref = refs
    sub = q_ref.shape[0] // n_sub

    def scores(r):
        q = q_ref[r * sub:(r + 1) * sub, :]
        parts = [_qk(q, k_ref[...])]
        if has_ctx:
            parts.append(_qk(q, kc_ref[...]))
        return parts

    s_next = scores(0)
    for r in range(n_sub):
        s_cur = s_next
        if r + 1 < n_sub:
            s_next = scores(r + 1)
        v_parts = [v_ref[...]] + ([vc_ref[...]] if has_ctx else [])
        o_ref[r * sub:(r + 1) * sub, :] = _softmax_pv(s_cur, v_parts).astype(o_ref.dtype)


def _attention(q, k, v, *, n_batch, n_heads, dq, dv, tq, n_q_tiles, q_blk0, t_k, k_blk0, t_ctx=None, n_sub=1, name):
    has_ctx = t_ctx is not None
    in_specs = [
        pl.BlockSpec((tq, dq), lambda b, h, i: (q_blk0 + b * n_q_tiles + i, h)),
        pl.BlockSpec((t_k, dq), lambda b, h, i: (k_blk0 + b, h)),
        pl.BlockSpec((t_k, dv), lambda b, h, i: (k_blk0 + b, h)),
    ]
    args = [q, k, v]
    if has_ctx:
        in_specs += [pl.BlockSpec((t_ctx, dq), lambda b, h, i: (b, h)),
                     pl.BlockSpec((t_ctx, dv), lambda b, h, i: (b, h))]
        args += [k, v]
    return pl.pallas_call(
        functools.partial(_attn_kernel, has_ctx=has_ctx, n_sub=n_sub),
        grid=(n_batch, n_heads, n_q_tiles), in_specs=in_specs,
        out_specs=pl.BlockSpec((tq, dv), lambda b, h, i: (b * n_q_tiles + i, h)),
        out_shape=jax.ShapeDtypeStruct((n_batch * n_q_tiles * tq, n_heads * dv), BF16),
        compiler_params=_params(3), name=name)(*args)


def _na_band_start(t, rows):
    return jnp.clip(t * NA_TILE_ROWS - NA_WIN_H // 2, 0, rows - NA_BAND_ROWS)


def _na_kernel(q_ref, k_ref, v_ref, kc_ref, vc_ref, bias_ref, o_ref, *, rows):
    t = pl.program_id(1)
    ks = pl.multiple_of(_na_band_start(t, rows) * GRID_W, GRID_W)
    band = pl.ds(ks, NA_BAND_ROWS * GRID_W)

    def scores(h):
        cols = slice(h * NA_HEAD_DIM, (h + 1) * NA_HEAD_DIM)
        q = q_ref[:, cols]
        return [_qk(q, k_ref[band, cols]) + bias_ref[h], _qk(q, kc_ref[:, cols])]

    s_next = scores(0)
    for h in range(NA_HEADS):
        s_cur = s_next
        if h + 1 < NA_HEADS:
            s_next = scores(h + 1)
        cols = slice(h * NA_HEAD_DIM, (h + 1) * NA_HEAD_DIM)
        o_ref[:, cols] = _softmax_pv(s_cur, [v_ref[band, cols], vc_ref[:, cols]]).astype(o_ref.dtype)


def _na_bias_tables(rel_bias, rows):
    n_tiles = rows // NA_TILE_ROWS
    n_r, n_c = 2 * NA_WIN_H - 1, 2 * NA_WIN_W - 1
    qr = np.arange(NA_TILE_ROWS)[:, None]
    ur = np.arange(NA_BAND_ROWS)[None, :]
    qc = np.arange(GRID_W)[:, None]
    kc = np.arange(GRID_W)[None, :]
    c0 = np.clip(qc - NA_WIN_W // 2, 0, GRID_W - NA_WIN_W)
    col_valid = (kc >= c0) & (kc < c0 + NA_WIN_W)
    col_onehot = np.eye(n_c, dtype=np.float32)[np.clip(kc - qc + NA_WIN_W - 1, 0, n_c - 1)]
    geoms, ids = [], []
    for t in range(n_tiles):
        u0 = int(np.clip(t * NA_TILE_ROWS - NA_WIN_H // 2, 0, rows - NA_BAND_ROWS))
        r = t * NA_TILE_ROWS + qr
        key_row = u0 + ur
        r0 = np.clip(r - NA_WIN_H // 2, 0, rows - NA_WIN_H)
        geom = ((key_row >= r0) & (key_row < r0 + NA_WIN_H), np.clip(key_row - r + NA_WIN_H - 1, 0, n_r - 1))
        for gi, g in enumerate(geoms):
            if all(np.array_equal(a, b) for a, b in zip(g, geom)):
                ids.append(gi)
                break
        else:
            ids.append(len(geoms))
            geoms.append(geom)
    row_valid = np.stack([g[0] for g in geoms])
    row_onehot = np.eye(n_r, dtype=np.float32)[np.stack([g[1] for g in geoms])]
    bias = jnp.einsum('gaur,hrc,bkc->ghabuk', row_onehot, rel_bias.astype(F32), col_onehot,
                      precision=lax.Precision.HIGHEST)
    valid = row_valid[:, None, :, None, :, None] & col_valid[None, None, None, :, None, :]
    tables = jnp.where(valid, bias, NEG)
    return tables.reshape(len(geoms), NA_HEADS, NA_TILE_ROWS * GRID_W, NA_BAND_ROWS * GRID_W), ids


def _na_attention(q, k, v, rel_bias, *, n_batch, seq, t_ctx):
    rows = seq // GRID_W
    n_tiles = rows // NA_TILE_ROWS
    tq = NA_TILE_ROWS * GRID_W
    tables, ids = _na_bias_tables(rel_bias, rows)
    assert ids == [0] + [1] * (n_tiles - 2) + [2], ids
    ctx_tiles = n_batch * t_ctx // tq
    ctx_units = n_batch * t_ctx // seq

    def table_id(t):
        return jnp.where(t == 0, 0, jnp.where(t == n_tiles - 1, 2, 1))

    return pl.pallas_call(
        functools.partial(_na_kernel, rows=rows),
        grid=(n_batch, n_tiles),
        in_specs=[
            pl.BlockSpec((tq, NA_WIDTH), lambda b, t: (ctx_tiles + b * n_tiles + t, 0)),
            pl.BlockSpec((seq, NA_WIDTH), lambda b, t: (ctx_units + b, 0)),
            pl.BlockSpec((seq, NA_WIDTH), lambda b, t: (ctx_units + b, 0)),
            pl.BlockSpec((t_ctx, NA_WIDTH), lambda b, t: (b, 0)),
            pl.BlockSpec((t_ctx, NA_WIDTH), lambda b, t: (b, 0)),
            pl.BlockSpec((None, NA_HEADS, tq, NA_BAND_ROWS * GRID_W), lambda b, t: (table_id(t), 0, 0, 0)),
        ],
        out_specs=pl.BlockSpec((tq, NA_WIDTH), lambda b, t: (b * n_tiles + t, 0)),
        out_shape=jax.ShapeDtypeStruct((n_batch * seq, NA_WIDTH), BF16),
        compiler_params=_params(2), name="na_attention")(q, k, v, k, v, tables)


def _sg_kernel(uv_ref, g_ref, b_ref, ws_ref, bs_ref, o_ref, *, chunks):
    for c in range(chunks):
        r = slice(c * SG_CHUNK, (c + 1) * SG_CHUNK)
        z = uv_ref[r, SG_WIDTH:].astype(F32)
        mu = jnp.mean(z, axis=-1, keepdims=True)
        zc = z - mu
        var = jnp.mean(zc * zc, axis=-1, keepdims=True)
        zn = (zc * lax.rsqrt(var + EPS) * g_ref[...] + b_ref[...]).astype(BF16)
        for g in range(SG_GROUPS):
            cols = slice(g * SG_GROUP_DIM, (g + 1) * SG_GROUP_DIM)
            mixed = jnp.dot(ws_ref[g], zn[:, cols], preferred_element_type=F32) + bs_ref[g]
            o_ref[r, cols] = (uv_ref[r, cols].astype(F32) * mixed).astype(o_ref.dtype)


def _spatial_gating(uv, ln_g, ln_b, w_s, b_s, tm=512):
    rows = uv.shape[0]
    bs = jnp.broadcast_to(b_s.astype(F32)[:, :, None], (SG_GROUPS, SG_CHUNK, SG_GROUP_DIM))
    return pl.pallas_call(
        functools.partial(_sg_kernel, chunks=tm // SG_CHUNK),
        grid=(rows // tm,),
        in_specs=[
            pl.BlockSpec((tm, 2 * SG_WIDTH), lambda i: (i, 0)),
            pl.BlockSpec((1, SG_WIDTH), lambda i: (0, 0)),
            pl.BlockSpec((1, SG_WIDTH), lambda i: (0, 0)),
            pl.BlockSpec((SG_GROUPS, SG_CHUNK, SG_CHUNK), lambda i: (0, 0, 0)),
            pl.BlockSpec((SG_GROUPS, SG_CHUNK, SG_GROUP_DIM), lambda i: (0, 0, 0)),
        ],
        out_specs=pl.BlockSpec((tm, SG_WIDTH), lambda i: (i, 0)),
        out_shape=jax.ShapeDtypeStruct((rows, SG_WIDTH), BF16),
        compiler_params=_params(1), name="spatial_gating")(
            uv, ln_g.reshape(1, SG_WIDTH), ln_b.reshape(1, SG_WIDTH), w_s.astype(BF16), bs)


def _swiglu(g, u):
    g = jnp.minimum(g, SWIGLU_LIMIT)
    u = jnp.clip(u, -SWIGLU_LIMIT, SWIGLU_LIMIT)
    return (u + 1.0) * (g * jax.nn.sigmoid(SWIGLU_ALPHA * g))


def _for_each_valid_part(i, tv_ref, o_ref, compute):
    part = o_ref.shape[0] // MOE_PARTS
    n_valid = (tv_ref[i] + part - 1) // part
    rows = [slice(p * part, (p + 1) * part) for p in range(MOE_PARTS)]
    for n in range(MOE_PARTS + 1):

        @pl.when(n_valid == n)
        def _(n=n):
            acc_next = compute.matmul(rows[0]) if n else None
            for p in range(n):
                acc = acc_next
                if p + 1 < n:
                    acc_next = compute.matmul(rows[p + 1])
                o_ref[rows[p], :] = compute.epilogue(acc).astype(o_ref.dtype)
            for p in range(n, MOE_PARTS):
                o_ref[rows[p], :] = jnp.zeros((part, o_ref.shape[1]), o_ref.dtype)


class _GateUp:
    def __init__(self, x_ref, wg_bf, wu_bf, bg_ref, bu_ref):
        self.x_ref, self.wg_bf, self.wu_bf, self.bg_ref, self.bu_ref = x_ref, wg_bf, wu_bf, bg_ref, bu_ref

    def matmul(self, rows):
        x = self.x_ref[rows, :]
        return (jnp.dot(x, self.wg_bf[...], preferred_element_type=F32),
                jnp.dot(x, self.wu_bf[...], preferred_element_type=F32))

    def epilogue(self, acc):
        return _swiglu(acc[0] + self.bg_ref[...], acc[1] + self.bu_ref[...])


class _Down:
    def __init__(self, h_ref, w_bf, b_ref):
        self.h_ref, self.w_bf, self.b_ref = h_ref, w_bf, b_ref

    def matmul(self, rows):
        return jnp.dot(self.h_ref[rows, :], self.w_bf[...], preferred_element_type=F32)

    def epilogue(self, acc):
        return acc + self.b_ref[...]


def _expert_weight_copies(w_hbm, lands, sem, layer, expert, cols):
    tn = lands[0].shape[1]
    return [pltpu.make_async_copy(w_hbm.at[layer, expert, :, pl.ds(pl.multiple_of(col, tn), tn)], land, sem.at[k])
            for k, (land, col) in enumerate(zip(lands, cols))]


def _refresh_expert_weights(te_ref, first_ref, nxt_ref, w_hbm, lands, bfs, sem, layer, col_fn, n_j):
    j = pl.program_id(0)
    i = pl.program_id(1)

    @pl.when(first_ref[i] == 1)
    def _():
        cur = _expert_weight_copies(w_hbm, lands, sem, layer, te_ref[i], col_fn(j))

        @pl.when((i == 0) & (j == 0))
        def _():
            for c in cur:
                c.start()

        for c in cur:
            c.wait()
        for land, bf in zip(lands, bfs):
            bf[...] = land[...].astype(BF16)
        nxt = nxt_ref[i]

        @pl.when(nxt >= 0)
        def _():
            for c in _expert_weight_copies(w_hbm, lands, sem, layer, nxt, col_fn(j)):
                c.start()

        @pl.when((nxt < 0) & (j + 1 < n_j))
        def _():
            for c in _expert_weight_copies(w_hbm, lands, sem, layer, te_ref[0], col_fn(j + 1)):
                c.start()


def _moe_gu_kernel(te_ref, tv_ref, ts_ref, first_ref, nxt_ref, x_ref, w_hbm, bg_ref, bu_ref, o_ref,
                   wg_land, wu_land, wg_bf, wu_bf, sem, *, layer, n_j):
    tn = wg_bf.shape[1]
    _refresh_expert_weights(te_ref, first_ref, nxt_ref, w_hbm, (wg_land, wu_land), (wg_bf, wu_bf), sem, layer,
                            lambda j: (j * tn, D_FF + j * tn), n_j)
    _for_each_valid_part(pl.program_id(1), tv_ref, o_ref, _GateUp(x_ref, wg_bf, wu_bf, bg_ref, bu_ref))


def _moe_down_kernel(te_ref, tv_ref, ts_ref, first_ref, nxt_ref, h_ref, w_hbm, b_ref, o_ref, w_land, w_bf, sem,
                     *, layer, n_j):
    tn = w_bf.shape[1]
    _refresh_expert_weights(te_ref, first_ref, nxt_ref, w_hbm, (w_land,), (w_bf,), sem, layer,
                            lambda j: (j * tn,), n_j)
    _for_each_valid_part(pl.program_id(1), tv_ref, o_ref, _Down(h_ref, w_bf, b_ref))


def _moe_experts(x_sorted, tile_e, tile_valid, tile_src, layer, w_gu, b_gu, w_down, b_down):
    r_pad, d = x_sorted.shape
    tm, tn, tn_down = MOE_TM, MOE_TN_GU, MOE_TN_DOWN
    n_tiles = r_pad // tm
    n_j = D_FF // tn
    n_jd = d // tn_down
    idx = jnp.arange(n_tiles, dtype=jnp.int32)
    first = jnp.concatenate([jnp.ones((1,), bool), tile_e[1:] != tile_e[:-1]])
    later_first = lax.cummin(jnp.where(first, idx, n_tiles)[::-1])[::-1]
    next_first = jnp.concatenate([later_first[1:], jnp.full((1,), n_tiles, jnp.int32)])
    nxt = jnp.where(next_first < n_tiles, tile_e[jnp.minimum(next_first, n_tiles - 1)], -1).astype(jnp.int32)
    prefetch = (tile_e, tile_valid, tile_src, first.astype(jnp.int32), nxt)
    x_map = lambda j, i, te, tv, ts, fr, nx: (ts[i], 0)
    out_map = lambda j, i, te, tv, ts, fr, nx: (i, j)

    hid = pl.pallas_call(
        functools.partial(_moe_gu_kernel, layer=layer, n_j=n_j),
        grid_spec=pltpu.PrefetchScalarGridSpec(
            num_scalar_prefetch=5, grid=(n_j, n_tiles),
            in_specs=[
                pl.BlockSpec((tm, d), x_map),
                pl.BlockSpec(memory_space=pl.ANY),
                pl.BlockSpec((None, None, 1, tn), lambda j, i, te, tv, ts, fr, nx: (layer, te[i], 0, j)),
                pl.BlockSpec((None, None, 1, tn), lambda j, i, te, tv, ts, fr, nx: (layer, te[i], 0, n_j + j)),
            ],
            out_specs=pl.BlockSpec((tm, tn), out_map),
            scratch_shapes=[pltpu.VMEM((d, tn), F32), pltpu.VMEM((d, tn), F32),
                            pltpu.VMEM((d, tn), BF16), pltpu.VMEM((d, tn), BF16),
                            pltpu.SemaphoreType.DMA((2,))]),
        out_shape=jax.ShapeDtypeStruct((r_pad, D_FF), BF16),
        compiler_params=_params(2), name="moe_gate_up")(
            *prefetch, x_sorted, w_gu, b_gu[:, :, None, :], b_gu[:, :, None, :])

    return pl.pallas_call(
        functools.partial(_moe_down_kernel, layer=layer, n_j=n_jd),
        grid_spec=pltpu.PrefetchScalarGridSpec(
            num_scalar_prefetch=5, grid=(n_jd, n_tiles),
            in_specs=[
                pl.BlockSpec((tm, D_FF), x_map),
                pl.BlockSpec(memory_space=pl.ANY),
                pl.BlockSpec((None, None, 1, tn_down), lambda j, i, te, tv, ts, fr, nx: (layer, te[i], 0, j)),
            ],
            out_specs=pl.BlockSpec((tm, tn_down), out_map),
            scratch_shapes=[pltpu.VMEM((D_FF, tn_down), F32), pltpu.VMEM((D_FF, tn_down), BF16),
                            pltpu.SemaphoreType.DMA((1,))]),
        out_shape=jax.ShapeDtypeStruct((r_pad, d), BF16),
        compiler_params=_params(2), name="moe_down")(
            *prefetch, hid, w_down, b_down[:, :, None, :])


def _moe_route(logits):
    n = logits.shape[0]
    nk = n * TOP_K
    tm = MOE_TM
    i32 = jnp.int32
    top_val, top_idx = lax.top_k(logits, TOP_K)
    gate = jax.nn.softmax(top_val, axis=-1)
    flat_e = top_idx.reshape(nk).astype(i32)
    iota = jnp.arange(nk, dtype=i32)
    _, order = lax.sort((flat_e, iota), num_keys=1)
    _, inv_order = lax.sort((order, iota), num_keys=1)
    onehot = flat_e[:, None] == jnp.arange(N_EXPERTS, dtype=i32)[None, :]
    counts = jnp.sum(onehot, axis=0, dtype=i32)
    padded = (counts + tm - 1) // tm * tm
    pad_end = jnp.cumsum(padded)
    pad_start = pad_end - padded
    start = jnp.cumsum(counts) - counts
    pos = inv_order + jnp.sum(jnp.where(onehot, (pad_start - start)[None, :], 0), axis=1)
    pos = pos.reshape(n, TOP_K).T.reshape(nk)

    n_tiles = nk // tm + N_EXPERTS
    n_used = pad_end[-1] // tm
    tile_src = jnp.minimum(jnp.arange(n_tiles, dtype=i32), n_used - 1)
    tile_e = jnp.minimum(jnp.sum(pad_end[None, :] <= (tile_src * tm)[:, None], axis=1, dtype=i32), N_EXPERTS - 1)
    in_use = jnp.arange(n_tiles, dtype=i32) < n_used
    rank0 = tile_src * tm - pad_start[tile_e]
    tile_valid = jnp.where(in_use, jnp.clip(counts[tile_e] - rank0, 0, tm), 0).astype(i32)
    within = jnp.arange(tm, dtype=i32)[None, :]
    valid = within < tile_valid[:, None]
    src = jnp.clip((start[tile_e] + rank0)[:, None] + within, 0, nk - 1)
    filler = (jnp.arange(n_tiles * tm, dtype=i32) % n).reshape(n_tiles, tm)
    rows_tok = jnp.where(valid, order.at[src].get(mode="promise_in_bounds") // TOP_K, filler).reshape(n_tiles * tm)
    return gate, rows_tok, pos, tile_e, tile_valid, tile_src.astype(i32)


def _moe_combine_kernel(x_ref, g2_ref, gate_ref, y_ref, o_ref):
    gate = gate_ref[...]
    f = gate[:, 0:1] * y_ref[0].astype(F32)
    for k in range(1, TOP_K):
        f = f + gate[:, k:k + 1] * y_ref[k].astype(F32)
    o_ref[...] = x_ref[...] + g2_ref[...] * f


def _moe_block(xs, h2, logits, mod, layer, n_ctx_rows, seq, w_gu, b_gu, w_down, b_down, tm=256):
    n, d = h2.shape
    gate, rows_tok, pos, tile_e, tile_valid, tile_src = _moe_route(logits)
    x_sorted = h2.at[rows_tok].get(mode="promise_in_bounds")
    y = _moe_experts(x_sorted, tile_e, tile_valid, tile_src, layer, w_gu, b_gu, w_down, b_down)
    y_k = y.at[pos].get(mode="promise_in_bounds").reshape(TOP_K, n, d)
    mrow = functools.partial(_mod_row, tm=tm, n_ctx_rows=n_ctx_rows, seq=seq)
    return pl.pallas_call(
        _moe_combine_kernel, grid=(n // tm,),
        in_specs=[
            pl.BlockSpec((tm, d), lambda i: (i, 0)),
            pl.BlockSpec((None, 1, d), lambda i: (mrow(i), 0, 5)),
            pl.BlockSpec((tm, TOP_K), lambda i: (i, 0)),
            pl.BlockSpec((TOP_K, tm, d), lambda i: (0, i, 0)),
        ],
        out_specs=pl.BlockSpec((tm, d), lambda i: (i, 0)),
        out_shape=jax.ShapeDtypeStruct((n, d), F32),
        compiler_params=_params(1), name="moe_combine")(xs, mod, gate, y_k)


def _axial_rope_tables(seq, tm):
    half = QK_ROPE // 2
    t = jnp.arange(seq)
    row = (t // GRID_W).astype(F32)
    col = (t % GRID_W).astype(F32)
    n_freq = QK_ROPE // 4
    inv = ROPE_THETA ** (-jnp.arange(n_freq, dtype=F32) / n_freq)
    ang = jnp.concatenate([row[:, None] * inv, col[:, None] * inv], axis=-1)
    cos, sin = jnp.cos(ang), jnp.sin(ang)
    z = jnp.zeros((seq, half), F32)
    zpad = jnp.zeros((seq, LANE - 2 * half), F32)
    c = jnp.concatenate([cos, cos, zpad], axis=-1)
    s_left = jnp.concatenate([-sin, z, zpad], axis=-1)
    s_right = jnp.concatenate([z, sin, zpad], axis=-1)
    tab = jnp.concatenate([c, s_left, s_right], axis=-1)
    ident = jnp.concatenate([jnp.ones((tm, 2 * half), F32), jnp.zeros((tm, 3 * LANE - 2 * half), F32)], axis=-1)
    return jnp.concatenate([ident, tab], axis=0)


def _ada_mod(cvec, ada_w, ada_b, layer):
    n = cvec.shape[0]
    m_pad = 16
    a = jnp.zeros((m_pad, D_MODEL), F32).at[:n].set(jax.nn.silu(cvec)).astype(BF16)
    n_cols = ada_w.shape[-1]
    tn = 1024
    (mod,) = _matmul(
        a, ada_w, w_lead=(layer,), n_cols=n_cols, tm=m_pad, tn=tn, epilogue=_ep_bias,
        extras=[(ada_b[:, None, :], (None, 1, tn), lambda j, i: (layer, 0, j))],
        outs=[(jax.ShapeDtypeStruct((m_pad, n_cols), F32), (m_pad, tn), lambda j, i: (i, j))],
        name="ada_mod")
    return mod[:n, None, :]


def kernel(x, c, ctx, c_ctx, ada_w, ada_b, norm1_g, norm2_g, ab_w_in, ab_w_out, na_q_g, na_k_g, na_rel_bias,
           sg_norm_g, sg_norm_b, sg_w, sg_b, mla_w_in, mla_q_norm_g, mla_kv_norm_g, mla_w_uq, mla_w_ukv,
           mla_q_g, mla_k_g, mla_w_out, moe_w_router, moe_b_router, moe_w_gu, moe_b_gu, moe_w_down, moe_b_down):
    n_batch, seq, d = x.shape
    t_ctx = ctx.shape[1]
    n_ctx = n_batch * t_ctx
    n_lat = n_batch * seq
    n_all = n_ctx + n_lat
    tm = MATMUL_TM
    tn = 512
    cvec = jnp.concatenate([c_ctx[None, :], c], axis=0)
    xs = jnp.concatenate([ctx.reshape(n_ctx, d), x.reshape(n_lat, d)], axis=0)
    mrow = functools.partial(_mod_row, tm=tm, n_ctx_rows=n_ctx, seq=seq)

    mod = _ada_mod(cvec, ada_w, ada_b, 0)
    h = _norm_mod(xs, norm1_g[0], mod, 0, 1, n_ctx, seq)
    w_in = ab_w_in
    q_gain = (na_q_g[0] * (NA_HEAD_DIM ** -0.5)).reshape(1, NA_HEAD_DIM)
    k_gain = na_k_g[0].reshape(1, NA_HEAD_DIM)

    def in_proj(col0, n_cols, epilogue, extras, name):
        (o,) = _matmul(h, w_in, w_lead=(0,), col0=col0, n_cols=n_cols, tm=tm, tn=tn, epilogue=epilogue,
                       extras=extras,
                       outs=[(jax.ShapeDtypeStruct((n_all, n_cols), BF16), (tm, tn), lambda j, i: (i, j))],
                       name=name)
        return o

    gain_spec = lambda g: [(g, (1, NA_HEAD_DIM), lambda j, i: (0, 0))]
    q = in_proj(0, NA_WIDTH, _ep_head_rms, gain_spec(q_gain), "ab_in_q")
    k = in_proj(NA_WIDTH, NA_WIDTH, _ep_head_rms, gain_spec(k_gain), "ab_in_k")
    v = in_proj(2 * NA_WIDTH, NA_WIDTH, _ep_cast, [], "ab_in_v")
    uv = in_proj(3 * NA_WIDTH, 2 * SG_WIDTH, _ep_gelu, [], "ab_in_uv")

    a_lat = _na_attention(q, k, v, na_rel_bias[0], n_batch=n_batch, seq=seq, t_ctx=t_ctx)
    a_ctx = _attention(q, k, v, n_batch=n_batch, n_heads=NA_HEADS, dq=NA_HEAD_DIM, dv=NA_HEAD_DIM,
                       tq=t_ctx, n_q_tiles=1, q_blk0=0, t_k=t_ctx, k_blk0=0, name="ctx_attention")
    gated = _spatial_gating(uv, sg_norm_g[0], sg_norm_b[0], sg_w[0], sg_b[0])
    mix = jnp.concatenate([jnp.concatenate([a_ctx, a_lat], axis=0), gated], axis=1)

    def out_proj(inp, w, resid, resid_row0, n_ctx_rows, name):
        mr = functools.partial(_mod_row, tm=tm, n_ctx_rows=n_ctx_rows, seq=seq)
        (o,) = _matmul(inp, w, w_lead=(0,), n_cols=d, tm=tm, tn=tn, epilogue=_ep_residual,
                       extras=[(resid, (tm, tn), lambda j, i: (resid_row0 // tm + i, j)),
                               (mod, (None, 1, tn), lambda j, i: (mr(i), 0, 2 * d // tn + j))],
                       outs=[(jax.ShapeDtypeStruct((inp.shape[0], d), F32), (tm, tn), lambda j, i: (i, j))],
                       name=name)
        return o

    xs = out_proj(mix, ab_w_out, xs, 0, n_ctx, "ab_out")
    h2, logits = _norm_mod(xs, norm2_g[0], mod, 3, 4, n_ctx, seq, router=(moe_w_router[0], moe_b_router[0]))
    xs = _moe_block(xs, h2, logits, mod, 0, n_ctx, seq, moe_w_gu, moe_b_gu, moe_w_down, moe_b_down)

    mod = _ada_mod(cvec, ada_w, ada_b, 1)
    h = _norm_mod(xs, norm1_g[1], mod, 0, 1, n_ctx, seq)
    w_in_pad = jnp.pad(mla_w_in, ((0, 0), (0, 0), (0, MLA_IN_PAD - mla_w_in.shape[-1])))
    c_q, c_kv, k_pe = _matmul(
        h, w_in_pad, w_lead=(0,), n_cols=MLA_IN_PAD, tm=tm, tn=MLA_IN_PAD, epilogue=_ep_mla_in,
        extras=[(mla_q_norm_g[0].reshape(1, Q_LORA), (1, Q_LORA), lambda j, i: (0, 0)),
                (mla_kv_norm_g[0].reshape(1, KV_LORA), (1, KV_LORA), lambda j, i: (0, 0))],
        outs=[(jax.ShapeDtypeStruct((n_all, Q_LORA), BF16), (tm, Q_LORA), lambda j, i: (i, 0)),
              (jax.ShapeDtypeStruct((n_all, KV_LORA), BF16), (tm, KV_LORA), lambda j, i: (i, 0)),
              (jax.ShapeDtypeStruct((n_all, LANE), F32), (tm, LANE), lambda j, i: (i, 0))],
        name="mla_in")

    rope_tab = _axial_rope_tables(seq, tm)
    n_ctx_tiles = n_ctx // tm
    seq_tiles = seq // tm
    pad_gain = lambda g, s: jnp.pad(g * s, (0, QK_PAD - QK_DIM)).reshape(1, QK_PAD)
    w_uq = jnp.pad(mla_w_uq[0].reshape(Q_LORA, MLA_HEADS, QK_DIM),
                   ((0, 0), (0, 0), (0, QK_PAD - QK_DIM))).reshape(Q_LORA, MLA_HEADS * QK_PAD)
    tn_up = 1024
    (q,) = _matmul(
        c_q, w_uq, row0=n_ctx, n_cols=MLA_HEADS * QK_PAD, tm=tm, tn=tn_up, epilogue=_ep_mla_q,
        extras=[(pad_gain(mla_q_g[0], QK_DIM ** -0.5), (1, QK_PAD), lambda j, i: (0, 0)),
                (rope_tab, (tm, 3 * LANE), lambda j, i: (1 + i % seq_tiles, 0))],
        outs=[(jax.ShapeDtypeStruct((n_lat, MLA_HEADS * QK_PAD), BF16), (tm, tn_up), lambda j, i: (i, j))],
        name="mla_up_q")
    heads_per_tile = tn_up // (QK_NOPE + V_DIM)
    k, v = _matmul(
        c_kv, mla_w_ukv, w_lead=(0,), n_cols=MLA_HEADS * (QK_NOPE + V_DIM), tm=tm, tn=tn_up, epilogue=_ep_mla_kv,
        extras=[(pad_gain(mla_k_g[0], 1.0), (1, QK_PAD), lambda j, i: (0, 0)),
                (rope_tab, (tm, 3 * LANE),
                 lambda j, i: (jnp.where(i < n_ctx_tiles, 0, 1 + (i - n_ctx_tiles) % seq_tiles), 0)),
                (k_pe, (tm, LANE), lambda j, i: (i, 0))],
        outs=[(jax.ShapeDtypeStruct((n_all, MLA_HEADS * QK_PAD), BF16), (tm, heads_per_tile * QK_PAD),
               lambda j, i: (i, j)),
              (jax.ShapeDtypeStruct((n_all, MLA_HEADS * V_DIM), BF16), (tm, heads_per_tile * V_DIM),
               lambda j, i: (i, j))],
        name="mla_up_kv")

    tq = 1024
    attn = _attention(q, k, v, n_batch=n_batch, n_heads=MLA_HEADS, dq=QK_PAD, dv=V_DIM, tq=tq,
                      n_q_tiles=seq // tq, q_blk0=0, t_k=seq, k_blk0=n_ctx // seq, t_ctx=t_ctx, n_sub=4,
                      name="mla_attention")
    x_lat = out_proj(attn, mla_w_out, xs, n_ctx, 0, "mla_out")
    h2, logits = _norm_mod(x_lat, norm2_g[1], mod, 3, 4, 0, seq, router=(moe_w_router[1], moe_b_router[1]))
    x_lat = _moe_block(x_lat, h2, logits, mod, 1, 0, seq, moe_w_gu, moe_b_gu, moe_w_down, moe_b_down)
    return x_lat.reshape(n_batch, seq, d)
```

```python
import functools

import numpy as np
import jax
import jax.numpy as jnp
from jax import lax
from jax.experimental import pallas as pl
from jax.experimental.pallas import tpu as pltpu

F32 = jnp.float32
BF16 = jnp.bfloat16

D_MODEL = 2048
GRID_W = 64
EPS = 1e-6
NEG = -1e30

NA_HEADS = 8
NA_HEAD_DIM = 128
NA_WIDTH = NA_HEADS * NA_HEAD_DIM
NA_WIN_H = 8
NA_WIN_W = 16
SG_GROUPS = 8
SG_WIDTH = D_MODEL // 2
SG_GROUP_DIM = SG_WIDTH // SG_GROUPS
SG_CHUNK = 128

MLA_HEADS = 16
Q_LORA = 512
KV_LORA = 256
QK_NOPE = 128
QK_ROPE = 64
V_DIM = 128
QK_DIM = QK_NOPE + QK_ROPE
QK_PAD = 256
MLA_IN_PAD = Q_LORA + KV_LORA + 128
ROPE_THETA = 10000.0

N_EXPERTS = 32
TOP_K = 4
D_FF = D_MODEL
SWIGLU_ALPHA = 1.702
SWIGLU_LIMIT = 7.0

LANE = 128
V7X_VMEM_LIMIT = 56 * 1024 * 1024

MATMUL_TM = 1024
MATMUL_SUB_ROWS = 256
NA_TILE_ROWS = 4
NA_BAND_ROWS = NA_TILE_ROWS + NA_WIN_H - 1
MOE_TM = 1024
MOE_PARTS = 4
MOE_TN_GU = 1024
MOE_TN_DOWN = 2048


def _params(n_axes):
    return pltpu.CompilerParams(dimension_semantics=("arbitrary",) * n_axes,
                                vmem_limit_bytes=V7X_VMEM_LIMIT)


def _norm_mod_body(x_ref, g_ref, sh_ref, sc_ref):
    x = x_ref[...]
    y = x * lax.rsqrt(jnp.mean(x * x, axis=-1, keepdims=True) + EPS) * g_ref[...]
    return y * (1.0 + sc_ref[...]) + sh_ref[...]


def _norm_mod_kernel(x_ref, g_ref, sh_ref, sc_ref, o_ref):
    o_ref[...] = _norm_mod_body(x_ref, g_ref, sh_ref, sc_ref).astype(o_ref.dtype)


def _norm_mod_router_kernel(x_ref, g_ref, sh_ref, sc_ref, wcat_ref, br_ref, o_ref, lg_ref):
    h = _norm_mod_body(x_ref, g_ref, sh_ref, sc_ref)
    h_hi = h.astype(BF16)
    o_ref[...] = h_hi
    n_e = lg_ref.shape[1]
    h_lo = (h - h_hi.astype(F32)).astype(BF16)
    a = jnp.dot(h_hi, wcat_ref[...], preferred_element_type=F32)
    b = jnp.dot(h_lo, wcat_ref[:, :n_e], preferred_element_type=F32)
    lg_ref[...] = a[:, :n_e] + a[:, n_e:] + b + br_ref[...]


def _mod_row(i, tm, n_ctx_rows, seq):
    n_ctx_tiles = n_ctx_rows // tm
    return jnp.where(i < n_ctx_tiles, 0, 1 + (i - n_ctx_tiles) // (seq // tm))


def _norm_mod(x, gain, mod, k_shift, k_scale, n_ctx_rows, seq, router=None, tm=256):
    rows, d = x.shape
    mrow = functools.partial(_mod_row, tm=tm, n_ctx_rows=n_ctx_rows, seq=seq)
    in_specs = [
        pl.BlockSpec((tm, d), lambda i: (i, 0)),
        pl.BlockSpec((1, d), lambda i: (0, 0)),
        pl.BlockSpec((None, 1, d), lambda i: (mrow(i), 0, k_shift)),
        pl.BlockSpec((None, 1, d), lambda i: (mrow(i), 0, k_scale)),
    ]
    args = [x, gain.reshape(1, d), mod, mod]
    if router is None:
        return pl.pallas_call(
            _norm_mod_kernel, grid=(rows // tm,), in_specs=in_specs,
            out_specs=pl.BlockSpec((tm, d), lambda i: (i, 0)),
            out_shape=jax.ShapeDtypeStruct((rows, d), BF16),
            compiler_params=_params(1), name="norm_mod")(*args)
    w_r, b_r = router
    n_e = w_r.shape[1]
    w_hi = w_r.astype(BF16)
    w_cat = jnp.concatenate([w_hi, (w_r - w_hi.astype(F32)).astype(BF16)], axis=1)
    in_specs += [pl.BlockSpec((d, 2 * n_e), lambda i: (0, 0)), pl.BlockSpec((1, n_e), lambda i: (0, 0))]
    return pl.pallas_call(
        _norm_mod_router_kernel, grid=(rows // tm,), in_specs=in_specs,
        out_specs=[pl.BlockSpec((tm, d), lambda i: (i, 0)), pl.BlockSpec((tm, n_e), lambda i: (i, 0))],
        out_shape=[jax.ShapeDtypeStruct((rows, d), BF16), jax.ShapeDtypeStruct((rows, n_e), F32)],
        compiler_params=_params(1), name="norm_mod_router")(*args, w_cat, b_r.reshape(1, n_e))


def _matmul_kernel(*refs, n_extra, n_out, epilogue, n_sub):
    x_ref, w_ref = refs[:2]
    extra = refs[2:2 + n_extra]
    outs = refs[2 + n_extra:2 + n_extra + n_out]
    wbf_ref = refs[-1]

    @pl.when(pl.program_id(1) == 0)
    def _():
        wbf_ref[...] = w_ref[...].astype(BF16)

    sub = x_ref.shape[0] // n_sub

    def dot(r):
        return jnp.dot(x_ref[r * sub:(r + 1) * sub, :], wbf_ref[...], preferred_element_type=F32)

    acc_next = dot(0)
    for r in range(n_sub):
        acc = acc_next
        if r + 1 < n_sub:
            acc_next = dot(r + 1)
        rows = slice(r * sub, (r + 1) * sub)
        epilogue(acc, [_RowView(e, rows) if e.shape[0] == x_ref.shape[0] else e for e in extra],
                 [_RowView(o, rows) for o in outs])


class _RowView:
    def __init__(self, ref, rows):
        self.ref, self.rows = ref, rows
        self.shape = (rows.stop - rows.start,) + tuple(ref.shape[1:])
        self.dtype = ref.dtype

    def _key(self, key):
        if key is Ellipsis:
            return (self.rows, slice(None))
        assert isinstance(key, tuple) and key[0] == slice(None), key
        return (self.rows,) + tuple(key[1:])

    def __getitem__(self, key):
        return self.ref[self._key(key)]

    def __setitem__(self, key, value):
        self.ref[self._key(key)] = value


def _matmul(x, w, *, w_lead=(), col0=0, n_cols, tm, tn, epilogue, extras=(), outs, name, row0=0, m=None):
    k = x.shape[1]
    m = x.shape[0] - row0 if m is None else m
    assert m % tm == 0 and row0 % tm == 0 and n_cols % tn == 0 and col0 % tn == 0
    lead = tuple(w_lead)
    w_spec = pl.BlockSpec((None,) * len(lead) + (k, tn), lambda j, i: lead + (0, col0 // tn + j))
    in_specs = [pl.BlockSpec((tm, k), lambda j, i: (row0 // tm + i, 0)), w_spec]
    in_specs += [pl.BlockSpec(bs, im) for _, bs, im in extras]
    out_specs = [pl.BlockSpec(bs, im) for _, bs, im in outs]
    kern = functools.partial(_matmul_kernel, n_extra=len(extras), n_out=len(outs), epilogue=epilogue,
                             n_sub=max(1, tm // MATMUL_SUB_ROWS))
    res = pl.pallas_call(
        kern, grid=(n_cols // tn, m // tm), in_specs=in_specs, out_specs=out_specs,
        out_shape=[s for s, _, _ in outs],
        scratch_shapes=[pltpu.VMEM((k, tn), BF16)],
        compiler_params=_params(2), name=name)(x, w, *[a for a, _, _ in extras])
    return res


def _ep_bias(acc, extra, outs):
    outs[0][...] = acc + extra[0][...]


def _ep_cast(acc, extra, outs):
    outs[0][...] = acc.astype(outs[0].dtype)


def _ep_gelu(acc, extra, outs):
    outs[0][...] = jax.nn.gelu(acc).astype(outs[0].dtype)


def _ep_head_rms(acc, extra, outs):
    g = extra[0][...]
    for h in range(acc.shape[1] // LANE):
        a = acc[:, h * LANE:(h + 1) * LANE]
        r = lax.rsqrt(jnp.mean(a * a, axis=-1, keepdims=True) + EPS)
        outs[0][:, h * LANE:(h + 1) * LANE] = (a * r * g).astype(outs[0].dtype)


def _ep_residual(acc, extra, outs):
    outs[0][...] = extra[0][...] + extra[1][...] * acc


def _rms(a, g):
    return a * lax.rsqrt(jnp.mean(a * a, axis=-1, keepdims=True) + EPS) * g


def _ep_mla_in(acc, extra, outs):
    outs[0][...] = _rms(acc[:, :Q_LORA], extra[0][...]).astype(BF16)
    outs[1][...] = _rms(acc[:, Q_LORA:Q_LORA + KV_LORA], extra[1][...]).astype(BF16)
    outs[2][...] = acc[:, Q_LORA + KV_LORA:]


def _rope_tail(t, tab_ref):
    return t * tab_ref[:, :LANE] + pltpu.roll(t, LANE // 2, axis=1) * tab_ref[:, LANE:]


def _ep_mla_q(acc, extra, outs):
    g_ref, tab_ref = extra
    g0 = g_ref[:, :LANE]
    g1 = g_ref[:, LANE:]
    for h in range(acc.shape[1] // QK_PAD):
        a0 = acc[:, h * QK_PAD:h * QK_PAD + LANE]
        a1 = acc[:, h * QK_PAD + LANE:(h + 1) * QK_PAD]
        r = lax.rsqrt(jnp.sum(a0 * a0 + a1 * a1, axis=-1, keepdims=True) * (1.0 / QK_DIM) + EPS)
        outs[0][:, h * QK_PAD:h * QK_PAD + LANE] = (a0 * r * g0).astype(BF16)
        outs[0][:, h * QK_PAD + LANE:(h + 1) * QK_PAD] = _rope_tail(a1 * r * g1, tab_ref).astype(BF16)


def _ep_mla_kv(acc, extra, outs):
    g_ref, tab_ref, pe_ref = extra
    k_out, v_out = outs
    g0 = g_ref[:, :LANE]
    pe = pe_ref[...]
    pe_sq = pe * pe
    pe_roped = _rope_tail(pe * g_ref[:, LANE:], tab_ref)
    for h in range(acc.shape[1] // (QK_NOPE + V_DIM)):
        base = h * (QK_NOPE + V_DIM)
        kn = acc[:, base:base + QK_NOPE]
        r = lax.rsqrt(jnp.sum(kn * kn + pe_sq, axis=-1, keepdims=True) * (1.0 / QK_DIM) + EPS)
        k_out[:, h * QK_PAD:h * QK_PAD + LANE] = (kn * r * g0).astype(BF16)
        k_out[:, h * QK_PAD + LANE:(h + 1) * QK_PAD] = (pe_roped * r).astype(BF16)
        v_out[:, h * V_DIM:(h + 1) * V_DIM] = acc[:, base + QK_NOPE:base + QK_NOPE + V_DIM].astype(BF16)


def _softmax_pv(s_parts, v_parts):
    m = s_parts[0].max(axis=-1, keepdims=True)
    for s in s_parts[1:]:
        m = jnp.maximum(m, s.max(axis=-1, keepdims=True))
    l = 0.0
    o = 0.0
    for s, v in zip(s_parts, v_parts):
        p = jnp.exp(s - m)
        l = l + p.sum(axis=-1, keepdims=True)
        o = o + jnp.dot(p.astype(BF16), v, preferred_element_type=F32)
    return o / l


def _qk(q, k):
    return lax.dot_general(q, k, (((1,), (1,)), ((), ())), preferred_element_type=F32)


def _attn_kernel(*refs, has_ctx, n_sub):
    if has_ctx:
        q_ref, k_ref, v_ref, kc_ref, vc_ref, o_ref = refs
    else:
        q_ref, k_ref, v_ref, o_ref = refs
    sub = q_ref.shape[0] // n_sub

    def scores(r):
        q = q_ref[r * sub:(r + 1) * sub, :]
        parts = [_qk(q, k_ref[...])]
        if has_ctx:
            parts.append(_qk(q, kc_ref[...]))
        return parts

    s_next = scores(0)
    for r in range(n_sub):
        s_cur = s_next
        if r + 1 < n_sub:
            s_next = scores(r + 1)
        v_parts = [v_ref[...]] + ([vc_ref[...]] if has_ctx else [])
        o_ref[r * sub:(r + 1) * sub, :] = _softmax_pv(s_cur, v_parts).astype(o_ref.dtype)


def _attention(q, k, v, *, n_batch, n_heads, dq, dv, tq, n_q_tiles, q_blk0, t_k, k_blk0, t_ctx=None, n_sub=1, name):
    has_ctx = t_ctx is not None
    in_specs = [
        pl.BlockSpec((tq, dq), lambda b, h, i: (q_blk0 + b * n_q_tiles + i, h)),
        pl.BlockSpec((t_k, dq), lambda b, h, i: (k_blk0 + b, h)),
        pl.BlockSpec((t_k, dv), lambda b, h, i: (k_blk0 + b, h)),
    ]
    args = [q, k, v]
    if has_ctx:
        in_specs += [pl.BlockSpec((t_ctx, dq), lambda b, h, i: (b, h)),
                     pl.BlockSpec((t_ctx, dv), lambda b, h, i: (b, h))]
        args += [k, v]
    return pl.pallas_call(
        functools.partial(_attn_kernel, has_ctx=has_ctx, n_sub=n_sub),
        grid=(n_batch, n_heads, n_q_tiles), in_specs=in_specs,
        out_specs=pl.BlockSpec((tq, dv), lambda b, h, i: (b * n_q_tiles + i, h)),
        out_shape=jax.ShapeDtypeStruct((n_batch * n_q_tiles * tq, n_heads * dv), BF16),
        compiler_params=_params(3), name=name)(*args)


def _na_band_start(t, rows):
    return jnp.clip(t * NA_TILE_ROWS - NA_WIN_H // 2, 0, rows - NA_BAND_ROWS)


def _na_kernel(q_ref, k_ref, v_ref, kc_ref, vc_ref, bias_ref, o_ref, *, rows):
    t = pl.program_id(1)
    ks = pl.multiple_of(_na_band_start(t, rows) * GRID_W, GRID_W)
    band = pl.ds(ks, NA_BAND_ROWS * GRID_W)

    def scores(h):
        cols = slice(h * NA_HEAD_DIM, (h + 1) * NA_HEAD_DIM)
        q = q_ref[:, cols]
        return [_qk(q, k_ref[band, cols]) + bias_ref[h], _qk(q, kc_ref[:, cols])]

    s_next = scores(0)
    for h in range(NA_HEADS):
        s_cur = s_next
        if h + 1 < NA_HEADS:
            s_next = scores(h + 1)
        cols = slice(h * NA_HEAD_DIM, (h + 1) * NA_HEAD_DIM)
        o_ref[:, cols] = _softmax_pv(s_cur, [v_ref[band, cols], vc_ref[:, cols]]).astype(o_ref.dtype)


def _na_bias_tables(rel_bias, rows):
    n_tiles = rows // NA_TILE_ROWS
    n_r, n_c = 2 * NA_WIN_H - 1, 2 * NA_WIN_W - 1
    qr = np.arange(NA_TILE_ROWS)[:, None]
    ur = np.arange(NA_BAND_ROWS)[None, :]
    qc = np.arange(GRID_W)[:, None]
    kc = np.arange(GRID_W)[None, :]
    c0 = np.clip(qc - NA_WIN_W // 2, 0, GRID_W - NA_WIN_W)
    col_valid = (kc >= c0) & (kc < c0 + NA_WIN_W)
    col_onehot = np.eye(n_c, dtype=np.float32)[np.clip(kc - qc + NA_WIN_W - 1, 0, n_c - 1)]
    geoms, ids = [], []
    for t in range(n_tiles):
        u0 = int(np.clip(t * NA_TILE_ROWS - NA_WIN_H // 2, 0, rows - NA_BAND_ROWS))
        r = t * NA_TILE_ROWS + qr
        key_row = u0 + ur
        r0 = np.clip(r - NA_WIN_H // 2, 0, rows - NA_WIN_H)
        geom = ((key_row >= r0) & (key_row < r0 + NA_WIN_H), np.clip(key_row - r + NA_WIN_H - 1, 0, n_r - 1))
        for gi, g in enumerate(geoms):
            if all(np.array_equal(a, b) for a, b in zip(g, geom)):
                ids.append(gi)
                break
        else:
            ids.append(len(geoms))
            geoms.append(geom)
    row_valid = np.stack([g[0] for g in geoms])
    row_onehot = np.eye(n_r, dtype=np.float32)[np.stack([g[1] for g in geoms])]
    bias = jnp.einsum('gaur,hrc,bkc->ghabuk', row_onehot, rel_bias.astype(F32), col_onehot,
                      precision=lax.Precision.HIGHEST)
    valid = row_valid[:, None, :, None, :, None] & col_valid[None, None, None, :, None, :]
    tables = jnp.where(valid, bias, NEG)
    return tables.reshape(len(geoms), NA_HEADS, NA_TILE_ROWS * GRID_W, NA_BAND_ROWS * GRID_W), ids


def _na_attention(q, k, v, rel_bias, *, n_batch, seq, t_ctx):
    rows = seq // GRID_W
    n_tiles = rows // NA_TILE_ROWS
    tq = NA_TILE_ROWS * GRID_W
    tables, ids = _na_bias_tables(rel_bias, rows)
    assert ids == [0] + [1] * (n_tiles - 2) + [2], ids
    ctx_tiles = n_batch * t_ctx // tq
    ctx_units = n_batch * t_ctx // seq

    def table_id(t):
        return jnp.where(t == 0, 0, jnp.where(t == n_tiles - 1, 2, 1))

    return pl.pallas_call(
        functools.partial(_na_kernel, rows=rows),
        grid=(n_batch, n_tiles),
        in_specs=[
            pl.BlockSpec((tq, NA_WIDTH), lambda b, t: (ctx_tiles + b * n_tiles + t, 0)),
            pl.BlockSpec((seq, NA_WIDTH), lambda b, t: (ctx_units + b, 0)),
            pl.BlockSpec((seq, NA_WIDTH), lambda b, t: (ctx_units + b, 0)),
            pl.BlockSpec((t_ctx, NA_WIDTH), lambda b, t: (b, 0)),
            pl.BlockSpec((t_ctx, NA_WIDTH), lambda b, t: (b, 0)),
            pl.BlockSpec((None, NA_HEADS, tq, NA_BAND_ROWS * GRID_W), lambda b, t: (table_id(t), 0, 0, 0)),
        ],
        out_specs=pl.BlockSpec((tq, NA_WIDTH), lambda b, t: (b * n_tiles + t, 0)),
        out_shape=jax.ShapeDtypeStruct((n_batch * seq, NA_WIDTH), BF16),
        compiler_params=_params(2), name="na_attention")(q, k, v, k, v, tables)


def _sg_kernel(uv_ref, g_ref, b_ref, ws_ref, bs_ref, o_ref, *, chunks):
    for c in range(chunks):
        r = slice(c * SG_CHUNK, (c + 1) * SG_CHUNK)
        z = uv_ref[r, SG_WIDTH:].astype(F32)
        mu = jnp.mean(z, axis=-1, keepdims=True)
        zc = z - mu
        var = jnp.mean(zc * zc, axis=-1, keepdims=True)
        zn = (zc * lax.rsqrt(var + EPS) * g_ref[...] + b_ref[...]).astype(BF16)
        for g in range(SG_GROUPS):
            cols = slice(g * SG_GROUP_DIM, (g + 1) * SG_GROUP_DIM)
            mixed = jnp.dot(ws_ref[g], zn[:, cols], preferred_element_type=F32) + bs_ref[g]
            o_ref[r, cols] = (uv_ref[r, cols].astype(F32) * mixed).astype(o_ref.dtype)


def _spatial_gating(uv, ln_g, ln_b, w_s, b_s, tm=512):
    rows = uv.shape[0]
    bs = jnp.broadcast_to(b_s.astype(F32)[:, :, None], (SG_GROUPS, SG_CHUNK, SG_GROUP_DIM))
    return pl.pallas_call(
        functools.partial(_sg_kernel, chunks=tm // SG_CHUNK),
        grid=(rows // tm,),
        in_specs=[
            pl.BlockSpec((tm, 2 * SG_WIDTH), lambda i: (i, 0)),
            pl.BlockSpec((1, SG_WIDTH), lambda i: (0, 0)),
            pl.BlockSpec((1, SG_WIDTH), lambda i: (0, 0)),
            pl.BlockSpec((SG_GROUPS, SG_CHUNK, SG_CHUNK), lambda i: (0, 0, 0)),
            pl.BlockSpec((SG_GROUPS, SG_CHUNK, SG_GROUP_DIM), lambda i: (0, 0, 0)),
        ],
        out_specs=pl.BlockSpec((tm, SG_WIDTH), lambda i: (i, 0)),
        out_shape=jax.ShapeDtypeStruct((rows, SG_WIDTH), BF16),
        compiler_params=_params(1), name="spatial_gating")(
            uv, ln_g.reshape(1, SG_WIDTH), ln_b.reshape(1, SG_WIDTH), w_s.astype(BF16), bs)


def _swiglu(g, u):
    g = jnp.minimum(g, SWIGLU_LIMIT)
    u = jnp.clip(u, -SWIGLU_LIMIT, SWIGLU_LIMIT)
    return (u + 1.0) * (g * jax.nn.sigmoid(SWIGLU_ALPHA * g))


def _for_each_valid_part(i, tv_ref, o_ref, compute):
    part = o_ref.shape[0] // MOE_PARTS
    n_valid = (tv_ref[i] + part - 1) // part
    rows = [slice(p * part, (p + 1) * part) for p in range(MOE_PARTS)]
    for n in range(MOE_PARTS + 1):

        @pl.when(n_valid == n)
        def _(n=n):
            acc_next = compute.matmul(rows[0]) if n else None
            for p in range(n):
                acc = acc_next
                if p + 1 < n:
                    acc_next = compute.matmul(rows[p + 1])
                o_ref[rows[p], :] = compute.epilogue(acc).astype(o_ref.dtype)
            for p in range(n, MOE_PARTS):
                o_ref[rows[p], :] = jnp.zeros((part, o_ref.shape[1]), o_ref.dtype)


class _GateUp:
    def __init__(self, x_ref, wg_bf, wu_bf, bg_ref, bu_ref):
        self.x_ref, self.wg_bf, self.wu_bf, self.bg_ref, self.bu_ref = x_ref, wg_bf, wu_bf, bg_ref, bu_ref

    def matmul(self, rows):
        x = self.x_ref[rows, :]
        return (jnp.dot(x, self.wg_bf[...], preferred_element_type=F32),
                jnp.dot(x, self.wu_bf[...], preferred_element_type=F32))

    def epilogue(self, acc):
        return _swiglu(acc[0] + self.bg_ref[...], acc[1] + self.bu_ref[...])


class _Down:
    def __init__(self, h_ref, w_bf, b_ref):
        self.h_ref, self.w_bf, self.b_ref = h_ref, w_bf, b_ref

    def matmul(self, rows):
        return jnp.dot(self.h_ref[rows, :], self.w_bf[...], preferred_element_type=F32)

    def epilogue(self, acc):
        return acc + self.b_ref[...]


def _expert_weight_copies(w_hbm, lands, sem, layer, expert, cols):
    tn = lands[0].shape[1]
    return [pltpu.make_async_copy(w_hbm.at[layer, expert, :, pl.ds(pl.multiple_of(col, tn), tn)], land, sem.at[k])
            for k, (land, col) in enumerate(zip(lands, cols))]


def _refresh_expert_weights(te_ref, first_ref, nxt_ref, w_hbm, lands, bfs, sem, layer, col_fn, n_j):
    j = pl.program_id(0)
    i = pl.program_id(1)

    @pl.when(first_ref[i] == 1)
    def _():
        cur = _expert_weight_copies(w_hbm, lands, sem, layer, te_ref[i], col_fn(j))

        @pl.when((i == 0) & (j == 0))
        def _():
            for c in cur:
                c.start()

        for c in cur:
            c.wait()
        for land, bf in zip(lands, bfs):
            bf[...] = land[...].astype(BF16)
        nxt = nxt_ref[i]

        @pl.when(nxt >= 0)
        def _():
            for c in _expert_weight_copies(w_hbm, lands, sem, layer, nxt, col_fn(j)):
                c.start()

        @pl.when((nxt < 0) & (j + 1 < n_j))
        def _():
            for c in _expert_weight_copies(w_hbm, lands, sem, layer, te_ref[0], col_fn(j + 1)):
                c.start()


def _moe_gu_kernel(te_ref, tv_ref, ts_ref, first_ref, nxt_ref, x_ref, w_hbm, bg_ref, bu_ref, o_ref,
                   wg_land, wu_land, wg_bf, wu_bf, sem, *, layer, n_j):
    tn = wg_bf.shape[1]
    _refresh_expert_weights(te_ref, first_ref, nxt_ref, w_hbm, (wg_land, wu_land), (wg_bf, wu_bf), sem, layer,
                            lambda j: (j * tn, D_FF + j * tn), n_j)
    _for_each_valid_part(pl.program_id(1), tv_ref, o_ref, _GateUp(x_ref, wg_bf, wu_bf, bg_ref, bu_ref))


def _moe_down_kernel(te_ref, tv_ref, ts_ref, first_ref, nxt_ref, h_ref, w_hbm, b_ref, o_ref, w_land, w_bf, sem,
                     *, layer, n_j):
    tn = w_bf.shape[1]
    _refresh_expert_weights(te_ref, first_ref, nxt_ref, w_hbm, (w_land,), (w_bf,), sem, layer,
                            lambda j: (j * tn,), n_j)
    _for_each_valid_part(pl.program_id(1), tv_ref, o_ref, _Down(h_ref, w_bf, b_ref))


def _moe_experts(x_sorted, tile_e, tile_valid, tile_src, layer, w_gu, b_gu, w_down, b_down):
    r_pad, d = x_sorted.shape
    tm, tn, tn_down = MOE_TM, MOE_TN_GU, MOE_TN_DOWN
    n_tiles = r_pad // tm
    n_j = D_FF // tn
    n_jd = d // tn_down
    idx = jnp.arange(n_tiles, dtype=jnp.int32)
    first = jnp.concatenate([jnp.ones((1,), bool), tile_e[1:] != tile_e[:-1]])
    later_first = lax.cummin(jnp.where(first, idx, n_tiles)[::-1])[::-1]
    next_first = jnp.concatenate([later_first[1:], jnp.full((1,), n_tiles, jnp.int32)])
    nxt = jnp.where(next_first < n_tiles, tile_e[jnp.minimum(next_first, n_tiles - 1)], -1).astype(jnp.int32)
    prefetch = (tile_e, tile_valid, tile_src, first.astype(jnp.int32), nxt)
    x_map = lambda j, i, te, tv, ts, fr, nx: (ts[i], 0)
    out_map = lambda j, i, te, tv, ts, fr, nx: (i, j)

    hid = pl.pallas_call(
        functools.partial(_moe_gu_kernel, layer=layer, n_j=n_j),
        grid_spec=pltpu.PrefetchScalarGridSpec(
            num_scalar_prefetch=5, grid=(n_j, n_tiles),
            in_specs=[
                pl.BlockSpec((tm, d), x_map),
                pl.BlockSpec(memory_space=pl.ANY),
                pl.BlockSpec((None, None, 1, tn), lambda j, i, te, tv, ts, fr, nx: (layer, te[i], 0, j)),
                pl.BlockSpec((None, None, 1, tn), lambda j, i, te, tv, ts, fr, nx: (layer, te[i], 0, n_j + j)),
            ],
            out_specs=pl.BlockSpec((tm, tn), out_map),
            scratch_shapes=[pltpu.VMEM((d, tn), F32), pltpu.VMEM((d, tn), F32),
                            pltpu.VMEM((d, tn), BF16), pltpu.VMEM((d, tn), BF16),
                            pltpu.SemaphoreType.DMA((2,))]),
        out_shape=jax.ShapeDtypeStruct((r_pad, D_FF), BF16),
        compiler_params=_params(2), name="moe_gate_up")(
            *prefetch, x_sorted, w_gu, b_gu[:, :, None, :], b_gu[:, :, None, :])

    return pl.pallas_call(
        functools.partial(_moe_down_kernel, layer=layer, n_j=n_jd),
        grid_spec=pltpu.PrefetchScalarGridSpec(
            num_scalar_prefetch=5, grid=(n_jd, n_tiles),
            in_specs=[
                pl.BlockSpec((tm, D_FF), x_map),
                pl.BlockSpec(memory_space=pl.ANY),
                pl.BlockSpec((None, None, 1, tn_down), lambda j, i, te, tv, ts, fr, nx: (layer, te[i], 0, j)),
            ],
            out_specs=pl.BlockSpec((tm, tn_down), out_map),
            scratch_shapes=[pltpu.VMEM((D_FF, tn_down), F32), pltpu.VMEM((D_FF, tn_down), BF16),
                            pltpu.SemaphoreType.DMA((1,))]),
        out_shape=jax.ShapeDtypeStruct((r_pad, d), BF16),
        compiler_params=_params(2), name="moe_down")(
            *prefetch, hid, w_down, b_down[:, :, None, :])


def _moe_route(logits):
    n = logits.shape[0]
    nk = n * TOP_K
    tm = MOE_TM
    i32 = jnp.int32
    top_val, top_idx = lax.top_k(logits, TOP_K)
    gate = jax.nn.softmax(top_val, axis=-1)
    flat_e = top_idx.reshape(nk).astype(i32)
    iota = jnp.arange(nk, dtype=i32)
    _, order = lax.sort((flat_e, iota), num_keys=1)
    _, inv_order = lax.sort((order, iota), num_keys=1)
    onehot = flat_e[:, None] == jnp.arange(N_EXPERTS, dtype=i32)[None, :]
    counts = jnp.sum(onehot, axis=0, dtype=i32)
    padded = (counts + tm - 1) // tm * tm
    pad_end = jnp.cumsum(padded)
    pad_start = pad_end - padded
    start = jnp.cumsum(counts) - counts
    pos = inv_order + jnp.sum(jnp.where(onehot, (pad_start - start)[None, :], 0), axis=1)
    pos = pos.reshape(n, TOP_K).T.reshape(nk)

    n_tiles = nk // tm + N_EXPERTS
    n_used = pad_end[-1] // tm
    tile_src = jnp.minimum(jnp.arange(n_tiles, dtype=i32), n_used - 1)
    tile_e = jnp.minimum(jnp.sum(pad_end[None, :] <= (tile_src * tm)[:, None], axis=1, dtype=i32), N_EXPERTS - 1)
    in_use = jnp.arange(n_tiles, dtype=i32) < n_used
    rank0 = tile_src * tm - pad_start[tile_e]
    tile_valid = jnp.where(in_use, jnp.clip(counts[tile_e] - rank0, 0, tm), 0).astype(i32)
    within = jnp.arange(tm, dtype=i32)[None, :]
    valid = within < tile_valid[:, None]
    src = jnp.clip((start[tile_e] + rank0)[:, None] + within, 0, nk - 1)
    filler = (jnp.arange(n_tiles * tm, dtype=i32) % n).reshape(n_tiles, tm)
    rows_tok = jnp.where(valid, order.at[src].get(mode="promise_in_bounds") // TOP_K, filler).reshape(n_tiles * tm)
    return gate, rows_tok, pos, tile_e, tile_valid, tile_src.astype(i32)


def _moe_combine_kernel(x_ref, g2_ref, gate_ref, y_ref, o_ref):
    gate = gate_ref[...]
    f = gate[:, 0:1] * y_ref[0].astype(F32)
    for k in range(1, TOP_K):
        f = f + gate[:, k:k + 1] * y_ref[k].astype(F32)
    o_ref[...] = x_ref[...] + g2_ref[...] * f


def _moe_block(xs, h2, logits, mod, layer, n_ctx_rows, seq, w_gu, b_gu, w_down, b_down, tm=256):
    n, d = h2.shape
    gate, rows_tok, pos, tile_e, tile_valid, tile_src = _moe_route(logits)
    x_sorted = h2.at[rows_tok].get(mode="promise_in_bounds")
    y = _moe_experts(x_sorted, tile_e, tile_valid, tile_src, layer, w_gu, b_gu, w_down, b_down)
    y_k = y.at[pos].get(mode="promise_in_bounds").reshape(TOP_K, n, d)
    mrow = functools.partial(_mod_row, tm=tm, n_ctx_rows=n_ctx_rows, seq=seq)
    return pl.pallas_call(
        _moe_combine_kernel, grid=(n // tm,),
        in_specs=[
            pl.BlockSpec((tm, d), lambda i: (i, 0)),
            pl.BlockSpec((None, 1, d), lambda i: (mrow(i), 0, 5)),
            pl.BlockSpec((tm, TOP_K), lambda i: (i, 0)),
            pl.BlockSpec((TOP_K, tm, d), lambda i: (0, i, 0)),
        ],
        out_specs=pl.BlockSpec((tm, d), lambda i: (i, 0)),
        out_shape=jax.ShapeDtypeStruct((n, d), F32),
        compiler_params=_params(1), name="moe_combine")(xs, mod, gate, y_k)


def _rope_tail_layout(x):
    half = QK_ROPE // 2
    z = jnp.zeros(x.shape[:-1] + (LANE // 2 - half,), x.dtype)
    return jnp.concatenate([x[..., :half], z, x[..., half:], z], axis=-1)


def _axial_rope_tables(seq, tm):
    t = jnp.arange(seq)
    row = (t // GRID_W).astype(F32)
    col = (t % GRID_W).astype(F32)
    n_freq = QK_ROPE // 4
    inv = ROPE_THETA ** (-jnp.arange(n_freq, dtype=F32) / n_freq)
    ang = jnp.concatenate([row[:, None] * inv, col[:, None] * inv], axis=-1)
    cos, sin = jnp.cos(ang), jnp.sin(ang)
    tab = jnp.concatenate([_rope_tail_layout(jnp.concatenate([cos, cos], axis=-1)),
                           _rope_tail_layout(jnp.concatenate([-sin, sin], axis=-1))], axis=-1)
    ident = jnp.concatenate([_rope_tail_layout(jnp.ones((tm, QK_ROPE), F32)), jnp.zeros((tm, LANE), F32)], axis=-1)
    return jnp.concatenate([ident, tab], axis=0)


def _ada_mod(cvec, ada_w, ada_b, layer):
    n = cvec.shape[0]
    m_pad = 16
    a = jnp.zeros((m_pad, D_MODEL), F32).at[:n].set(jax.nn.silu(cvec)).astype(BF16)
    n_cols = ada_w.shape[-1]
    tn = 1024
    (mod,) = _matmul(
        a, ada_w, w_lead=(layer,), n_cols=n_cols, tm=m_pad, tn=tn, epilogue=_ep_bias,
        extras=[(ada_b[:, None, :], (None, 1, tn), lambda j, i: (layer, 0, j))],
        outs=[(jax.ShapeDtypeStruct((m_pad, n_cols), F32), (m_pad, tn), lambda j, i: (i, j))],
        name="ada_mod")
    return mod[:n, None, :]


def _mla_project(h, n_ctx, seq, tm, mla_w_in, mla_q_norm_g, mla_kv_norm_g, mla_w_uq, mla_w_ukv, mla_q_g, mla_k_g):
    n_all = h.shape[0]
    n_lat = n_all - n_ctx
    w_in_pad = jnp.concatenate([mla_w_in[..., :Q_LORA + KV_LORA], _rope_tail_layout(mla_w_in[..., Q_LORA + KV_LORA:])],
                               axis=-1)
    c_q, c_kv, k_pe = _matmul(
        h, w_in_pad, w_lead=(0,), n_cols=MLA_IN_PAD, tm=tm, tn=MLA_IN_PAD, epilogue=_ep_mla_in,
        extras=[(mla_q_norm_g[0].reshape(1, Q_LORA), (1, Q_LORA), lambda j, i: (0, 0)),
                (mla_kv_norm_g[0].reshape(1, KV_LORA), (1, KV_LORA), lambda j, i: (0, 0))],
        outs=[(jax.ShapeDtypeStruct((n_all, Q_LORA), BF16), (tm, Q_LORA), lambda j, i: (i, 0)),
              (jax.ShapeDtypeStruct((n_all, KV_LORA), BF16), (tm, KV_LORA), lambda j, i: (i, 0)),
              (jax.ShapeDtypeStruct((n_all, LANE), F32), (tm, LANE), lambda j, i: (i, 0))],
        name="mla_in")

    rope_tab = _axial_rope_tables(seq, tm)
    n_ctx_tiles = n_ctx // tm
    seq_tiles = seq // tm
    pad_head = lambda a: jnp.concatenate([a[..., :QK_NOPE], _rope_tail_layout(a[..., QK_NOPE:])], axis=-1)
    pad_gain = lambda g, s: pad_head(g * s).reshape(1, QK_PAD)
    w_uq = pad_head(mla_w_uq[0].reshape(Q_LORA, MLA_HEADS, QK_DIM)).reshape(Q_LORA, MLA_HEADS * QK_PAD)
    tn_up = 1024
    (q,) = _matmul(
        c_q, w_uq, row0=n_ctx, n_cols=MLA_HEADS * QK_PAD, tm=tm, tn=tn_up, epilogue=_ep_mla_q,
        extras=[(pad_gain(mla_q_g[0], QK_DIM ** -0.5), (1, QK_PAD), lambda j, i: (0, 0)),
                (rope_tab, (tm, 2 * LANE), lambda j, i: (1 + i % seq_tiles, 0))],
        outs=[(jax.ShapeDtypeStruct((n_lat, MLA_HEADS * QK_PAD), BF16), (tm, tn_up), lambda j, i: (i, j))],
        name="mla_up_q")
    heads_per_tile = tn_up // (QK_NOPE + V_DIM)
    k, v = _matmul(
        c_kv, mla_w_ukv, w_lead=(0,), n_cols=MLA_HEADS * (QK_NOPE + V_DIM), tm=tm, tn=tn_up, epilogue=_ep_mla_kv,
        extras=[(pad_gain(mla_k_g[0], 1.0), (1, QK_PAD), lambda j, i: (0, 0)),
                (rope_tab, (tm, 2 * LANE),
                 lambda j, i: (jnp.where(i < n_ctx_tiles, 0, 1 + (i - n_ctx_tiles) % seq_tiles), 0)),
                (k_pe, (tm, LANE), lambda j, i: (i, 0))],
        outs=[(jax.ShapeDtypeStruct((n_all, MLA_HEADS * QK_PAD), BF16), (tm, heads_per_tile * QK_PAD),
               lambda j, i: (i, j)),
              (jax.ShapeDtypeStruct((n_all, MLA_HEADS * V_DIM), BF16), (tm, heads_per_tile * V_DIM),
               lambda j, i: (i, j))],
        name="mla_up_kv")
    return q, k, v


def kernel(x, c, ctx, c_ctx, ada_w, ada_b, norm1_g, norm2_g, ab_w_in, ab_w_out, na_q_g, na_k_g, na_rel_bias,
           sg_norm_g, sg_norm_b, sg_w, sg_b, mla_w_in, mla_q_norm_g, mla_kv_norm_g, mla_w_uq, mla_w_ukv,
           mla_q_g, mla_k_g, mla_w_out, moe_w_router, moe_b_router, moe_w_gu, moe_b_gu, moe_w_down, moe_b_down):
    n_batch, seq, d = x.shape
    t_ctx = ctx.shape[1]
    n_ctx = n_batch * t_ctx
    n_lat = n_batch * seq
    n_all = n_ctx + n_lat
    tm = MATMUL_TM
    tn = 512
    cvec = jnp.concatenate([c_ctx[None, :], c], axis=0)
    xs = jnp.concatenate([ctx.reshape(n_ctx, d), x.reshape(n_lat, d)], axis=0)
    mrow = functools.partial(_mod_row, tm=tm, n_ctx_rows=n_ctx, seq=seq)

    mod = _ada_mod(cvec, ada_w, ada_b, 0)
    h = _norm_mod(xs, norm1_g[0], mod, 0, 1, n_ctx, seq)
    w_in = ab_w_in
    q_gain = (na_q_g[0] * (NA_HEAD_DIM ** -0.5)).reshape(1, NA_HEAD_DIM)
    k_gain = na_k_g[0].reshape(1, NA_HEAD_DIM)

    def in_proj(col0, n_cols, epilogue, extras, name):
        (o,) = _matmul(h, w_in, w_lead=(0,), col0=col0, n_cols=n_cols, tm=tm, tn=tn, epilogue=epilogue,
                       extras=extras,
                       outs=[(jax.ShapeDtypeStruct((n_all, n_cols), BF16), (tm, tn), lambda j, i: (i, j))],
                       name=name)
        return o

    gain_spec = lambda g: [(g, (1, NA_HEAD_DIM), lambda j, i: (0, 0))]
    q = in_proj(0, NA_WIDTH, _ep_head_rms, gain_spec(q_gain), "ab_in_q")
    k = in_proj(NA_WIDTH, NA_WIDTH, _ep_head_rms, gain_spec(k_gain), "ab_in_k")
    v = in_proj(2 * NA_WIDTH, NA_WIDTH, _ep_cast, [], "ab_in_v")
    uv = in_proj(3 * NA_WIDTH, 2 * SG_WIDTH, _ep_gelu, [], "ab_in_uv")

    a_lat = _na_attention(q, k, v, na_rel_bias[0], n_batch=n_batch, seq=seq, t_ctx=t_ctx)
    a_ctx = _attention(q, k, v, n_batch=n_batch, n_heads=NA_HEADS, dq=NA_HEAD_DIM, dv=NA_HEAD_DIM,
                       tq=t_ctx, n_q_tiles=1, q_blk0=0, t_k=t_ctx, k_blk0=0, name="ctx_attention")
    gated = _spatial_gating(uv, sg_norm_g[0], sg_norm_b[0], sg_w[0], sg_b[0])
    mix = jnp.concatenate([jnp.concatenate([a_ctx, a_lat], axis=0), gated], axis=1)

    def out_proj(inp, w, resid, resid_row0, n_ctx_rows, name):
        mr = functools.partial(_mod_row, tm=tm, n_ctx_rows=n_ctx_rows, seq=seq)
        (o,) = _matmul(inp, w, w_lead=(0,), n_cols=d, tm=tm, tn=tn, epilogue=_ep_residual,
                       extras=[(resid, (tm, tn), lambda j, i: (resid_row0 // tm + i, j)),
                               (mod, (None, 1, tn), lambda j, i: (mr(i), 0, 2 * d // tn + j))],
                       outs=[(jax.ShapeDtypeStruct((inp.shape[0], d), F32), (tm, tn), lambda j, i: (i, j))],
                       name=name)
        return o

    xs = out_proj(mix, ab_w_out, xs, 0, n_ctx, "ab_out")
    h2, logits = _norm_mod(xs, norm2_g[0], mod, 3, 4, n_ctx, seq, router=(moe_w_router[0], moe_b_router[0]))
    xs = _moe_block(xs, h2, logits, mod, 0, n_ctx, seq, moe_w_gu, moe_b_gu, moe_w_down, moe_b_down)

    mod = _ada_mod(cvec, ada_w, ada_b, 1)
    h = _norm_mod(xs, norm1_g[1], mod, 0, 1, n_ctx, seq)
    q, k, v = _mla_project(h, n_ctx, seq, tm, mla_w_in, mla_q_norm_g, mla_kv_norm_g, mla_w_uq, mla_w_ukv,
                           mla_q_g, mla_k_g)

    tq = seq
    attn = _attention(q, k, v, n_batch=n_batch, n_heads=MLA_HEADS, dq=QK_PAD, dv=V_DIM, tq=tq,
                      n_q_tiles=seq // tq, q_blk0=0, t_k=seq, k_blk0=n_ctx // seq, t_ctx=t_ctx, n_sub=8,
                      name="mla_attention")
    x_lat = out_proj(attn, mla_w_out, xs, n_ctx, 0, "mla_out")
    h2, logits = _norm_mod(x_lat, norm2_g[1], mod, 3, 4, 0, seq, router=(moe_w_router[1], moe_b_router[1]))
    x_lat = _moe_block(x_lat, h2, logits, mod, 1, 0, seq, moe_w_gu, moe_b_gu, moe_w_down, moe_b_down)
    return x_lat.reshape(n_batch, seq, d)
```

```python
import functools

import numpy as np
import jax
import jax.numpy as jnp
from jax import lax
from jax.experimental import pallas as pl
from jax.experimental.pallas import tpu as pltpu

F32 = jnp.float32
BF16 = jnp.bfloat16

D_MODEL = 2048
GRID_W = 64
EPS = 1e-6
NEG = -1e30

NA_HEADS = 8
NA_HEAD_DIM = 128
NA_WIDTH = NA_HEADS * NA_HEAD_DIM
NA_WIN_H = 8
NA_WIN_W = 16
SG_GROUPS = 8
SG_WIDTH = D_MODEL // 2
SG_GROUP_DIM = SG_WIDTH // SG_GROUPS
SG_CHUNK = 128

MLA_HEADS = 16
Q_LORA = 512
KV_LORA = 256
QK_NOPE = 128
QK_ROPE = 64
V_DIM = 128
QK_DIM = QK_NOPE + QK_ROPE
QK_PAD = 256
MLA_IN_PAD = Q_LORA + KV_LORA + 128
ROPE_THETA = 10000.0

N_EXPERTS = 32
TOP_K = 4
D_FF = D_MODEL
SWIGLU_ALPHA = 1.702
SWIGLU_LIMIT = 7.0

LANE = 128
V7X_VMEM_LIMIT = 56 * 1024 * 1024

MATMUL_TM = 1024
MATMUL_SUB_ROWS = 256
NA_TILE_ROWS = 4
NA_BAND_ROWS = NA_TILE_ROWS + NA_WIN_H - 1
MOE_TM = 1024
MOE_PARTS = 4
MOE_TN_GU = 1024
MOE_TN_DOWN = 2048


def _params(n_axes):
    return pltpu.CompilerParams(dimension_semantics=("arbitrary",) * n_axes,
                                vmem_limit_bytes=V7X_VMEM_LIMIT)


def _norm_mod_val(x, g, shift, scale):
    y = x * lax.rsqrt(jnp.mean(x * x, axis=-1, keepdims=True) + EPS) * g
    return y * (1.0 + scale) + shift


def _norm_mod_body(x_ref, g_ref, sh_ref, sc_ref):
    return _norm_mod_val(x_ref[...], g_ref[...], sh_ref[...], sc_ref[...])


class _SplitRows:
    def __init__(self, ctx_ref, lat_ref, is_ctx):
        self.ctx_ref, self.lat_ref, self.is_ctx = ctx_ref, lat_ref, is_ctx
        self.shape, self.dtype = lat_ref.shape, lat_ref.dtype

    def __getitem__(self, key):
        return jnp.where(self.is_ctx, self.ctx_ref[key], self.lat_ref[key])


def _split_specs(block, index_map, n_ctx_tiles):
    def ctx_map(*ids):
        r, *rest = index_map(*ids)
        return (jnp.minimum(r, n_ctx_tiles - 1), *rest)

    def lat_map(*ids):
        r, *rest = index_map(*ids)
        return (jnp.maximum(r - n_ctx_tiles, 0), *rest)

    return [pl.BlockSpec(block, ctx_map), pl.BlockSpec(block, lat_map)]


def _norm_mod_kernel(x_ref, g_ref, sh_ref, sc_ref, o_ref):
    o_ref[...] = _norm_mod_body(x_ref, g_ref, sh_ref, sc_ref).astype(o_ref.dtype)


def _norm_mod_split_kernel(xc_ref, xl_ref, g_ref, sh_ref, sc_ref, o_ref, *, n_ctx_tiles):
    x_ref = _SplitRows(xc_ref, xl_ref, pl.program_id(0) < n_ctx_tiles)
    o_ref[...] = _norm_mod_body(x_ref, g_ref, sh_ref, sc_ref).astype(o_ref.dtype)


def _norm_mod_router_kernel(x_ref, g_ref, sh_ref, sc_ref, wcat_ref, br_ref, o_ref, lg_ref):
    h = _norm_mod_body(x_ref, g_ref, sh_ref, sc_ref)
    h_hi = h.astype(BF16)
    o_ref[...] = h_hi
    n_e = lg_ref.shape[1]
    h_lo = (h - h_hi.astype(F32)).astype(BF16)
    a = jnp.dot(h_hi, wcat_ref[...], preferred_element_type=F32)
    b = jnp.dot(h_lo, wcat_ref[:, :n_e], preferred_element_type=F32)
    lg_ref[...] = a[:, :n_e] + a[:, n_e:] + b + br_ref[...]


def _mod_row(i, tm, n_ctx_rows, seq):
    n_ctx_tiles = n_ctx_rows // tm
    return jnp.where(i < n_ctx_tiles, 0, 1 + (i - n_ctx_tiles) // (seq // tm))


def _norm_mod(x, gain, mod, k_shift, k_scale, n_ctx_rows, seq, router=None, tm=256):
    split = isinstance(x, tuple)
    rows = sum(a.shape[0] for a in x) if split else x.shape[0]
    d = gain.shape[0]
    mrow = functools.partial(_mod_row, tm=tm, n_ctx_rows=n_ctx_rows, seq=seq)
    x_specs = (_split_specs((tm, d), lambda i: (i, 0), n_ctx_rows // tm) if split
               else [pl.BlockSpec((tm, d), lambda i: (i, 0))])
    in_specs = x_specs + [
        pl.BlockSpec((1, d), lambda i: (0, 0)),
        pl.BlockSpec((None, 1, d), lambda i: (mrow(i), 0, k_shift)),
        pl.BlockSpec((None, 1, d), lambda i: (mrow(i), 0, k_scale)),
    ]
    args = [*x, gain.reshape(1, d), mod, mod] if split else [x, gain.reshape(1, d), mod, mod]
    if router is None:
        kern = (functools.partial(_norm_mod_split_kernel, n_ctx_tiles=n_ctx_rows // tm) if split
                else _norm_mod_kernel)
        return pl.pallas_call(
            kern, grid=(rows // tm,), in_specs=in_specs,
            out_specs=pl.BlockSpec((tm, d), lambda i: (i, 0)),
            out_shape=jax.ShapeDtypeStruct((rows, d), BF16),
            compiler_params=_params(1), name="norm_mod")(*args)
    assert not split
    w_r, b_r = router
    n_e = w_r.shape[1]
    w_hi = w_r.astype(BF16)
    w_cat = jnp.concatenate([w_hi, (w_r - w_hi.astype(F32)).astype(BF16)], axis=1)
    in_specs += [pl.BlockSpec((d, 2 * n_e), lambda i: (0, 0)), pl.BlockSpec((1, n_e), lambda i: (0, 0))]
    return pl.pallas_call(
        _norm_mod_router_kernel, grid=(rows // tm,), in_specs=in_specs,
        out_specs=[pl.BlockSpec((tm, d), lambda i: (i, 0)), pl.BlockSpec((tm, n_e), lambda i: (i, 0))],
        out_shape=[jax.ShapeDtypeStruct((rows, d), BF16), jax.ShapeDtypeStruct((rows, n_e), F32)],
        compiler_params=_params(1), name="norm_mod_router")(*args, w_cat, b_r.reshape(1, n_e))


def _matmul_kernel(*refs, x_split, extra_split, n_out, epilogue, n_sub, n_ctx_tiles):
    is_ctx = pl.program_id(1) < n_ctx_tiles
    refs = list(refs)

    def take(split):
        if split:
            return _SplitRows(refs.pop(0), refs.pop(0), is_ctx)
        return refs.pop(0)

    xs = [take(sp) for sp in x_split]
    w_ref = refs.pop(0)
    extra = [take(sp) for sp in extra_split]
    outs, wbf_ref = refs[:n_out], refs[-1]
    tm = outs[0].shape[0]

    @pl.when(pl.program_id(1) == 0)
    def _():
        wbf_ref[...] = w_ref[...].astype(BF16)

    sub = tm // n_sub

    def dot(r):
        acc, k0 = None, 0
        for x in xs:
            part = jnp.dot(x[r * sub:(r + 1) * sub, :], wbf_ref[k0:k0 + x.shape[1], :], preferred_element_type=F32)
            acc = part if acc is None else acc + part
            k0 += x.shape[1]
        return acc

    acc_next = dot(0)
    for r in range(n_sub):
        acc = acc_next
        if r + 1 < n_sub:
            acc_next = dot(r + 1)
        rows = slice(r * sub, (r + 1) * sub)
        epilogue(acc, [_RowView(e, rows) if e.shape[0] == tm else e for e in extra],
                 [_RowView(o, rows) for o in outs])


class _RowView:
    def __init__(self, ref, rows):
        self.ref, self.rows = ref, rows
        self.shape = (rows.stop - rows.start,) + tuple(ref.shape[1:])
        self.dtype = ref.dtype

    def _key(self, key):
        if key is Ellipsis:
            return (self.rows, slice(None))
        assert isinstance(key, tuple) and key[0] == slice(None), key
        return (self.rows,) + tuple(key[1:])

    def __getitem__(self, key):
        return self.ref[self._key(key)]

    def __setitem__(self, key, value):
        self.ref[self._key(key)] = value


def _matmul(x, w, *, w_lead=(), col0=0, n_cols, tm, tn, epilogue, extras=(), outs, name, row0=0, m=None):
    parts = x if isinstance(x, list) else [x]
    n_rows = lambda a: sum(b.shape[0] for b in a) if isinstance(a, tuple) else a.shape[0]
    width = lambda a: a[0].shape[1] if isinstance(a, tuple) else a.shape[1]
    splits = [a for a in parts + [e[0] for e in extras] if isinstance(a, tuple)]
    n_ctx_tiles = splits[0][0].shape[0] // tm if splits else 0
    assert all(a[0].shape[0] == n_ctx_tiles * tm for a in splits) and (row0 == 0 or not splits)
    k = sum(width(a) for a in parts)
    m = n_rows(parts[0]) - row0 if m is None else m
    assert m % tm == 0 and row0 % tm == 0 and n_cols % tn == 0 and col0 % tn == 0
    lead = tuple(w_lead)

    def specs(a, block, index_map):
        if isinstance(a, tuple):
            return _split_specs(block, index_map, n_ctx_tiles)
        return [pl.BlockSpec(block, index_map)]

    in_specs, args = [], []
    for a in parts:
        in_specs += specs(a, (tm, width(a)), lambda j, i: (row0 // tm + i, 0))
        args += list(a) if isinstance(a, tuple) else [a]
    in_specs.append(pl.BlockSpec((None,) * len(lead) + (k, tn), lambda j, i: lead + (0, col0 // tn + j)))
    args.append(w)
    for a, bs, im in extras:
        in_specs += specs(a, bs, im)
        args += list(a) if isinstance(a, tuple) else [a]
    out_specs = [pl.BlockSpec(bs, im) for _, bs, im in outs]
    kern = functools.partial(
        _matmul_kernel, x_split=tuple(isinstance(a, tuple) for a in parts),
        extra_split=tuple(isinstance(e[0], tuple) for e in extras), n_out=len(outs), epilogue=epilogue,
        n_sub=max(1, tm // MATMUL_SUB_ROWS), n_ctx_tiles=n_ctx_tiles)
    return pl.pallas_call(
        kern, grid=(n_cols // tn, m // tm), in_specs=in_specs, out_specs=out_specs,
        out_shape=[s for s, _, _ in outs],
        scratch_shapes=[pltpu.VMEM((k, tn), BF16)],
        compiler_params=_params(2), name=name)(*args)


def _ep_bias(acc, extra, outs):
    outs[0][...] = acc + extra[0][...]


def _ep_cast(acc, extra, outs):
    outs[0][...] = acc.astype(outs[0].dtype)


def _ep_gelu(acc, extra, outs):
    outs[0][...] = jax.nn.gelu(acc).astype(outs[0].dtype)


def _ep_head_rms(acc, extra, outs):
    g = extra[0][...]
    for h in range(acc.shape[1] // LANE):
        a = acc[:, h * LANE:(h + 1) * LANE]
        r = lax.rsqrt(jnp.mean(a * a, axis=-1, keepdims=True) + EPS)
        outs[0][:, h * LANE:(h + 1) * LANE] = (a * r * g).astype(outs[0].dtype)


def _ep_residual(acc, extra, outs):
    outs[0][...] = extra[0][...] + extra[1][...] * acc


def _rms(a, g):
    return a * lax.rsqrt(jnp.mean(a * a, axis=-1, keepdims=True) + EPS) * g


def _ep_mla_in(acc, extra, outs):
    outs[0][...] = _rms(acc[:, :Q_LORA], extra[0][...]).astype(BF16)
    outs[1][...] = _rms(acc[:, Q_LORA:Q_LORA + KV_LORA], extra[1][...]).astype(BF16)
    outs[2][...] = acc[:, Q_LORA + KV_LORA:]


def _rope_tail(t, tab_ref):
    return t * tab_ref[:, :LANE] + pltpu.roll(t, LANE // 2, axis=1) * tab_ref[:, LANE:]


def _ep_mla_q(acc, extra, outs):
    g_ref, tab_ref = extra
    g0 = g_ref[:, :LANE]
    g1 = g_ref[:, LANE:]
    for h in range(acc.shape[1] // QK_PAD):
        a0 = acc[:, h * QK_PAD:h * QK_PAD + LANE]
        a1 = acc[:, h * QK_PAD + LANE:(h + 1) * QK_PAD]
        r = lax.rsqrt(jnp.sum(a0 * a0 + a1 * a1, axis=-1, keepdims=True) * (1.0 / QK_DIM) + EPS)
        outs[0][:, h * QK_PAD:h * QK_PAD + LANE] = (a0 * r * g0).astype(BF16)
        outs[0][:, h * QK_PAD + LANE:(h + 1) * QK_PAD] = _rope_tail(a1 * r * g1, tab_ref).astype(BF16)


def _ep_mla_kv(acc, extra, outs):
    g_ref, tab_ref, pe_ref = extra
    k_out, v_out = outs
    g0 = g_ref[:, :LANE]
    pe = pe_ref[...]
    pe_sq = pe * pe
    pe_roped = _rope_tail(pe * g_ref[:, LANE:], tab_ref)
    for h in range(acc.shape[1] // (QK_NOPE + V_DIM)):
        base = h * (QK_NOPE + V_DIM)
        kn = acc[:, base:base + QK_NOPE]
        r = lax.rsqrt(jnp.sum(kn * kn + pe_sq, axis=-1, keepdims=True) * (1.0 / QK_DIM) + EPS)
        k_out[:, h * QK_PAD:h * QK_PAD + LANE] = (kn * r * g0).astype(BF16)
        k_out[:, h * QK_PAD + LANE:(h + 1) * QK_PAD] = (pe_roped * r).astype(BF16)
        v_out[:, h * V_DIM:(h + 1) * V_DIM] = acc[:, base + QK_NOPE:base + QK_NOPE + V_DIM].astype(BF16)


def _softmax_pv(s_parts, v_parts):
    m = s_parts[0].max(axis=-1, keepdims=True)
    for s in s_parts[1:]:
        m = jnp.maximum(m, s.max(axis=-1, keepdims=True))
    l = 0.0
    o = 0.0
    for s, v in zip(s_parts, v_parts):
        p = jnp.exp(s - m)
        l = l + p.sum(axis=-1, keepdims=True)
        o = o + jnp.dot(p.astype(BF16), v, preferred_element_type=F32)
    return o / l


def _qk(q, k):
    return lax.dot_general(q, k, (((1,), (1,)), ((), ())), preferred_element_type=F32)


def _attn_kernel(*refs, has_ctx, n_sub):
    if has_ctx:
        q_ref, k_ref, v_ref, kc_ref, vc_ref, o_ref = refs
    else:
        q_ref, k_ref, v_ref, o_ref = refs
    sub = q_ref.shape[0] // n_sub

    def scores(r):
        q = q_ref[r * sub:(r + 1) * sub, :]
        parts = [_qk(q, k_ref[...])]
        if has_ctx:
            parts.append(_qk(q, kc_ref[...]))
        return parts

    s_next = scores(0)
    for r in range(n_sub):
        s_cur = s_next
        if r + 1 < n_sub:
            s_next = scores(r + 1)
        v_parts = [v_ref[...]] + ([vc_ref[...]] if has_ctx else [])
        o_ref[r * sub:(r + 1) * sub, :] = _softmax_pv(s_cur, v_parts).astype(o_ref.dtype)


def _attention(q, k, v, *, n_batch, n_heads, dq, dv, tq, n_q_tiles, q_blk0, t_k, k_blk0, t_ctx=None, n_sub=1, name):
    has_ctx = t_ctx is not None
    in_specs = [
        pl.BlockSpec((tq, dq), lambda b, h, i: (q_blk0 + b * n_q_tiles + i, h)),
        pl.BlockSpec((t_k, dq), lambda b, h, i: (k_blk0 + b, h)),
        pl.BlockSpec((t_k, dv), lambda b, h, i: (k_blk0 + b, h)),
    ]
    args = [q, k, v]
    if has_ctx:
        in_specs += [pl.BlockSpec((t_ctx, dq), lambda b, h, i: (b, h)),
                     pl.BlockSpec((t_ctx, dv), lambda b, h, i: (b, h))]
        args += [k, v]
    return pl.pallas_call(
        functools.partial(_attn_kernel, has_ctx=has_ctx, n_sub=n_sub),
        grid=(n_batch, n_heads, n_q_tiles), in_specs=in_specs,
        out_specs=pl.BlockSpec((tq, dv), lambda b, h, i: (b * n_q_tiles + i, h)),
        out_shape=jax.ShapeDtypeStruct((n_batch * n_q_tiles * tq, n_heads * dv), BF16),
        compiler_params=_params(3), name=name)(*args)


def _na_band_start(t, rows):
    return jnp.clip(t * NA_TILE_ROWS - NA_WIN_H // 2, 0, rows - NA_BAND_ROWS)


def _na_kernel(q_ref, k_ref, v_ref, kc_ref, vc_ref, bias_ref, o_ref, *, rows):
    t = pl.program_id(1)
    ks = pl.multiple_of(_na_band_start(t, rows) * GRID_W, GRID_W)
    band = pl.ds(ks, NA_BAND_ROWS * GRID_W)

    def scores(h):
        cols = slice(h * NA_HEAD_DIM, (h + 1) * NA_HEAD_DIM)
        q = q_ref[:, cols]
        return [_qk(q, k_ref[band, cols]) + bias_ref[h], _qk(q, kc_ref[:, cols])]

    s_next = scores(0)
    for h in range(NA_HEADS):
        s_cur = s_next
        if h + 1 < NA_HEADS:
            s_next = scores(h + 1)
        cols = slice(h * NA_HEAD_DIM, (h + 1) * NA_HEAD_DIM)
        o_ref[:, cols] = _softmax_pv(s_cur, [v_ref[band, cols], vc_ref[:, cols]]).astype(o_ref.dtype)


def _na_bias_tables(rel_bias, rows):
    n_tiles = rows // NA_TILE_ROWS
    n_r, n_c = 2 * NA_WIN_H - 1, 2 * NA_WIN_W - 1
    qr = np.arange(NA_TILE_ROWS)[:, None]
    ur = np.arange(NA_BAND_ROWS)[None, :]
    qc = np.arange(GRID_W)[:, None]
    kc = np.arange(GRID_W)[None, :]
    c0 = np.clip(qc - NA_WIN_W // 2, 0, GRID_W - NA_WIN_W)
    col_valid = (kc >= c0) & (kc < c0 + NA_WIN_W)
    col_onehot = np.eye(n_c, dtype=np.float32)[np.clip(kc - qc + NA_WIN_W - 1, 0, n_c - 1)]
    geoms, ids = [], []
    for t in range(n_tiles):
        u0 = int(np.clip(t * NA_TILE_ROWS - NA_WIN_H // 2, 0, rows - NA_BAND_ROWS))
        r = t * NA_TILE_ROWS + qr
        key_row = u0 + ur
        r0 = np.clip(r - NA_WIN_H // 2, 0, rows - NA_WIN_H)
        geom = ((key_row >= r0) & (key_row < r0 + NA_WIN_H), np.clip(key_row - r + NA_WIN_H - 1, 0, n_r - 1))
        for gi, g in enumerate(geoms):
            if all(np.array_equal(a, b) for a, b in zip(g, geom)):
                ids.append(gi)
                break
        else:
            ids.append(len(geoms))
            geoms.append(geom)
    row_valid = np.stack([g[0] for g in geoms])
    row_onehot = np.eye(n_r, dtype=np.float32)[np.stack([g[1] for g in geoms])]
    bias = jnp.einsum('gaur,hrc,bkc->ghabuk', row_onehot, rel_bias.astype(F32), col_onehot,
                      precision=lax.Precision.HIGHEST)
    valid = row_valid[:, None, :, None, :, None] & col_valid[None, None, None, :, None, :]
    tables = jnp.where(valid, bias, NEG)
    return tables.reshape(len(geoms), NA_HEADS, NA_TILE_ROWS * GRID_W, NA_BAND_ROWS * GRID_W), ids


def _na_attention(q, k, v, rel_bias, *, n_batch, seq, t_ctx):
    rows = seq // GRID_W
    n_tiles = rows // NA_TILE_ROWS
    tq = NA_TILE_ROWS * GRID_W
    tables, ids = _na_bias_tables(rel_bias, rows)
    assert ids == [0] + [1] * (n_tiles - 2) + [2], ids
    ctx_tiles = n_batch * t_ctx // tq
    ctx_units = n_batch * t_ctx // seq

    def table_id(t):
        return jnp.where(t == 0, 0, jnp.where(t == n_tiles - 1, 2, 1))

    return pl.pallas_call(
        functools.partial(_na_kernel, rows=rows),
        grid=(n_batch, n_tiles),
        in_specs=[
            pl.BlockSpec((tq, NA_WIDTH), lambda b, t: (ctx_tiles + b * n_tiles + t, 0)),
            pl.BlockSpec((seq, NA_WIDTH), lambda b, t: (ctx_units + b, 0)),
            pl.BlockSpec((seq, NA_WIDTH), lambda b, t: (ctx_units + b, 0)),
            pl.BlockSpec((t_ctx, NA_WIDTH), lambda b, t: (b, 0)),
            pl.BlockSpec((t_ctx, NA_WIDTH), lambda b, t: (b, 0)),
            pl.BlockSpec((None, NA_HEADS, tq, NA_BAND_ROWS * GRID_W), lambda b, t: (table_id(t), 0, 0, 0)),
        ],
        out_specs=pl.BlockSpec((tq, NA_WIDTH), lambda b, t: (b * n_tiles + t, 0)),
        out_shape=jax.ShapeDtypeStruct((n_batch * seq, NA_WIDTH), BF16),
        compiler_params=_params(2), name="na_attention")(q, k, v, k, v, tables)


def _sg_kernel(uv_ref, g_ref, b_ref, ws_ref, bs_ref, o_ref, *, chunks):
    for c in range(chunks):
        r = slice(c * SG_CHUNK, (c + 1) * SG_CHUNK)
        z = uv_ref[r, SG_WIDTH:].astype(F32)
        mu = jnp.mean(z, axis=-1, keepdims=True)
        zc = z - mu
        var = jnp.mean(zc * zc, axis=-1, keepdims=True)
        zn = (zc * lax.rsqrt(var + EPS) * g_ref[...] + b_ref[...]).astype(BF16)
        for g in range(SG_GROUPS):
            cols = slice(g * SG_GROUP_DIM, (g + 1) * SG_GROUP_DIM)
            mixed = jnp.dot(ws_ref[g], zn[:, cols], preferred_element_type=F32) + bs_ref[g]
            o_ref[r, cols] = (uv_ref[r, cols].astype(F32) * mixed).astype(o_ref.dtype)


def _spatial_gating(uv, ln_g, ln_b, w_s, b_s, tm=512):
    rows = uv.shape[0]
    bs = jnp.broadcast_to(b_s.astype(F32)[:, :, None], (SG_GROUPS, SG_CHUNK, SG_GROUP_DIM))
    return pl.pallas_call(
        functools.partial(_sg_kernel, chunks=tm // SG_CHUNK),
        grid=(rows // tm,),
        in_specs=[
            pl.BlockSpec((tm, 2 * SG_WIDTH), lambda i: (i, 0)),
            pl.BlockSpec((1, SG_WIDTH), lambda i: (0, 0)),
            pl.BlockSpec((1, SG_WIDTH), lambda i: (0, 0)),
            pl.BlockSpec((SG_GROUPS, SG_CHUNK, SG_CHUNK), lambda i: (0, 0, 0)),
            pl.BlockSpec((SG_GROUPS, SG_CHUNK, SG_GROUP_DIM), lambda i: (0, 0, 0)),
        ],
        out_specs=pl.BlockSpec((tm, SG_WIDTH), lambda i: (i, 0)),
        out_shape=jax.ShapeDtypeStruct((rows, SG_WIDTH), BF16),
        compiler_params=_params(1), name="spatial_gating")(
            uv, ln_g.reshape(1, SG_WIDTH), ln_b.reshape(1, SG_WIDTH), w_s.astype(BF16), bs)


def _swiglu(g, u):
    g = jnp.minimum(g, SWIGLU_LIMIT)
    u = jnp.clip(u, -SWIGLU_LIMIT, SWIGLU_LIMIT)
    return (u + 1.0) * (g * jax.nn.sigmoid(SWIGLU_ALPHA * g))


def _for_each_valid_part(i, tv_ref, o_ref, compute):
    part = o_ref.shape[0] // MOE_PARTS
    n_valid = (tv_ref[i] + part - 1) // part
    rows = [slice(p * part, (p + 1) * part) for p in range(MOE_PARTS)]
    for n in range(MOE_PARTS + 1):

        @pl.when(n_valid == n)
        def _(n=n):
            acc_next = compute.matmul(rows[0]) if n else None
            for p in range(n):
                acc = acc_next
                if p + 1 < n:
                    acc_next = compute.matmul(rows[p + 1])
                o_ref[rows[p], :] = compute.epilogue(acc).astype(o_ref.dtype)
            for p in range(n, MOE_PARTS):
                o_ref[rows[p], :] = jnp.zeros((part, o_ref.shape[1]), o_ref.dtype)


class _GateUp:
    def __init__(self, x_ref, w_bf, bg_ref, bu_ref):
        self.x_ref, self.w_bf, self.bg_ref, self.bu_ref = x_ref, w_bf, bg_ref, bu_ref

    def matmul(self, rows):
        return jnp.dot(self.x_ref[rows, :], self.w_bf[...], preferred_element_type=F32)

    def epilogue(self, acc):
        tn = acc.shape[1] // 2
        return _swiglu(acc[:, :tn] + self.bg_ref[...], acc[:, tn:] + self.bu_ref[...])


class _Down:
    def __init__(self, h_ref, w_bf, b_ref):
        self.h_ref, self.w_bf, self.b_ref = h_ref, w_bf, b_ref

    def matmul(self, rows):
        return jnp.dot(self.h_ref[rows, :], self.w_bf[...], preferred_element_type=F32)

    def epilogue(self, acc):
        return acc + self.b_ref[...]


def _expert_weight_copies(w_hbm, land, sem, layer, expert, cols):
    tn = land.shape[1] // len(cols)
    return [pltpu.make_async_copy(w_hbm.at[layer, expert, :, pl.ds(pl.multiple_of(col, tn), tn)],
                                  land.at[:, pl.ds(k * tn, tn)], sem.at[k])
            for k, col in enumerate(cols)]


def _cast_rows(src, dst, chunk=256):
    def body(c, carry):
        rows = pl.ds(pl.multiple_of(c * chunk, chunk), chunk)
        dst[rows, :] = src[rows, :].astype(dst.dtype)
        return carry
    lax.fori_loop(0, src.shape[0] // chunk, body, 0)


def _refresh_expert_weights(te_ref, first_ref, nxt_ref, w_hbm, land, bf, sem, layer, col_fn, n_j):
    j = pl.program_id(0)
    i = pl.program_id(1)

    @pl.when(first_ref[i] == 1)
    def _():
        cur = _expert_weight_copies(w_hbm, land, sem, layer, te_ref[i], col_fn(j))

        @pl.when((i == 0) & (j == 0))
        def _():
            for c in cur:
                c.start()

        for c in cur:
            c.wait()
        _cast_rows(land, bf)
        nxt = nxt_ref[i]

        @pl.when(nxt >= 0)
        def _():
            for c in _expert_weight_copies(w_hbm, land, sem, layer, nxt, col_fn(j)):
                c.start()

        @pl.when((nxt < 0) & (j + 1 < n_j))
        def _():
            for c in _expert_weight_copies(w_hbm, land, sem, layer, te_ref[0], col_fn(j + 1)):
                c.start()


def _moe_gu_kernel(te_ref, tv_ref, ts_ref, first_ref, nxt_ref, x_ref, w_hbm, bg_ref, bu_ref, o_ref, w_land, w_bf, sem,
                   *, layer, n_j):
    tn = w_bf.shape[1] // 2
    _refresh_expert_weights(te_ref, first_ref, nxt_ref, w_hbm, w_land, w_bf, sem, layer,
                            lambda j: (j * tn, D_FF + j * tn), n_j)
    _for_each_valid_part(pl.program_id(1), tv_ref, o_ref, _GateUp(x_ref, w_bf, bg_ref, bu_ref))


def _moe_down_kernel(te_ref, tv_ref, ts_ref, first_ref, nxt_ref, h_ref, w_hbm, b_ref, o_ref, w_land, w_bf, sem,
                     *, layer, n_j):
    tn = w_bf.shape[1]
    _refresh_expert_weights(te_ref, first_ref, nxt_ref, w_hbm, w_land, w_bf, sem, layer,
                            lambda j: (j * tn,), n_j)
    _for_each_valid_part(pl.program_id(1), tv_ref, o_ref, _Down(h_ref, w_bf, b_ref))


def _moe_experts(x_sorted, tile_e, tile_valid, tile_src, layer, w_gu, b_gu, w_down, b_down):
    r_pad, d = x_sorted.shape
    tm, tn, tn_down = MOE_TM, MOE_TN_GU, MOE_TN_DOWN
    n_tiles = r_pad // tm
    n_j = D_FF // tn
    n_jd = d // tn_down
    idx = jnp.arange(n_tiles, dtype=jnp.int32)
    first = jnp.concatenate([jnp.ones((1,), bool), tile_e[1:] != tile_e[:-1]])
    later_first = lax.cummin(jnp.where(first, idx, n_tiles)[::-1])[::-1]
    next_first = jnp.concatenate([later_first[1:], jnp.full((1,), n_tiles, jnp.int32)])
    nxt = jnp.where(next_first < n_tiles, tile_e[jnp.minimum(next_first, n_tiles - 1)], -1).astype(jnp.int32)
    prefetch = (tile_e, tile_valid, tile_src, first.astype(jnp.int32), nxt)
    x_map = lambda j, i, te, tv, ts, fr, nx: (ts[i], 0)
    out_map = lambda j, i, te, tv, ts, fr, nx: (i, j)

    hid = pl.pallas_call(
        functools.partial(_moe_gu_kernel, layer=layer, n_j=n_j),
        grid_spec=pltpu.PrefetchScalarGridSpec(
            num_scalar_prefetch=5, grid=(n_j, n_tiles),
            in_specs=[
                pl.BlockSpec((tm, d), x_map),
                pl.BlockSpec(memory_space=pl.ANY),
                pl.BlockSpec((None, None, 1, tn), lambda j, i, te, tv, ts, fr, nx: (layer, te[i], 0, j)),
                pl.BlockSpec((None, None, 1, tn), lambda j, i, te, tv, ts, fr, nx: (layer, te[i], 0, n_j + j)),
            ],
            out_specs=pl.BlockSpec((tm, tn), out_map),
            scratch_shapes=[pltpu.VMEM((d, 2 * tn), F32), pltpu.VMEM((d, 2 * tn), BF16),
                            pltpu.SemaphoreType.DMA((2,))]),
        out_shape=jax.ShapeDtypeStruct((r_pad, D_FF), BF16),
        compiler_params=_params(2), name="moe_gate_up")(
            *prefetch, x_sorted, w_gu, b_gu[:, :, None, :], b_gu[:, :, None, :])

    return pl.pallas_call(
        functools.partial(_moe_down_kernel, layer=layer, n_j=n_jd),
        grid_spec=pltpu.PrefetchScalarGridSpec(
            num_scalar_prefetch=5, grid=(n_jd, n_tiles),
            in_specs=[
                pl.BlockSpec((tm, D_FF), x_map),
                pl.BlockSpec(memory_space=pl.ANY),
                pl.BlockSpec((None, None, 1, tn_down), lambda j, i, te, tv, ts, fr, nx: (layer, te[i], 0, j)),
            ],
            out_specs=pl.BlockSpec((tm, tn_down), out_map),
            scratch_shapes=[pltpu.VMEM((D_FF, tn_down), F32), pltpu.VMEM((D_FF, tn_down), BF16),
                            pltpu.SemaphoreType.DMA((1,))]),
        out_shape=jax.ShapeDtypeStruct((r_pad, d), BF16),
        compiler_params=_params(2), name="moe_down")(
            *prefetch, hid, w_down, b_down[:, :, None, :])


def _moe_route(logits):
    n = logits.shape[0]
    nk = n * TOP_K
    tm = MOE_TM
    i32 = jnp.int32
    top_val, top_idx = lax.top_k(logits, TOP_K)
    gate = jax.nn.softmax(top_val, axis=-1)
    flat_e = top_idx.reshape(nk).astype(i32)
    iota = jnp.arange(nk, dtype=i32)
    _, order = lax.sort((flat_e, iota), num_keys=1)
    _, inv_order = lax.sort((order, iota), num_keys=1)
    onehot = flat_e[:, None] == jnp.arange(N_EXPERTS, dtype=i32)[None, :]
    counts = jnp.sum(onehot, axis=0, dtype=i32)
    padded = (counts + tm - 1) // tm * tm
    pad_end = jnp.cumsum(padded)
    pad_start = pad_end - padded
    start = jnp.cumsum(counts) - counts
    pos = inv_order + jnp.sum(jnp.where(onehot, (pad_start - start)[None, :], 0), axis=1)
    pos = pos.reshape(n, TOP_K).T.reshape(nk)

    n_tiles = nk // tm + N_EXPERTS
    n_used = pad_end[-1] // tm
    tile_src = jnp.minimum(jnp.arange(n_tiles, dtype=i32), n_used - 1)
    tile_e = jnp.minimum(jnp.sum(pad_end[None, :] <= (tile_src * tm)[:, None], axis=1, dtype=i32), N_EXPERTS - 1)
    in_use = jnp.arange(n_tiles, dtype=i32) < n_used
    rank0 = tile_src * tm - pad_start[tile_e]
    tile_valid = jnp.where(in_use, jnp.clip(counts[tile_e] - rank0, 0, tm), 0).astype(i32)
    within = jnp.arange(tm, dtype=i32)[None, :]
    valid = within < tile_valid[:, None]
    src = jnp.clip((start[tile_e] + rank0)[:, None] + within, 0, nk - 1)
    filler = (jnp.arange(n_tiles * tm, dtype=i32) % n).reshape(n_tiles, tm)
    rows_tok = jnp.where(valid, order.at[src].get(mode="promise_in_bounds") // TOP_K, filler).reshape(n_tiles * tm)
    return gate, rows_tok, pos, tile_e, tile_valid, tile_src.astype(i32)


def _moe_combine(x_ref, g2_ref, gate_ref, y_ref):
    gate = gate_ref[...]
    f = gate[:, 0:1] * y_ref[0].astype(F32)
    for k in range(1, TOP_K):
        f = f + gate[:, k:k + 1] * y_ref[k].astype(F32)
    return x_ref[...] + g2_ref[...] * f


def _moe_combine_kernel(x_ref, g2_ref, gate_ref, y_ref, o_ref):
    o_ref[...] = _moe_combine(x_ref, g2_ref, gate_ref, y_ref)


def _moe_combine_norm_kernel(x_ref, g2_ref, gate_ref, y_ref, ng_ref, nsh_ref, nsc_ref, o_ref, h_ref):
    x = _moe_combine(x_ref, g2_ref, gate_ref, y_ref)
    o_ref[...] = x
    h_ref[...] = _norm_mod_val(x, ng_ref[...], nsh_ref[...], nsc_ref[...]).astype(h_ref.dtype)


def _moe_block(xs, h2, logits, mod, layer, n_ctx_rows, seq, w_gu, b_gu, w_down, b_down, next_norm=None, tm=256):
    n, d = h2.shape
    gate, rows_tok, pos, tile_e, tile_valid, tile_src = _moe_route(logits)
    x_sorted = h2.at[rows_tok].get(mode="promise_in_bounds")
    y = _moe_experts(x_sorted, tile_e, tile_valid, tile_src, layer, w_gu, b_gu, w_down, b_down)
    y_k = y.at[pos].get(mode="promise_in_bounds").reshape(TOP_K, n, d)
    mrow = functools.partial(_mod_row, tm=tm, n_ctx_rows=n_ctx_rows, seq=seq)
    row_spec = pl.BlockSpec((tm, d), lambda i: (i, 0))
    mod_spec = lambda k: pl.BlockSpec((None, 1, d), lambda i: (mrow(i), 0, k))
    in_specs = [row_spec, mod_spec(5), pl.BlockSpec((tm, TOP_K), lambda i: (i, 0)),
                pl.BlockSpec((TOP_K, tm, d), lambda i: (0, i, 0))]
    if next_norm is None:
        return pl.pallas_call(
            _moe_combine_kernel, grid=(n // tm,), in_specs=in_specs, out_specs=row_spec,
            out_shape=jax.ShapeDtypeStruct((n, d), F32),
            compiler_params=_params(1), name="moe_combine")(xs, mod, gate, y_k)
    gain, next_mod = next_norm
    return pl.pallas_call(
        _moe_combine_norm_kernel, grid=(n // tm,),
        in_specs=in_specs + [pl.BlockSpec((1, d), lambda i: (0, 0)), mod_spec(0), mod_spec(1)],
        out_specs=[row_spec, row_spec],
        out_shape=[jax.ShapeDtypeStruct((n, d), F32), jax.ShapeDtypeStruct((n, d), BF16)],
        compiler_params=_params(1), name="moe_combine_norm")(
            xs, mod, gate, y_k, gain.reshape(1, d), next_mod, next_mod)


def _rope_tail_layout(x):
    half = QK_ROPE // 2
    z = jnp.zeros(x.shape[:-1] + (LANE // 2 - half,), x.dtype)
    return jnp.concatenate([x[..., :half], z, x[..., half:], z], axis=-1)


def _axial_rope_tables(seq, tm):
    t = jnp.arange(seq)
    row = (t // GRID_W).astype(F32)
    col = (t % GRID_W).astype(F32)
    n_freq = QK_ROPE // 4
    inv = ROPE_THETA ** (-jnp.arange(n_freq, dtype=F32) / n_freq)
    ang = jnp.concatenate([row[:, None] * inv, col[:, None] * inv], axis=-1)
    cos, sin = jnp.cos(ang), jnp.sin(ang)
    tab = jnp.concatenate([_rope_tail_layout(jnp.concatenate([cos, cos], axis=-1)),
                           _rope_tail_layout(jnp.concatenate([-sin, sin], axis=-1))], axis=-1)
    ident = jnp.concatenate([_rope_tail_layout(jnp.ones((tm, QK_ROPE), F32)), jnp.zeros((tm, LANE), F32)], axis=-1)
    return jnp.concatenate([ident, tab], axis=0)


def _ada_mod(cvec, ada_w, ada_b, layer):
    n = cvec.shape[0]
    m_pad = 16
    a = jnp.zeros((m_pad, D_MODEL), F32).at[:n].set(jax.nn.silu(cvec)).astype(BF16)
    n_cols = ada_w.shape[-1]
    tn = 1024
    (mod,) = _matmul(
        a, ada_w, w_lead=(layer,), n_cols=n_cols, tm=m_pad, tn=tn, epilogue=_ep_bias,
        extras=[(ada_b[:, None, :], (None, 1, tn), lambda j, i: (layer, 0, j))],
        outs=[(jax.ShapeDtypeStruct((m_pad, n_cols), F32), (m_pad, tn), lambda j, i: (i, j))],
        name="ada_mod")
    return mod[:n, None, :]


def _mla_project(h, n_ctx, seq, tm, mla_w_in, mla_q_norm_g, mla_kv_norm_g, mla_w_uq, mla_w_ukv, mla_q_g, mla_k_g):
    n_all = h.shape[0]
    n_lat = n_all - n_ctx
    w_in_pad = jnp.concatenate([mla_w_in[..., :Q_LORA + KV_LORA], _rope_tail_layout(mla_w_in[..., Q_LORA + KV_LORA:])],
                               axis=-1)
    c_q, c_kv, k_pe = _matmul(
        h, w_in_pad, w_lead=(0,), n_cols=MLA_IN_PAD, tm=tm, tn=MLA_IN_PAD, epilogue=_ep_mla_in,
        extras=[(mla_q_norm_g[0].reshape(1, Q_LORA), (1, Q_LORA), lambda j, i: (0, 0)),
                (mla_kv_norm_g[0].reshape(1, KV_LORA), (1, KV_LORA), lambda j, i: (0, 0))],
        outs=[(jax.ShapeDtypeStruct((n_all, Q_LORA), BF16), (tm, Q_LORA), lambda j, i: (i, 0)),
              (jax.ShapeDtypeStruct((n_all, KV_LORA), BF16), (tm, KV_LORA), lambda j, i: (i, 0)),
              (jax.ShapeDtypeStruct((n_all, LANE), F32), (tm, LANE), lambda j, i: (i, 0))],
        name="mla_in")

    rope_tab = _axial_rope_tables(seq, tm)
    n_ctx_tiles = n_ctx // tm
    seq_tiles = seq // tm
    pad_head = lambda a: jnp.concatenate([a[..., :QK_NOPE], _rope_tail_layout(a[..., QK_NOPE:])], axis=-1)
    pad_gain = lambda g, s: pad_head(g * s).reshape(1, QK_PAD)
    w_uq = pad_head(mla_w_uq[0].reshape(Q_LORA, MLA_HEADS, QK_DIM)).reshape(Q_LORA, MLA_HEADS * QK_PAD)
    tn_up = 1024
    (q,) = _matmul(
        c_q, w_uq, row0=n_ctx, n_cols=MLA_HEADS * QK_PAD, tm=tm, tn=tn_up, epilogue=_ep_mla_q,
        extras=[(pad_gain(mla_q_g[0], QK_DIM ** -0.5), (1, QK_PAD), lambda j, i: (0, 0)),
                (rope_tab, (tm, 2 * LANE), lambda j, i: (1 + i % seq_tiles, 0))],
        outs=[(jax.ShapeDtypeStruct((n_lat, MLA_HEADS * QK_PAD), BF16), (tm, tn_up), lambda j, i: (i, j))],
        name="mla_up_q")
    heads_per_tile = tn_up // (QK_NOPE + V_DIM)
    k, v = _matmul(
        c_kv, mla_w_ukv, w_lead=(0,), n_cols=MLA_HEADS * (QK_NOPE + V_DIM), tm=tm, tn=tn_up, epilogue=_ep_mla_kv,
        extras=[(pad_gain(mla_k_g[0], 1.0), (1, QK_PAD), lambda j, i: (0, 0)),
                (rope_tab, (tm, 2 * LANE),
                 lambda j, i: (jnp.where(i < n_ctx_tiles, 0, 1 + (i - n_ctx_tiles) % seq_tiles), 0)),
                (k_pe, (tm, LANE), lambda j, i: (i, 0))],
        outs=[(jax.ShapeDtypeStruct((n_all, MLA_HEADS * QK_PAD), BF16), (tm, heads_per_tile * QK_PAD),
               lambda j, i: (i, j)),
              (jax.ShapeDtypeStruct((n_all, MLA_HEADS * V_DIM), BF16), (tm, heads_per_tile * V_DIM),
               lambda j, i: (i, j))],
        name="mla_up_kv")
    return q, k, v


def kernel(x, c, ctx, c_ctx, ada_w, ada_b, norm1_g, norm2_g, ab_w_in, ab_w_out, na_q_g, na_k_g, na_rel_bias,
           sg_norm_g, sg_norm_b, sg_w, sg_b, mla_w_in, mla_q_norm_g, mla_kv_norm_g, mla_w_uq, mla_w_ukv,
           mla_q_g, mla_k_g, mla_w_out, moe_w_router, moe_b_router, moe_w_gu, moe_b_gu, moe_w_down, moe_b_down):
    n_batch, seq, d = x.shape
    t_ctx = ctx.shape[1]
    n_ctx = n_batch * t_ctx
    n_lat = n_batch * seq
    n_all = n_ctx + n_lat
    tm = MATMUL_TM
    tn = 512
    cvec = jnp.concatenate([c_ctx[None, :], c], axis=0)
    xs = (ctx.reshape(n_ctx, d), x.reshape(n_lat, d))
    mod0 = _ada_mod(cvec, ada_w, ada_b, 0)
    mod1 = _ada_mod(cvec, ada_w, ada_b, 1)

    h = _norm_mod(xs, norm1_g[0], mod0, 0, 1, n_ctx, seq)
    w_in = ab_w_in
    q_gain = (na_q_g[0] * (NA_HEAD_DIM ** -0.5)).reshape(1, NA_HEAD_DIM)
    k_gain = na_k_g[0].reshape(1, NA_HEAD_DIM)

    def in_proj(col0, n_cols, epilogue, extras, name):
        (o,) = _matmul(h, w_in, w_lead=(0,), col0=col0, n_cols=n_cols, tm=tm, tn=tn, epilogue=epilogue,
                       extras=extras,
                       outs=[(jax.ShapeDtypeStruct((n_all, n_cols), BF16), (tm, tn), lambda j, i: (i, j))],
                       name=name)
        return o

    gain_spec = lambda g: [(g, (1, NA_HEAD_DIM), lambda j, i: (0, 0))]
    q = in_proj(0, NA_WIDTH, _ep_head_rms, gain_spec(q_gain), "ab_in_q")
    k = in_proj(NA_WIDTH, NA_WIDTH, _ep_head_rms, gain_spec(k_gain), "ab_in_k")
    v = in_proj(2 * NA_WIDTH, NA_WIDTH, _ep_cast, [], "ab_in_v")
    uv = in_proj(3 * NA_WIDTH, 2 * SG_WIDTH, _ep_gelu, [], "ab_in_uv")

    a_lat = _na_attention(q, k, v, na_rel_bias[0], n_batch=n_batch, seq=seq, t_ctx=t_ctx)
    a_ctx = _attention(q, k, v, n_batch=n_batch, n_heads=NA_HEADS, dq=NA_HEAD_DIM, dv=NA_HEAD_DIM,
                       tq=t_ctx, n_q_tiles=1, q_blk0=0, t_k=t_ctx, k_blk0=0, name="ctx_attention")
    gated = _spatial_gating(uv, sg_norm_g[0], sg_norm_b[0], sg_w[0], sg_b[0])

    def out_proj(inp, n_rows, w, resid, resid_row0, mod, n_ctx_rows, name):
        mr = functools.partial(_mod_row, tm=tm, n_ctx_rows=n_ctx_rows, seq=seq)
        (o,) = _matmul(inp, w, w_lead=(0,), n_cols=d, tm=tm, tn=tn, epilogue=_ep_residual,
                       extras=[(resid, (tm, tn), lambda j, i: (resid_row0 // tm + i, j)),
                               (mod, (None, 1, tn), lambda j, i: (mr(i), 0, 2 * d // tn + j))],
                       outs=[(jax.ShapeDtypeStruct((n_rows, d), F32), (tm, tn), lambda j, i: (i, j))],
                       name=name)
        return o

    xs = out_proj([(a_ctx, a_lat), gated], n_all, ab_w_out, xs, 0, mod0, n_ctx, "ab_out")
    h2, logits = _norm_mod(xs, norm2_g[0], mod0, 3, 4, n_ctx, seq, router=(moe_w_router[0], moe_b_router[0]))
    xs, h = _moe_block(xs, h2, logits, mod0, 0, n_ctx, seq, moe_w_gu, moe_b_gu, moe_w_down, moe_b_down,
                       next_norm=(norm1_g[1], mod1))

    q, k, v = _mla_project(h, n_ctx, seq, tm, mla_w_in, mla_q_norm_g, mla_kv_norm_g, mla_w_uq, mla_w_ukv,
                           mla_q_g, mla_k_g)

    tq = seq
    attn = _attention(q, k, v, n_batch=n_batch, n_heads=MLA_HEADS, dq=QK_PAD, dv=V_DIM, tq=tq,
                      n_q_tiles=seq // tq, q_blk0=0, t_k=seq, k_blk0=n_ctx // seq, t_ctx=t_ctx, n_sub=8,
                      name="mla_attention")
    x_lat = out_proj(attn, n_lat, mla_w_out, xs, n_ctx, mod1, 0, "mla_out")
    h2, logits = _norm_mod(x_lat, norm2_g[1], mod1, 3, 4, 0, seq, router=(moe_w_router[1], moe_b_router[1]))
    x_lat = _moe_block(x_lat, h2, logits, mod1, 1, 0, seq, moe_w_gu, moe_b_gu, moe_w_down, moe_b_down)
    return x_lat.reshape(n_batch, seq, d)
```

```python
import functools

import numpy as np
import jax
import jax.numpy as jnp
from jax import lax
from jax.experimental import pallas as pl
from jax.experimental.pallas import tpu as pltpu

F32 = jnp.float32
BF16 = jnp.bfloat16

D_MODEL = 2048
GRID_W = 64
EPS = 1e-6
NEG = -1e30

NA_HEADS = 8
NA_HEAD_DIM = 128
NA_WIDTH = NA_HEADS * NA_HEAD_DIM
NA_WIN_H = 8
NA_WIN_W = 16
SG_GROUPS = 8
SG_WIDTH = D_MODEL // 2
SG_GROUP_DIM = SG_WIDTH // SG_GROUPS
SG_CHUNK = 128

MLA_HEADS = 16
Q_LORA = 512
KV_LORA = 256
QK_NOPE = 128
QK_ROPE = 64
V_DIM = 128
QK_DIM = QK_NOPE + QK_ROPE
QK_PAD = 256
MLA_IN_PAD = Q_LORA + KV_LORA + 128
ROPE_THETA = 10000.0

N_EXPERTS = 32
TOP_K = 4
D_FF = D_MODEL
SWIGLU_ALPHA = 1.702
SWIGLU_LIMIT = 7.0

LANE = 128
V7X_VMEM_LIMIT = 56 * 1024 * 1024

MATMUL_TM = 1024
MATMUL_SUB_ROWS = 256
NA_TILE_ROWS = 4
NA_BAND_ROWS = NA_TILE_ROWS + NA_WIN_H - 1
MOE_TM = 1024
MOE_PARTS = 4
MOE_TN_GU = 1024
MOE_TN_DOWN = 1024


def _params(n_axes):
    return pltpu.CompilerParams(dimension_semantics=("arbitrary",) * n_axes,
                                vmem_limit_bytes=V7X_VMEM_LIMIT)


def _norm_mod_val(x, g, shift, scale):
    y = x * lax.rsqrt(jnp.mean(x * x, axis=-1, keepdims=True) + EPS) * g
    return y * (1.0 + scale) + shift


def _norm_mod_body(x_ref, g_ref, sh_ref, sc_ref):
    return _norm_mod_val(x_ref[...], g_ref[...], sh_ref[...], sc_ref[...])


class _SplitRows:
    def __init__(self, ctx_ref, lat_ref, is_ctx):
        self.ctx_ref, self.lat_ref, self.is_ctx = ctx_ref, lat_ref, is_ctx
        self.shape, self.dtype = lat_ref.shape, lat_ref.dtype

    def __getitem__(self, key):
        return jnp.where(self.is_ctx, self.ctx_ref[key], self.lat_ref[key])


def _split_specs(block, index_map, n_ctx_tiles):
    def ctx_map(*ids):
        r, *rest = index_map(*ids)
        return (jnp.minimum(r, n_ctx_tiles - 1), *rest)

    def lat_map(*ids):
        r, *rest = index_map(*ids)
        return (jnp.maximum(r - n_ctx_tiles, 0), *rest)

    return [pl.BlockSpec(block, ctx_map), pl.BlockSpec(block, lat_map)]


def _norm_mod_kernel(x_ref, g_ref, sh_ref, sc_ref, o_ref):
    o_ref[...] = _norm_mod_body(x_ref, g_ref, sh_ref, sc_ref).astype(o_ref.dtype)


def _norm_mod_split_kernel(xc_ref, xl_ref, g_ref, sh_ref, sc_ref, o_ref, *, n_ctx_tiles):
    x_ref = _SplitRows(xc_ref, xl_ref, pl.program_id(0) < n_ctx_tiles)
    o_ref[...] = _norm_mod_body(x_ref, g_ref, sh_ref, sc_ref).astype(o_ref.dtype)


def _norm_mod_router_kernel(x_ref, g_ref, sh_ref, sc_ref, wcat_ref, br_ref, o_ref, lg_ref):
    h = _norm_mod_body(x_ref, g_ref, sh_ref, sc_ref)
    h_hi = h.astype(BF16)
    o_ref[...] = h_hi
    n_e = lg_ref.shape[1]
    h_lo = (h - h_hi.astype(F32)).astype(BF16)
    a = jnp.dot(h_hi, wcat_ref[...], preferred_element_type=F32)
    b = jnp.dot(h_lo, wcat_ref[:, :n_e], preferred_element_type=F32)
    lg_ref[...] = a[:, :n_e] + a[:, n_e:] + b + br_ref[...]


def _mod_row(i, tm, n_ctx_rows, seq):
    n_ctx_tiles = n_ctx_rows // tm
    return jnp.where(i < n_ctx_tiles, 0, 1 + (i - n_ctx_tiles) // (seq // tm))


def _norm_mod(x, gain, mod, k_shift, k_scale, n_ctx_rows, seq, router=None, tm=256):
    split = isinstance(x, tuple)
    rows = sum(a.shape[0] for a in x) if split else x.shape[0]
    d = gain.shape[0]
    mrow = functools.partial(_mod_row, tm=tm, n_ctx_rows=n_ctx_rows, seq=seq)
    x_specs = (_split_specs((tm, d), lambda i: (i, 0), n_ctx_rows // tm) if split
               else [pl.BlockSpec((tm, d), lambda i: (i, 0))])
    in_specs = x_specs + [
        pl.BlockSpec((1, d), lambda i: (0, 0)),
        pl.BlockSpec((None, 1, d), lambda i: (mrow(i), 0, k_shift)),
        pl.BlockSpec((None, 1, d), lambda i: (mrow(i), 0, k_scale)),
    ]
    args = [*x, gain.reshape(1, d), mod, mod] if split else [x, gain.reshape(1, d), mod, mod]
    if router is None:
        kern = (functools.partial(_norm_mod_split_kernel, n_ctx_tiles=n_ctx_rows // tm) if split
                else _norm_mod_kernel)
        return pl.pallas_call(
            kern, grid=(rows // tm,), in_specs=in_specs,
            out_specs=pl.BlockSpec((tm, d), lambda i: (i, 0)),
            out_shape=jax.ShapeDtypeStruct((rows, d), BF16),
            compiler_params=_params(1), name="norm_mod")(*args)
    assert not split
    w_r, b_r = router
    n_e = w_r.shape[1]
    w_hi = w_r.astype(BF16)
    w_cat = jnp.concatenate([w_hi, (w_r - w_hi.astype(F32)).astype(BF16)], axis=1)
    in_specs += [pl.BlockSpec((d, 2 * n_e), lambda i: (0, 0)), pl.BlockSpec((1, n_e), lambda i: (0, 0))]
    return pl.pallas_call(
        _norm_mod_router_kernel, grid=(rows // tm,), in_specs=in_specs,
        out_specs=[pl.BlockSpec((tm, d), lambda i: (i, 0)), pl.BlockSpec((tm, n_e), lambda i: (i, 0))],
        out_shape=[jax.ShapeDtypeStruct((rows, d), BF16), jax.ShapeDtypeStruct((rows, n_e), F32)],
        compiler_params=_params(1), name="norm_mod_router")(*args, w_cat, b_r.reshape(1, n_e))


def _matmul_kernel(*refs, x_split, extra_split, n_out, epilogue, n_sub, n_ctx_tiles):
    is_ctx = pl.program_id(1) < n_ctx_tiles
    refs = list(refs)

    def take(split):
        if split:
            return _SplitRows(refs.pop(0), refs.pop(0), is_ctx)
        return refs.pop(0)

    xs = [take(sp) for sp in x_split]
    w_ref = refs.pop(0)
    extra = [take(sp) for sp in extra_split]
    outs, wbf_ref = refs[:n_out], refs[-1]
    tm = outs[0].shape[0]

    @pl.when(pl.program_id(1) == 0)
    def _():
        wbf_ref[...] = w_ref[...].astype(BF16)

    sub = tm // n_sub

    def dot(r):
        acc, k0 = None, 0
        for x in xs:
            part = jnp.dot(x[r * sub:(r + 1) * sub, :], wbf_ref[k0:k0 + x.shape[1], :], preferred_element_type=F32)
            acc = part if acc is None else acc + part
            k0 += x.shape[1]
        return acc

    acc_next = dot(0)
    for r in range(n_sub):
        acc = acc_next
        if r + 1 < n_sub:
            acc_next = dot(r + 1)
        rows = slice(r * sub, (r + 1) * sub)
        epilogue(acc, [_RowView(e, rows) if e.shape[0] == tm else e for e in extra],
                 [_RowView(o, rows) for o in outs])


class _RowView:
    def __init__(self, ref, rows):
        self.ref, self.rows = ref, rows
        self.shape = (rows.stop - rows.start,) + tuple(ref.shape[1:])
        self.dtype = ref.dtype

    def _key(self, key):
        if key is Ellipsis:
            return (self.rows, slice(None))
        assert isinstance(key, tuple) and key[0] == slice(None), key
        return (self.rows,) + tuple(key[1:])

    def __getitem__(self, key):
        return self.ref[self._key(key)]

    def __setitem__(self, key, value):
        self.ref[self._key(key)] = value


def _matmul(x, w, *, w_lead=(), col0=0, n_cols, tm, tn, epilogue, extras=(), outs, name, row0=0, m=None):
    parts = x if isinstance(x, list) else [x]
    n_rows = lambda a: sum(b.shape[0] for b in a) if isinstance(a, tuple) else a.shape[0]
    width = lambda a: a[0].shape[1] if isinstance(a, tuple) else a.shape[1]
    splits = [a for a in parts + [e[0] for e in extras] if isinstance(a, tuple)]
    n_ctx_tiles = splits[0][0].shape[0] // tm if splits else 0
    assert all(a[0].shape[0] == n_ctx_tiles * tm for a in splits) and (row0 == 0 or not splits)
    k = sum(width(a) for a in parts)
    m = n_rows(parts[0]) - row0 if m is None else m
    assert m % tm == 0 and row0 % tm == 0 and n_cols % tn == 0 and col0 % tn == 0
    lead = tuple(w_lead)

    def specs(a, block, index_map):
        if isinstance(a, tuple):
            return _split_specs(block, index_map, n_ctx_tiles)
        return [pl.BlockSpec(block, index_map)]

    in_specs, args = [], []
    for a in parts:
        in_specs += specs(a, (tm, width(a)), lambda j, i: (row0 // tm + i, 0))
        args += list(a) if isinstance(a, tuple) else [a]
    in_specs.append(pl.BlockSpec((None,) * len(lead) + (k, tn), lambda j, i: lead + (0, col0 // tn + j)))
    args.append(w)
    for a, bs, im in extras:
        in_specs += specs(a, bs, im)
        args += list(a) if isinstance(a, tuple) else [a]
    out_specs = [pl.BlockSpec(bs, im) for _, bs, im in outs]
    kern = functools.partial(
        _matmul_kernel, x_split=tuple(isinstance(a, tuple) for a in parts),
        extra_split=tuple(isinstance(e[0], tuple) for e in extras), n_out=len(outs), epilogue=epilogue,
        n_sub=max(1, tm // MATMUL_SUB_ROWS), n_ctx_tiles=n_ctx_tiles)
    return pl.pallas_call(
        kern, grid=(n_cols // tn, m // tm), in_specs=in_specs, out_specs=out_specs,
        out_shape=[s for s, _, _ in outs],
        scratch_shapes=[pltpu.VMEM((k, tn), BF16)],
        compiler_params=_params(2), name=name)(*args)


def _ep_bias(acc, extra, outs):
    outs[0][...] = acc + extra[0][...]


def _ep_cast(acc, extra, outs):
    outs[0][...] = acc.astype(outs[0].dtype)


def _ep_gelu(acc, extra, outs):
    outs[0][...] = jax.nn.gelu(acc).astype(outs[0].dtype)


def _ep_head_rms(acc, extra, outs):
    g = extra[0][...]
    for h in range(acc.shape[1] // LANE):
        a = acc[:, h * LANE:(h + 1) * LANE]
        r = lax.rsqrt(jnp.mean(a * a, axis=-1, keepdims=True) + EPS)
        outs[0][:, h * LANE:(h + 1) * LANE] = (a * r * g).astype(outs[0].dtype)


def _ep_residual(acc, extra, outs):
    outs[0][...] = extra[0][...] + extra[1][...] * acc


def _rms(a, g):
    return a * lax.rsqrt(jnp.mean(a * a, axis=-1, keepdims=True) + EPS) * g


def _ep_mla_in(acc, extra, outs):
    outs[0][...] = _rms(acc[:, :Q_LORA], extra[0][...]).astype(BF16)
    outs[1][...] = _rms(acc[:, Q_LORA:Q_LORA + KV_LORA], extra[1][...]).astype(BF16)
    outs[2][...] = acc[:, Q_LORA + KV_LORA:]


def _rope_tail(t, tab_ref):
    return t * tab_ref[:, :LANE] + pltpu.roll(t, LANE // 2, axis=1) * tab_ref[:, LANE:]


def _ep_mla_q(acc, extra, outs):
    g_ref, tab_ref = extra
    g0 = g_ref[:, :LANE]
    g1 = g_ref[:, LANE:]
    for h in range(acc.shape[1] // QK_PAD):
        a0 = acc[:, h * QK_PAD:h * QK_PAD + LANE]
        a1 = acc[:, h * QK_PAD + LANE:(h + 1) * QK_PAD]
        r = lax.rsqrt(jnp.sum(a0 * a0 + a1 * a1, axis=-1, keepdims=True) * (1.0 / QK_DIM) + EPS)
        outs[0][:, h * QK_PAD:h * QK_PAD + LANE] = (a0 * r * g0).astype(BF16)
        outs[0][:, h * QK_PAD + LANE:(h + 1) * QK_PAD] = _rope_tail(a1 * r * g1, tab_ref).astype(BF16)


def _ep_mla_kv(acc, extra, outs):
    g_ref, tab_ref, pe_ref = extra
    k_out, v_out = outs
    g0 = g_ref[:, :LANE]
    pe = pe_ref[...]
    pe_sq = pe * pe
    pe_roped = _rope_tail(pe * g_ref[:, LANE:], tab_ref)
    for h in range(acc.shape[1] // (QK_NOPE + V_DIM)):
        base = h * (QK_NOPE + V_DIM)
        kn = acc[:, base:base + QK_NOPE]
        r = lax.rsqrt(jnp.sum(kn * kn + pe_sq, axis=-1, keepdims=True) * (1.0 / QK_DIM) + EPS)
        k_out[:, h * QK_PAD:h * QK_PAD + LANE] = (kn * r * g0).astype(BF16)
        k_out[:, h * QK_PAD + LANE:(h + 1) * QK_PAD] = (pe_roped * r).astype(BF16)
        v_out[:, h * V_DIM:(h + 1) * V_DIM] = acc[:, base + QK_NOPE:base + QK_NOPE + V_DIM].astype(BF16)


def _softmax_pv(s_parts, v_parts):
    m = s_parts[0].max(axis=-1, keepdims=True)
    for s in s_parts[1:]:
        m = jnp.maximum(m, s.max(axis=-1, keepdims=True))
    l = 0.0
    o = 0.0
    for s, v in zip(s_parts, v_parts):
        p = jnp.exp(s - m)
        l = l + p.sum(axis=-1, keepdims=True)
        o = o + jnp.dot(p.astype(BF16), v, preferred_element_type=F32)
    return o / l


def _qk(q, k):
    return lax.dot_general(q, k, (((1,), (1,)), ((), ())), preferred_element_type=F32)


def _attn_kernel(*refs, has_ctx, n_sub):
    if has_ctx:
        q_ref, k_ref, v_ref, kc_ref, vc_ref, o_ref = refs
    else:
        q_ref, k_ref, v_ref, o_ref = refs
    sub = q_ref.shape[0] // n_sub

    def scores(r):
        q = q_ref[r * sub:(r + 1) * sub, :]
        parts = [_qk(q, k_ref[...])]
        if has_ctx:
            parts.append(_qk(q, kc_ref[...]))
        return parts

    s_next = scores(0)
    for r in range(n_sub):
        s_cur = s_next
        if r + 1 < n_sub:
            s_next = scores(r + 1)
        v_parts = [v_ref[...]] + ([vc_ref[...]] if has_ctx else [])
        o_ref[r * sub:(r + 1) * sub, :] = _softmax_pv(s_cur, v_parts).astype(o_ref.dtype)


def _attention(q, k, v, *, n_batch, n_heads, dq, dv, tq, n_q_tiles, q_blk0, t_k, k_blk0, t_ctx=None, n_sub=1, name):
    has_ctx = t_ctx is not None
    in_specs = [
        pl.BlockSpec((tq, dq), lambda b, h, i: (q_blk0 + b * n_q_tiles + i, h)),
        pl.BlockSpec((t_k, dq), lambda b, h, i: (k_blk0 + b, h)),
        pl.BlockSpec((t_k, dv), lambda b, h, i: (k_blk0 + b, h)),
    ]
    args = [q, k, v]
    if has_ctx:
        in_specs += [pl.BlockSpec((t_ctx, dq), lambda b, h, i: (b, h)),
                     pl.BlockSpec((t_ctx, dv), lambda b, h, i: (b, h))]
        args += [k, v]
    return pl.pallas_call(
        functools.partial(_attn_kernel, has_ctx=has_ctx, n_sub=n_sub),
        grid=(n_batch, n_heads, n_q_tiles), in_specs=in_specs,
        out_specs=pl.BlockSpec((tq, dv), lambda b, h, i: (b * n_q_tiles + i, h)),
        out_shape=jax.ShapeDtypeStruct((n_batch * n_q_tiles * tq, n_heads * dv), BF16),
        compiler_params=_params(3), name=name)(*args)


def _na_band_start(t, rows):
    return jnp.clip(t * NA_TILE_ROWS - NA_WIN_H // 2, 0, rows - NA_BAND_ROWS)


def _na_kernel(q_ref, k_ref, v_ref, kc_ref, vc_ref, bias_ref, o_ref, *, rows):
    t = pl.program_id(1)
    ks = pl.multiple_of(_na_band_start(t, rows) * GRID_W, GRID_W)
    band = pl.ds(ks, NA_BAND_ROWS * GRID_W)

    def scores(h):
        cols = slice(h * NA_HEAD_DIM, (h + 1) * NA_HEAD_DIM)
        q = q_ref[:, cols]
        return [_qk(q, k_ref[band, cols]) + bias_ref[h], _qk(q, kc_ref[:, cols])]

    s_next = scores(0)
    for h in range(NA_HEADS):
        s_cur = s_next
        if h + 1 < NA_HEADS:
            s_next = scores(h + 1)
        cols = slice(h * NA_HEAD_DIM, (h + 1) * NA_HEAD_DIM)
        o_ref[:, cols] = _softmax_pv(s_cur, [v_ref[band, cols], vc_ref[:, cols]]).astype(o_ref.dtype)


def _na_bias_tables(rel_bias, rows):
    n_tiles = rows // NA_TILE_ROWS
    n_r, n_c = 2 * NA_WIN_H - 1, 2 * NA_WIN_W - 1
    qr = np.arange(NA_TILE_ROWS)[:, None]
    ur = np.arange(NA_BAND_ROWS)[None, :]
    qc = np.arange(GRID_W)[:, None]
    kc = np.arange(GRID_W)[None, :]
    c0 = np.clip(qc - NA_WIN_W // 2, 0, GRID_W - NA_WIN_W)
    col_valid = (kc >= c0) & (kc < c0 + NA_WIN_W)
    col_onehot = np.eye(n_c, dtype=np.float32)[np.clip(kc - qc + NA_WIN_W - 1, 0, n_c - 1)]
    geoms, ids = [], []
    for t in range(n_tiles):
        u0 = int(np.clip(t * NA_TILE_ROWS - NA_WIN_H // 2, 0, rows - NA_BAND_ROWS))
        r = t * NA_TILE_ROWS + qr
        key_row = u0 + ur
        r0 = np.clip(r - NA_WIN_H // 2, 0, rows - NA_WIN_H)
        geom = ((key_row >= r0) & (key_row < r0 + NA_WIN_H), np.clip(key_row - r + NA_WIN_H - 1, 0, n_r - 1))
        for gi, g in enumerate(geoms):
            if all(np.array_equal(a, b) for a, b in zip(g, geom)):
                ids.append(gi)
                break
        else:
            ids.append(len(geoms))
            geoms.append(geom)
    row_valid = np.stack([g[0] for g in geoms])
    row_onehot = np.eye(n_r, dtype=np.float32)[np.stack([g[1] for g in geoms])]
    bias = jnp.einsum('gaur,hrc,bkc->ghabuk', row_onehot, rel_bias.astype(F32), col_onehot,
                      precision=lax.Precision.HIGHEST)
    valid = row_valid[:, None, :, None, :, None] & col_valid[None, None, None, :, None, :]
    tables = jnp.where(valid, bias, NEG)
    return tables.reshape(len(geoms), NA_HEADS, NA_TILE_ROWS * GRID_W, NA_BAND_ROWS * GRID_W), ids


def _na_attention(q, k, v, rel_bias, *, n_batch, seq, t_ctx):
    rows = seq // GRID_W
    n_tiles = rows // NA_TILE_ROWS
    tq = NA_TILE_ROWS * GRID_W
    tables, ids = _na_bias_tables(rel_bias, rows)
    assert ids == [0] + [1] * (n_tiles - 2) + [2], ids
    ctx_tiles = n_batch * t_ctx // tq
    ctx_units = n_batch * t_ctx // seq

    def table_id(t):
        return jnp.where(t == 0, 0, jnp.where(t == n_tiles - 1, 2, 1))

    return pl.pallas_call(
        functools.partial(_na_kernel, rows=rows),
        grid=(n_batch, n_tiles),
        in_specs=[
            pl.BlockSpec((tq, NA_WIDTH), lambda b, t: (ctx_tiles + b * n_tiles + t, 0)),
            pl.BlockSpec((seq, NA_WIDTH), lambda b, t: (ctx_units + b, 0)),
            pl.BlockSpec((seq, NA_WIDTH), lambda b, t: (ctx_units + b, 0)),
            pl.BlockSpec((t_ctx, NA_WIDTH), lambda b, t: (b, 0)),
            pl.BlockSpec((t_ctx, NA_WIDTH), lambda b, t: (b, 0)),
            pl.BlockSpec((None, NA_HEADS, tq, NA_BAND_ROWS * GRID_W), lambda b, t: (table_id(t), 0, 0, 0)),
        ],
        out_specs=pl.BlockSpec((tq, NA_WIDTH), lambda b, t: (b * n_tiles + t, 0)),
        out_shape=jax.ShapeDtypeStruct((n_batch * seq, NA_WIDTH), BF16),
        compiler_params=_params(2), name="na_attention")(q, k, v, k, v, tables)


def _sg_kernel(uv_ref, g_ref, b_ref, ws_ref, bs_ref, o_ref, *, chunks):
    for c in range(chunks):
        r = slice(c * SG_CHUNK, (c + 1) * SG_CHUNK)
        z = uv_ref[r, SG_WIDTH:].astype(F32)
        mu = jnp.mean(z, axis=-1, keepdims=True)
        zc = z - mu
        var = jnp.mean(zc * zc, axis=-1, keepdims=True)
        zn = (zc * lax.rsqrt(var + EPS) * g_ref[...] + b_ref[...]).astype(BF16)
        for g in range(SG_GROUPS):
            cols = slice(g * SG_GROUP_DIM, (g + 1) * SG_GROUP_DIM)
            mixed = jnp.dot(ws_ref[g], zn[:, cols], preferred_element_type=F32) + bs_ref[g]
            o_ref[r, cols] = (uv_ref[r, cols].astype(F32) * mixed).astype(o_ref.dtype)


def _spatial_gating(uv, ln_g, ln_b, w_s, b_s, tm=512):
    rows = uv.shape[0]
    bs = jnp.broadcast_to(b_s.astype(F32)[:, :, None], (SG_GROUPS, SG_CHUNK, SG_GROUP_DIM))
    return pl.pallas_call(
        functools.partial(_sg_kernel, chunks=tm // SG_CHUNK),
        grid=(rows // tm,),
        in_specs=[
            pl.BlockSpec((tm, 2 * SG_WIDTH), lambda i: (i, 0)),
            pl.BlockSpec((1, SG_WIDTH), lambda i: (0, 0)),
            pl.BlockSpec((1, SG_WIDTH), lambda i: (0, 0)),
            pl.BlockSpec((SG_GROUPS, SG_CHUNK, SG_CHUNK), lambda i: (0, 0, 0)),
            pl.BlockSpec((SG_GROUPS, SG_CHUNK, SG_GROUP_DIM), lambda i: (0, 0, 0)),
        ],
        out_specs=pl.BlockSpec((tm, SG_WIDTH), lambda i: (i, 0)),
        out_shape=jax.ShapeDtypeStruct((rows, SG_WIDTH), BF16),
        compiler_params=_params(1), name="spatial_gating")(
            uv, ln_g.reshape(1, SG_WIDTH), ln_b.reshape(1, SG_WIDTH), w_s.astype(BF16), bs)


def _swiglu(g, u):
    g = jnp.minimum(g, SWIGLU_LIMIT)
    u = jnp.clip(u, -SWIGLU_LIMIT, SWIGLU_LIMIT)
    return (u + 1.0) * (g * jax.nn.sigmoid(SWIGLU_ALPHA * g))


def _for_each_valid_part(i, tv_ref, o_ref, compute):
    part = o_ref.shape[0] // MOE_PARTS
    n_valid = (tv_ref[i] + part - 1) // part
    rows = [slice(p * part, (p + 1) * part) for p in range(MOE_PARTS)]
    for n in range(MOE_PARTS + 1):

        @pl.when(n_valid == n)
        def _(n=n):
            acc_next = compute.matmul(rows[0]) if n else None
            for p in range(n):
                acc = acc_next
                if p + 1 < n:
                    acc_next = compute.matmul(rows[p + 1])
                o_ref[rows[p], :] = compute.epilogue(acc).astype(o_ref.dtype)
            for p in range(n, MOE_PARTS):
                o_ref[rows[p], :] = jnp.zeros((part, o_ref.shape[1]), o_ref.dtype)


class _GateUp:
    def __init__(self, x_ref, w_bf, bg_ref, bu_ref):
        self.x_ref, self.w_bf, self.bg_ref, self.bu_ref = x_ref, w_bf, bg_ref, bu_ref

    def matmul(self, rows):
        return jnp.dot(self.x_ref[rows, :], self.w_bf[...], preferred_element_type=F32)

    def epilogue(self, acc):
        tn = acc.shape[1] // 2
        return _swiglu(acc[:, :tn] + self.bg_ref[...], acc[:, tn:] + self.bu_ref[...])


class _Down:
    def __init__(self, h_ref, w_bf, b_ref):
        self.h_ref, self.w_bf, self.b_ref = h_ref, w_bf, b_ref

    def matmul(self, rows):
        return jnp.dot(self.h_ref[rows, :], self.w_bf[...], preferred_element_type=F32)

    def epilogue(self, acc):
        return acc + self.b_ref[...]


def _expert_weight_copies(w_hbm, land, sem, layer, expert, cols):
    tn = land.shape[1] // len(cols)
    return [pltpu.make_async_copy(w_hbm.at[layer, expert, :, pl.ds(pl.multiple_of(col, tn), tn)],
                                  land.at[:, pl.ds(k * tn, tn)], sem.at[k])
            for k, col in enumerate(cols)]


def _cast_rows(src, dst, chunk=256):
    def body(c, carry):
        rows = pl.ds(pl.multiple_of(c * chunk, chunk), chunk)
        dst[rows, :] = src[rows, :].astype(dst.dtype)
        return carry
    lax.fori_loop(0, src.shape[0] // chunk, body, 0)


def _refresh_expert_weights(te_ref, first_ref, nxt_ref, w_hbm, land, bf, sem, layer, col_fn, n_j):
    j = pl.program_id(0)
    i = pl.program_id(1)

    @pl.when(first_ref[i] == 1)
    def _():
        cur = _expert_weight_copies(w_hbm, land, sem, layer, te_ref[i], col_fn(j))

        @pl.when((i == 0) & (j == 0))
        def _():
            for c in cur:
                c.start()

        for c in cur:
            c.wait()
        _cast_rows(land, bf)
        nxt = nxt_ref[i]

        @pl.when(nxt >= 0)
        def _():
            for c in _expert_weight_copies(w_hbm, land, sem, layer, nxt, col_fn(j)):
                c.start()

        @pl.when((nxt < 0) & (j + 1 < n_j))
        def _():
            for c in _expert_weight_copies(w_hbm, land, sem, layer, te_ref[0], col_fn(j + 1)):
                c.start()


def _moe_gu_kernel(te_ref, tv_ref, ts_ref, first_ref, nxt_ref, x_ref, w_hbm, bg_ref, bu_ref, o_ref, w_land, w_bf, sem,
                   *, layer, n_j):
    tn = w_bf.shape[1] // 2
    _refresh_expert_weights(te_ref, first_ref, nxt_ref, w_hbm, w_land, w_bf, sem, layer,
                            lambda j: (j * tn, D_FF + j * tn), n_j)
    _for_each_valid_part(pl.program_id(1), tv_ref, o_ref, _GateUp(x_ref, w_bf, bg_ref, bu_ref))


def _moe_down_kernel(te_ref, tv_ref, ts_ref, first_ref, nxt_ref, ha_ref, hb_ref, w_hbm, b_ref, o_ref, w_land, w_bf, sem,
                     *, layer, n_j, n_first_half):
    tn = w_bf.shape[1]
    _refresh_expert_weights(te_ref, first_ref, nxt_ref, w_hbm, w_land, w_bf, sem, layer,
                            lambda j: (j * tn,), n_j)
    i = pl.program_id(1)
    h_ref = _SplitRows(ha_ref, hb_ref, ts_ref[i] < n_first_half)
    _for_each_valid_part(i, tv_ref, o_ref, _Down(h_ref, w_bf, b_ref))


def _expert_runs(tile_e):
    n_tiles = tile_e.shape[0]
    idx = jnp.arange(n_tiles, dtype=jnp.int32)
    first = jnp.concatenate([jnp.ones((1,), bool), tile_e[1:] != tile_e[:-1]])
    later_first = lax.cummin(jnp.where(first, idx, n_tiles)[::-1])[::-1]
    next_first = jnp.concatenate([later_first[1:], jnp.full((1,), n_tiles, jnp.int32)])
    nxt = jnp.where(next_first < n_tiles, tile_e[jnp.minimum(next_first, n_tiles - 1)], -1)
    return first.astype(jnp.int32), nxt.astype(jnp.int32)


def _moe_experts(x_halves, tile_e, tile_valid, tile_src, layer, w_gu, b_gu, w_down, b_down):
    tm, tn, tn_down = MOE_TM, MOE_TN_GU, MOE_TN_DOWN
    d = x_halves[0].shape[1]
    n_a = x_halves[0].shape[0] // tm
    n_tiles = n_a + x_halves[1].shape[0] // tm
    n_j = D_FF // tn
    n_jd = d // tn_down
    x_map = lambda j, i, te, tv, ts, fr, nx: (ts[i], 0)
    out_map = lambda j, i, te, tv, ts, fr, nx: (i, j)

    hid = []
    for x_half, t0, t1 in ((x_halves[0], 0, n_a), (x_halves[1], n_a, n_tiles)):
        te = tile_e[t0:t1]
        src = jnp.clip(tile_src[t0:t1] - t0, 0, t1 - t0 - 1)
        hid.append(pl.pallas_call(
            functools.partial(_moe_gu_kernel, layer=layer, n_j=n_j),
            grid_spec=pltpu.PrefetchScalarGridSpec(
                num_scalar_prefetch=5, grid=(n_j, t1 - t0),
                in_specs=[
                    pl.BlockSpec((tm, d), x_map),
                    pl.BlockSpec(memory_space=pl.ANY),
                    pl.BlockSpec((None, None, 1, tn), lambda j, i, te, tv, ts, fr, nx: (layer, te[i], 0, j)),
                    pl.BlockSpec((None, None, 1, tn), lambda j, i, te, tv, ts, fr, nx: (layer, te[i], 0, n_j + j)),
                ],
                out_specs=pl.BlockSpec((tm, tn), out_map),
                scratch_shapes=[pltpu.VMEM((d, 2 * tn), F32), pltpu.VMEM((d, 2 * tn), BF16),
                                pltpu.SemaphoreType.DMA((2,))]),
            out_shape=jax.ShapeDtypeStruct(((t1 - t0) * tm, D_FF), BF16),
            compiler_params=_params(2), name="moe_gate_up")(
                te, tile_valid[t0:t1], src, *_expert_runs(te), x_half, w_gu, b_gu[:, :, None, :], b_gu[:, :, None, :]))

    return pl.pallas_call(
        functools.partial(_moe_down_kernel, layer=layer, n_j=n_jd, n_first_half=n_a),
        grid_spec=pltpu.PrefetchScalarGridSpec(
            num_scalar_prefetch=5, grid=(n_jd, n_tiles),
            in_specs=[
                pl.BlockSpec((tm, D_FF), lambda j, i, te, tv, ts, fr, nx: (jnp.minimum(ts[i], n_a - 1), 0)),
                pl.BlockSpec((tm, D_FF), lambda j, i, te, tv, ts, fr, nx: (jnp.maximum(ts[i] - n_a, 0), 0)),
                pl.BlockSpec(memory_space=pl.ANY),
                pl.BlockSpec((None, None, 1, tn_down), lambda j, i, te, tv, ts, fr, nx: (layer, te[i], 0, j)),
            ],
            out_specs=pl.BlockSpec((tm, tn_down), out_map),
            scratch_shapes=[pltpu.VMEM((D_FF, tn_down), F32), pltpu.VMEM((D_FF, tn_down), BF16),
                            pltpu.SemaphoreType.DMA((1,))]),
        out_shape=jax.ShapeDtypeStruct((n_tiles * tm, d), BF16),
        compiler_params=_params(2), name="moe_down")(
            tile_e, tile_valid, tile_src, *_expert_runs(tile_e), hid[0], hid[1], w_down, b_down[:, :, None, :])


def _moe_route(logits):
    n = logits.shape[0]
    nk = n * TOP_K
    tm = MOE_TM
    i32 = jnp.int32
    top_val, top_idx = lax.top_k(logits, TOP_K)
    gate = jax.nn.softmax(top_val, axis=-1)
    flat_e = top_idx.reshape(nk).astype(i32)
    iota = jnp.arange(nk, dtype=i32)
    bits = (nk - 1).bit_length()
    assert (N_EXPERTS << bits) < 2 ** 31
    order = lax.sort((flat_e << bits) | iota) & ((1 << bits) - 1)
    _, inv_order = lax.sort((order, iota), num_keys=1)
    onehot = flat_e[:, None] == jnp.arange(N_EXPERTS, dtype=i32)[None, :]
    counts = jnp.sum(onehot, axis=0, dtype=i32)
    padded = (counts + tm - 1) // tm * tm
    pad_end = jnp.cumsum(padded)
    pad_start = pad_end - padded
    start = jnp.cumsum(counts) - counts
    pos = inv_order + jnp.sum(jnp.where(onehot, (pad_start - start)[None, :], 0), axis=1)
    pos = pos.reshape(n, TOP_K).T.reshape(nk)

    n_tiles = nk // tm + N_EXPERTS
    n_used = pad_end[-1] // tm
    tile_src = jnp.minimum(jnp.arange(n_tiles, dtype=i32), n_used - 1)
    tile_e = jnp.minimum(jnp.sum(pad_end[None, :] <= (tile_src * tm)[:, None], axis=1, dtype=i32), N_EXPERTS - 1)
    in_use = jnp.arange(n_tiles, dtype=i32) < n_used
    rank0 = tile_src * tm - pad_start[tile_e]
    tile_valid = jnp.where(in_use, jnp.clip(counts[tile_e] - rank0, 0, tm), 0).astype(i32)
    within = jnp.arange(tm, dtype=i32)[None, :]
    valid = within < tile_valid[:, None]
    src = jnp.clip((start[tile_e] + rank0)[:, None] + within, 0, nk - 1)
    filler = (jnp.arange(n_tiles * tm, dtype=i32) % n).reshape(n_tiles, tm)
    rows_tok = jnp.where(valid, order.at[src].get(mode="promise_in_bounds") // TOP_K, filler).reshape(n_tiles * tm)
    return gate, rows_tok, pos, tile_e, tile_valid, tile_src.astype(i32)


def _moe_combine(x_ref, g2_ref, gate_ref, y_ref):
    gate = gate_ref[...]
    f = gate[:, 0:1] * y_ref[0].astype(F32)
    for k in range(1, TOP_K):
        f = f + gate[:, k:k + 1] * y_ref[k].astype(F32)
    return x_ref[...] + g2_ref[...] * f


def _moe_combine_kernel(x_ref, g2_ref, gate_ref, y_ref, o_ref):
    o_ref[...] = _moe_combine(x_ref, g2_ref, gate_ref, y_ref)


def _moe_combine_norm_kernel(x_ref, g2_ref, gate_ref, y_ref, ng_ref, nsh_ref, nsc_ref, o_ref, h_ref):
    x = _moe_combine(x_ref, g2_ref, gate_ref, y_ref)
    o_ref[...] = x
    h_ref[...] = _norm_mod_val(x, ng_ref[...], nsh_ref[...], nsc_ref[...]).astype(h_ref.dtype)


def _moe_block(xs, h2, logits, mod, layer, n_ctx_rows, seq, w_gu, b_gu, w_down, b_down, next_norm=None, tm=256):
    n, d = h2.shape
    gate, rows_tok, pos, tile_e, tile_valid, tile_src = _moe_route(logits)
    half = tile_e.shape[0] // 2 * MOE_TM
    x_halves = [h2.at[r].get(mode="promise_in_bounds") for r in (rows_tok[:half], rows_tok[half:])]
    y = _moe_experts(x_halves, tile_e, tile_valid, tile_src, layer, w_gu, b_gu, w_down, b_down)
    y_k = y.at[pos].get(mode="promise_in_bounds").reshape(TOP_K, n, d)
    mrow = functools.partial(_mod_row, tm=tm, n_ctx_rows=n_ctx_rows, seq=seq)
    row_spec = pl.BlockSpec((tm, d), lambda i: (i, 0))
    mod_spec = lambda k: pl.BlockSpec((None, 1, d), lambda i: (mrow(i), 0, k))
    in_specs = [row_spec, mod_spec(5), pl.BlockSpec((tm, TOP_K), lambda i: (i, 0)),
                pl.BlockSpec((TOP_K, tm, d), lambda i: (0, i, 0))]
    if next_norm is None:
        return pl.pallas_call(
            _moe_combine_kernel, grid=(n // tm,), in_specs=in_specs, out_specs=row_spec,
            out_shape=jax.ShapeDtypeStruct((n, d), F32),
            compiler_params=_params(1), name="moe_combine")(xs, mod, gate, y_k)
    gain, next_mod = next_norm
    return pl.pallas_call(
        _moe_combine_norm_kernel, grid=(n // tm,),
        in_specs=in_specs + [pl.BlockSpec((1, d), lambda i: (0, 0)), mod_spec(0), mod_spec(1)],
        out_specs=[row_spec, row_spec],
        out_shape=[jax.ShapeDtypeStruct((n, d), F32), jax.ShapeDtypeStruct((n, d), BF16)],
        compiler_params=_params(1), name="moe_combine_norm")(
            xs, mod, gate, y_k, gain.reshape(1, d), next_mod, next_mod)


def _rope_tail_layout(x):
    half = QK_ROPE // 2
    z = jnp.zeros(x.shape[:-1] + (LANE // 2 - half,), x.dtype)
    return jnp.concatenate([x[..., :half], z, x[..., half:], z], axis=-1)


def _axial_rope_tables(seq, tm):
    t = jnp.arange(seq)
    row = (t // GRID_W).astype(F32)
    col = (t % GRID_W).astype(F32)
    n_freq = QK_ROPE // 4
    inv = ROPE_THETA ** (-jnp.arange(n_freq, dtype=F32) / n_freq)
    ang = jnp.concatenate([row[:, None] * inv, col[:, None] * inv], axis=-1)
    cos, sin = jnp.cos(ang), jnp.sin(ang)
    tab = jnp.concatenate([_rope_tail_layout(jnp.concatenate([cos, cos], axis=-1)),
                           _rope_tail_layout(jnp.concatenate([-sin, sin], axis=-1))], axis=-1)
    ident = jnp.concatenate([_rope_tail_layout(jnp.ones((tm, QK_ROPE), F32)), jnp.zeros((tm, LANE), F32)], axis=-1)
    return jnp.concatenate([ident, tab], axis=0)


def _ada_mod(cvec, ada_w, ada_b, layer):
    n = cvec.shape[0]
    m_pad = 16
    a = jnp.zeros((m_pad, D_MODEL), F32).at[:n].set(jax.nn.silu(cvec)).astype(BF16)
    n_cols = ada_w.shape[-1]
    tn = 1024
    (mod,) = _matmul(
        a, ada_w, w_lead=(layer,), n_cols=n_cols, tm=m_pad, tn=tn, epilogue=_ep_bias,
        extras=[(ada_b[:, None, :], (None, 1, tn), lambda j, i: (layer, 0, j))],
        outs=[(jax.ShapeDtypeStruct((m_pad, n_cols), F32), (m_pad, tn), lambda j, i: (i, j))],
        name="ada_mod")
    return mod[:n, None, :]


def _mla_project(h, n_ctx, seq, tm, mla_w_in, mla_q_norm_g, mla_kv_norm_g, mla_w_uq, mla_w_ukv, mla_q_g, mla_k_g):
    n_all = h.shape[0]
    n_lat = n_all - n_ctx
    w_in_pad = jnp.concatenate([mla_w_in[..., :Q_LORA + KV_LORA], _rope_tail_layout(mla_w_in[..., Q_LORA + KV_LORA:])],
                               axis=-1)
    c_q, c_kv, k_pe = _matmul(
        h, w_in_pad, w_lead=(0,), n_cols=MLA_IN_PAD, tm=tm, tn=MLA_IN_PAD, epilogue=_ep_mla_in,
        extras=[(mla_q_norm_g[0].reshape(1, Q_LORA), (1, Q_LORA), lambda j, i: (0, 0)),
                (mla_kv_norm_g[0].reshape(1, KV_LORA), (1, KV_LORA), lambda j, i: (0, 0))],
        outs=[(jax.ShapeDtypeStruct((n_all, Q_LORA), BF16), (tm, Q_LORA), lambda j, i: (i, 0)),
              (jax.ShapeDtypeStruct((n_all, KV_LORA), BF16), (tm, KV_LORA), lambda j, i: (i, 0)),
              (jax.ShapeDtypeStruct((n_all, LANE), F32), (tm, LANE), lambda j, i: (i, 0))],
        name="mla_in")

    rope_tab = _axial_rope_tables(seq, tm)
    n_ctx_tiles = n_ctx // tm
    seq_tiles = seq // tm
    pad_head = lambda a: jnp.concatenate([a[..., :QK_NOPE], _rope_tail_layout(a[..., QK_NOPE:])], axis=-1)
    pad_gain = lambda g, s: pad_head(g * s).reshape(1, QK_PAD)
    w_uq = pad_head(mla_w_uq[0].reshape(Q_LORA, MLA_HEADS, QK_DIM)).reshape(Q_LORA, MLA_HEADS * QK_PAD)
    tn_up = 1024
    (q,) = _matmul(
        c_q, w_uq, row0=n_ctx, n_cols=MLA_HEADS * QK_PAD, tm=tm, tn=tn_up, epilogue=_ep_mla_q,
        extras=[(pad_gain(mla_q_g[0], QK_DIM ** -0.5), (1, QK_PAD), lambda j, i: (0, 0)),
                (rope_tab, (tm, 2 * LANE), lambda j, i: (1 + i % seq_tiles, 0))],
        outs=[(jax.ShapeDtypeStruct((n_lat, MLA_HEADS * QK_PAD), BF16), (tm, tn_up), lambda j, i: (i, j))],
        name="mla_up_q")
    heads_per_tile = tn_up // (QK_NOPE + V_DIM)
    k, v = _matmul(
        c_kv, mla_w_ukv, w_lead=(0,), n_cols=MLA_HEADS * (QK_NOPE + V_DIM), tm=tm, tn=tn_up, epilogue=_ep_mla_kv,
        extras=[(pad_gain(mla_k_g[0], 1.0), (1, QK_PAD), lambda j, i: (0, 0)),
                (rope_tab, (tm, 2 * LANE),
                 lambda j, i: (jnp.where(i < n_ctx_tiles, 0, 1 + (i - n_ctx_tiles) % seq_tiles), 0)),
                (k_pe, (tm, LANE), lambda j, i: (i, 0))],
        outs=[(jax.ShapeDtypeStruct((n_all, MLA_HEADS * QK_PAD), BF16), (tm, heads_per_tile * QK_PAD),
               lambda j, i: (i, j)),
              (jax.ShapeDtypeStruct((n_all, MLA_HEADS * V_DIM), BF16), (tm, heads_per_tile * V_DIM),
               lambda j, i: (i, j))],
        name="mla_up_kv")
    return q, k, v


def kernel(x, c, ctx, c_ctx, ada_w, ada_b, norm1_g, norm2_g, ab_w_in, ab_w_out, na_q_g, na_k_g, na_rel_bias,
           sg_norm_g, sg_norm_b, sg_w, sg_b, mla_w_in, mla_q_norm_g, mla_kv_norm_g, mla_w_uq, mla_w_ukv,
           mla_q_g, mla_k_g, mla_w_out, moe_w_router, moe_b_router, moe_w_gu, moe_b_gu, moe_w_down, moe_b_down):
    n_batch, seq, d = x.shape
    t_ctx = ctx.shape[1]
    n_ctx = n_batch * t_ctx
    n_lat = n_batch * seq
    n_all = n_ctx + n_lat
    tm = MATMUL_TM
    tn = 512
    cvec = jnp.concatenate([c_ctx[None, :], c], axis=0)
    xs = (ctx.reshape(n_ctx, d), x.reshape(n_lat, d))
    mod0 = _ada_mod(cvec, ada_w, ada_b, 0)
    mod1 = _ada_mod(cvec, ada_w, ada_b, 1)

    h = _norm_mod(xs, norm1_g[0], mod0, 0, 1, n_ctx, seq)
    w_in = ab_w_in
    q_gain = (na_q_g[0] * (NA_HEAD_DIM ** -0.5)).reshape(1, NA_HEAD_DIM)
    k_gain = na_k_g[0].reshape(1, NA_HEAD_DIM)

    def in_proj(col0, n_cols, epilogue, extras, name):
        (o,) = _matmul(h, w_in, w_lead=(0,), col0=col0, n_cols=n_cols, tm=tm, tn=tn, epilogue=epilogue,
                       extras=extras,
                       outs=[(jax.ShapeDtypeStruct((n_all, n_cols), BF16), (tm, tn), lambda j, i: (i, j))],
                       name=name)
        return o

    gain_spec = lambda g: [(g, (1, NA_HEAD_DIM), lambda j, i: (0, 0))]
    q = in_proj(0, NA_WIDTH, _ep_head_rms, gain_spec(q_gain), "ab_in_q")
    k = in_proj(NA_WIDTH, NA_WIDTH, _ep_head_rms, gain_spec(k_gain), "ab_in_k")
    v = in_proj(2 * NA_WIDTH, NA_WIDTH, _ep_cast, [], "ab_in_v")
    uv = in_proj(3 * NA_WIDTH, 2 * SG_WIDTH, _ep_gelu, [], "ab_in_uv")

    a_lat = _na_attention(q, k, v, na_rel_bias[0], n_batch=n_batch, seq=seq, t_ctx=t_ctx)
    a_ctx = _attention(q, k, v, n_batch=n_batch, n_heads=NA_HEADS, dq=NA_HEAD_DIM, dv=NA_HEAD_DIM,
                       tq=t_ctx, n_q_tiles=1, q_blk0=0, t_k=t_ctx, k_blk0=0, name="ctx_attention")
    gated = _spatial_gating(uv, sg_norm_g[0], sg_norm_b[0], sg_w[0], sg_b[0])

    def out_proj(inp, n_rows, w, resid, resid_row0, mod, n_ctx_rows, name):
        mr = functools.partial(_mod_row, tm=tm, n_ctx_rows=n_ctx_rows, seq=seq)
        (o,) = _matmul(inp, w, w_lead=(0,), n_cols=d, tm=tm, tn=tn, epilogue=_ep_residual,
                       extras=[(resid, (tm, tn), lambda j, i: (resid_row0 // tm + i, j)),
                               (mod, (None, 1, tn), lambda j, i: (mr(i), 0, 2 * d // tn + j))],
                       outs=[(jax.ShapeDtypeStruct((n_rows, d), F32), (tm, tn), lambda j, i: (i, j))],
                       name=name)
        return o

    xs = out_proj([(a_ctx, a_lat), gated], n_all, ab_w_out, xs, 0, mod0, n_ctx, "ab_out")
    h2, logits = _norm_mod(xs, norm2_g[0], mod0, 3, 4, n_ctx, seq, router=(moe_w_router[0], moe_b_router[0]))
    xs, h = _moe_block(xs, h2, logits, mod0, 0, n_ctx, seq, moe_w_gu, moe_b_gu, moe_w_down, moe_b_down,
                       next_norm=(norm1_g[1], mod1))

    q, k, v = _mla_project(h, n_ctx, seq, tm, mla_w_in, mla_q_norm_g, mla_kv_norm_g, mla_w_uq, mla_w_ukv,
                           mla_q_g, mla_k_g)

    tq = seq
    attn = _attention(q, k, v, n_batch=n_batch, n_heads=MLA_HEADS, dq=QK_PAD, dv=V_DIM, tq=tq,
                      n_q_tiles=seq // tq, q_blk0=0, t_k=seq, k_blk0=n_ctx // seq, t_ctx=t_ctx, n_sub=8,
                      name="mla_attention")
    x_lat = out_proj(attn, n_lat, mla_w_out, xs, n_ctx, mod1, 0, "mla_out")
    h2, logits = _norm_mod(x_lat, norm2_g[1], mod1, 3, 4, 0, seq, router=(moe_w_router[1], moe_b_router[1]))
    x_lat = _moe_block(x_lat, h2, logits, mod1, 1, 0, seq, moe_w_gu, moe_b_gu, moe_w_down, moe_b_down)
    return x_lat.reshape(n_batch, seq, d)
```

```python
import functools

import numpy as np
import jax
import jax.numpy as jnp
from jax import lax
from jax.experimental import pallas as pl
from jax.experimental.pallas import tpu as pltpu

F32 = jnp.float32
BF16 = jnp.bfloat16

D_MODEL = 2048
GRID_W = 64
EPS = 1e-6
NEG = -1e30

NA_HEADS = 8
NA_HEAD_DIM = 128
NA_WIDTH = NA_HEADS * NA_HEAD_DIM
NA_WIN_H = 8
NA_WIN_W = 16
SG_GROUPS = 8
SG_WIDTH = D_MODEL // 2
SG_GROUP_DIM = SG_WIDTH // SG_GROUPS
SG_CHUNK = 128

MLA_HEADS = 16
Q_LORA = 512
KV_LORA = 256
QK_NOPE = 128
QK_ROPE = 64
V_DIM = 128
QK_DIM = QK_NOPE + QK_ROPE
QK_PAD = 256
MLA_IN_PAD = Q_LORA + KV_LORA + 128
ROPE_THETA = 10000.0

N_EXPERTS = 32
TOP_K = 4
D_FF = D_MODEL
SWIGLU_ALPHA = 1.702
SWIGLU_LIMIT = 7.0

LANE = 128
V7X_VMEM_LIMIT = 56 * 1024 * 1024

MATMUL_TM = 1024
MATMUL_SUB_ROWS = 256
NA_TILE_ROWS = 4
NA_BAND_ROWS = NA_TILE_ROWS + NA_WIN_H - 1
MOE_TM = 1024
MOE_PARTS = 4
MOE_TN_GU = 1024
MOE_TN_DOWN = 1024


def _params(n_axes):
    return pltpu.CompilerParams(dimension_semantics=("arbitrary",) * n_axes,
                                vmem_limit_bytes=V7X_VMEM_LIMIT)


def _norm_mod_val(x, g, shift, scale):
    y = x * lax.rsqrt(jnp.mean(x * x, axis=-1, keepdims=True) + EPS) * g
    return y * (1.0 + scale) + shift


def _norm_mod_body(x_ref, g_ref, sh_ref, sc_ref):
    return _norm_mod_val(x_ref[...], g_ref[...], sh_ref[...], sc_ref[...])


class _SplitRows:
    def __init__(self, ctx_ref, lat_ref, is_ctx):
        self.ctx_ref, self.lat_ref, self.is_ctx = ctx_ref, lat_ref, is_ctx
        self.shape, self.dtype = lat_ref.shape, lat_ref.dtype

    def __getitem__(self, key):
        return jnp.where(self.is_ctx, self.ctx_ref[key], self.lat_ref[key])


def _split_specs(block, index_map, n_ctx_tiles):
    def ctx_map(*ids):
        r, *rest = index_map(*ids)
        return (jnp.minimum(r, n_ctx_tiles - 1), *rest)

    def lat_map(*ids):
        r, *rest = index_map(*ids)
        return (jnp.maximum(r - n_ctx_tiles, 0), *rest)

    return [pl.BlockSpec(block, ctx_map), pl.BlockSpec(block, lat_map)]


def _norm_mod_kernel(x_ref, g_ref, sh_ref, sc_ref, o_ref):
    o_ref[...] = _norm_mod_body(x_ref, g_ref, sh_ref, sc_ref).astype(o_ref.dtype)


def _norm_mod_split_kernel(xc_ref, xl_ref, g_ref, sh_ref, sc_ref, o_ref, *, n_ctx_tiles):
    x_ref = _SplitRows(xc_ref, xl_ref, pl.program_id(0) < n_ctx_tiles)
    o_ref[...] = _norm_mod_body(x_ref, g_ref, sh_ref, sc_ref).astype(o_ref.dtype)


def _norm_mod_router_kernel(x_ref, g_ref, sh_ref, sc_ref, wcat_ref, br_ref, o_ref, lg_ref):
    h = _norm_mod_body(x_ref, g_ref, sh_ref, sc_ref)
    h_hi = h.astype(BF16)
    o_ref[...] = h_hi
    n_e = lg_ref.shape[1]
    h_lo = (h - h_hi.astype(F32)).astype(BF16)
    a = jnp.dot(h_hi, wcat_ref[...], preferred_element_type=F32)
    b = jnp.dot(h_lo, wcat_ref[:, :n_e], preferred_element_type=F32)
    lg_ref[...] = a[:, :n_e] + a[:, n_e:] + b + br_ref[...]


def _mod_row(i, tm, n_ctx_rows, seq):
    n_ctx_tiles = n_ctx_rows // tm
    return jnp.where(i < n_ctx_tiles, 0, 1 + (i - n_ctx_tiles) // (seq // tm))


def _norm_mod(x, gain, mod, k_shift, k_scale, n_ctx_rows, seq, router=None, tm=256):
    split = isinstance(x, tuple)
    rows = sum(a.shape[0] for a in x) if split else x.shape[0]
    d = gain.shape[0]
    mrow = functools.partial(_mod_row, tm=tm, n_ctx_rows=n_ctx_rows, seq=seq)
    x_specs = (_split_specs((tm, d), lambda i: (i, 0), n_ctx_rows // tm) if split
               else [pl.BlockSpec((tm, d), lambda i: (i, 0))])
    in_specs = x_specs + [
        pl.BlockSpec((1, d), lambda i: (0, 0)),
        pl.BlockSpec((None, 1, d), lambda i: (mrow(i), 0, k_shift)),
        pl.BlockSpec((None, 1, d), lambda i: (mrow(i), 0, k_scale)),
    ]
    args = [*x, gain.reshape(1, d), mod, mod] if split else [x, gain.reshape(1, d), mod, mod]
    if router is None:
        kern = (functools.partial(_norm_mod_split_kernel, n_ctx_tiles=n_ctx_rows // tm) if split
                else _norm_mod_kernel)
        return pl.pallas_call(
            kern, grid=(rows // tm,), in_specs=in_specs,
            out_specs=pl.BlockSpec((tm, d), lambda i: (i, 0)),
            out_shape=jax.ShapeDtypeStruct((rows, d), BF16),
            compiler_params=_params(1), name="norm_mod")(*args)
    assert not split
    w_r, b_r = router
    n_e = w_r.shape[1]
    w_hi = w_r.astype(BF16)
    w_cat = jnp.concatenate([w_hi, (w_r - w_hi.astype(F32)).astype(BF16)], axis=1)
    in_specs += [pl.BlockSpec((d, 2 * n_e), lambda i: (0, 0)), pl.BlockSpec((1, n_e), lambda i: (0, 0))]
    return pl.pallas_call(
        _norm_mod_router_kernel, grid=(rows // tm,), in_specs=in_specs,
        out_specs=[pl.BlockSpec((tm, d), lambda i: (i, 0)), pl.BlockSpec((tm, n_e), lambda i: (i, 0))],
        out_shape=[jax.ShapeDtypeStruct((rows, d), BF16), jax.ShapeDtypeStruct((rows, n_e), F32)],
        compiler_params=_params(1), name="norm_mod_router")(*args, w_cat, b_r.reshape(1, n_e))


def _matmul_kernel(*refs, x_split, extra_split, n_out, epilogue, n_sub, n_ctx_tiles):
    is_ctx = pl.program_id(1) < n_ctx_tiles
    refs = list(refs)

    def take(split):
        if split:
            return _SplitRows(refs.pop(0), refs.pop(0), is_ctx)
        return refs.pop(0)

    xs = [take(sp) for sp in x_split]
    w_ref = refs.pop(0)
    extra = [take(sp) for sp in extra_split]
    outs, wbf_ref = refs[:n_out], refs[-1]
    tm = outs[0].shape[0]

    @pl.when(pl.program_id(1) == 0)
    def _():
        wbf_ref[...] = w_ref[...].astype(BF16)

    sub = tm // n_sub

    def dot(r):
        acc, k0 = None, 0
        for x in xs:
            part = jnp.dot(x[r * sub:(r + 1) * sub, :], wbf_ref[k0:k0 + x.shape[1], :], preferred_element_type=F32)
            acc = part if acc is None else acc + part
            k0 += x.shape[1]
        return acc

    acc_next = dot(0)
    for r in range(n_sub):
        acc = acc_next
        if r + 1 < n_sub:
            acc_next = dot(r + 1)
        rows = slice(r * sub, (r + 1) * sub)
        epilogue(acc, [_RowView(e, rows) if e.shape[0] == tm else e for e in extra],
                 [_RowView(o, rows) for o in outs])


class _RowView:
    def __init__(self, ref, rows):
        self.ref, self.rows = ref, rows
        self.shape = (rows.stop - rows.start,) + tuple(ref.shape[1:])
        self.dtype = ref.dtype

    def _key(self, key):
        if key is Ellipsis:
            return (self.rows, slice(None))
        assert isinstance(key, tuple) and key[0] == slice(None), key
        return (self.rows,) + tuple(key[1:])

    def __getitem__(self, key):
        return self.ref[self._key(key)]

    def __setitem__(self, key, value):
        self.ref[self._key(key)] = value


def _matmul(x, w, *, w_lead=(), col0=0, n_cols, tm, tn, epilogue, extras=(), outs, name, row0=0, m=None):
    parts = x if isinstance(x, list) else [x]
    n_rows = lambda a: sum(b.shape[0] for b in a) if isinstance(a, tuple) else a.shape[0]
    width = lambda a: a[0].shape[1] if isinstance(a, tuple) else a.shape[1]
    splits = [a for a in parts + [e[0] for e in extras] if isinstance(a, tuple)]
    n_ctx_tiles = splits[0][0].shape[0] // tm if splits else 0
    assert all(a[0].shape[0] == n_ctx_tiles * tm for a in splits) and (row0 == 0 or not splits)
    k = sum(width(a) for a in parts)
    m = n_rows(parts[0]) - row0 if m is None else m
    assert m % tm == 0 and row0 % tm == 0 and n_cols % tn == 0 and col0 % tn == 0
    lead = tuple(w_lead)

    def specs(a, block, index_map):
        if isinstance(a, tuple):
            return _split_specs(block, index_map, n_ctx_tiles)
        return [pl.BlockSpec(block, index_map)]

    in_specs, args = [], []
    for a in parts:
        in_specs += specs(a, (tm, width(a)), lambda j, i: (row0 // tm + i, 0))
        args += list(a) if isinstance(a, tuple) else [a]
    in_specs.append(pl.BlockSpec((None,) * len(lead) + (k, tn), lambda j, i: lead + (0, col0 // tn + j)))
    args.append(w)
    for a, bs, im in extras:
        in_specs += specs(a, bs, im)
        args += list(a) if isinstance(a, tuple) else [a]
    out_specs = [pl.BlockSpec(bs, im) for _, bs, im in outs]
    kern = functools.partial(
        _matmul_kernel, x_split=tuple(isinstance(a, tuple) for a in parts),
        extra_split=tuple(isinstance(e[0], tuple) for e in extras), n_out=len(outs), epilogue=epilogue,
        n_sub=max(1, tm // MATMUL_SUB_ROWS), n_ctx_tiles=n_ctx_tiles)
    return pl.pallas_call(
        kern, grid=(n_cols // tn, m // tm), in_specs=in_specs, out_specs=out_specs,
        out_shape=[s for s, _, _ in outs],
        scratch_shapes=[pltpu.VMEM((k, tn), BF16)],
        compiler_params=_params(2), name=name)(*args)


def _ep_bias(acc, extra, outs):
    outs[0][...] = acc + extra[0][...]


def _ep_cast(acc, extra, outs):
    outs[0][...] = acc.astype(outs[0].dtype)


def _ep_gelu(acc, extra, outs):
    outs[0][...] = jax.nn.gelu(acc).astype(outs[0].dtype)


def _ep_head_rms(acc, extra, outs):
    g = extra[0][...]
    for h in range(acc.shape[1] // LANE):
        a = acc[:, h * LANE:(h + 1) * LANE]
        r = lax.rsqrt(jnp.mean(a * a, axis=-1, keepdims=True) + EPS)
        outs[0][:, h * LANE:(h + 1) * LANE] = (a * r * g).astype(outs[0].dtype)


def _ep_residual(acc, extra, outs):
    outs[0][...] = extra[0][...] + extra[1][...] * acc


def _rms(a, g):
    return a * lax.rsqrt(jnp.mean(a * a, axis=-1, keepdims=True) + EPS) * g


def _ep_mla_in(acc, extra, outs):
    outs[0][...] = _rms(acc[:, :Q_LORA], extra[0][...]).astype(BF16)
    outs[1][...] = _rms(acc[:, Q_LORA:Q_LORA + KV_LORA], extra[1][...]).astype(BF16)
    outs[2][...] = acc[:, Q_LORA + KV_LORA:]


def _rope_tail(t, tab_ref):
    return t * tab_ref[:, :LANE] + pltpu.roll(t, LANE // 2, axis=1) * tab_ref[:, LANE:]


def _ep_mla_q(acc, extra, outs):
    g_ref, tab_ref = extra
    g0 = g_ref[:, :LANE]
    g1 = g_ref[:, LANE:]
    for h in range(acc.shape[1] // QK_PAD):
        a0 = acc[:, h * QK_PAD:h * QK_PAD + LANE]
        a1 = acc[:, h * QK_PAD + LANE:(h + 1) * QK_PAD]
        r = lax.rsqrt(jnp.sum(a0 * a0 + a1 * a1, axis=-1, keepdims=True) * (1.0 / QK_DIM) + EPS)
        outs[0][:, h * QK_PAD:h * QK_PAD + LANE] = (a0 * r * g0).astype(BF16)
        outs[0][:, h * QK_PAD + LANE:(h + 1) * QK_PAD] = _rope_tail(a1 * r * g1, tab_ref).astype(BF16)


def _ep_mla_kv(acc, extra, outs):
    g_ref, tab_ref, pe_ref = extra
    k_out, v_out = outs
    g0 = g_ref[:, :LANE]
    pe = pe_ref[...]
    pe_sq = pe * pe
    pe_roped = _rope_tail(pe * g_ref[:, LANE:], tab_ref)
    for h in range(acc.shape[1] // (QK_NOPE + V_DIM)):
        base = h * (QK_NOPE + V_DIM)
        kn = acc[:, base:base + QK_NOPE]
        r = lax.rsqrt(jnp.sum(kn * kn + pe_sq, axis=-1, keepdims=True) * (1.0 / QK_DIM) + EPS)
        k_out[:, h * QK_PAD:h * QK_PAD + LANE] = (kn * r * g0).astype(BF16)
        k_out[:, h * QK_PAD + LANE:(h + 1) * QK_PAD] = (pe_roped * r).astype(BF16)
        v_out[:, h * V_DIM:(h + 1) * V_DIM] = acc[:, base + QK_NOPE:base + QK_NOPE + V_DIM].astype(BF16)


def _softmax_pv(s_parts, v_parts):
    m = s_parts[0].max(axis=-1, keepdims=True)
    for s in s_parts[1:]:
        m = jnp.maximum(m, s.max(axis=-1, keepdims=True))
    l = 0.0
    o = 0.0
    for s, v in zip(s_parts, v_parts):
        p = jnp.exp(s - m)
        l = l + p.sum(axis=-1, keepdims=True)
        o = o + jnp.dot(p.astype(BF16), v, preferred_element_type=F32)
    return o / l


def _qk(q, k):
    return lax.dot_general(q, k, (((1,), (1,)), ((), ())), preferred_element_type=F32)


def _attn_kernel(*refs, has_ctx, n_sub):
    if has_ctx:
        q_ref, k_ref, v_ref, kc_ref, vc_ref, o_ref = refs
    else:
        q_ref, k_ref, v_ref, o_ref = refs
    sub = q_ref.shape[0] // n_sub

    def scores(r):
        q = q_ref[r * sub:(r + 1) * sub, :]
        parts = [_qk(q, k_ref[...])]
        if has_ctx:
            parts.append(_qk(q, kc_ref[...]))
        return parts

    s_next = scores(0)
    for r in range(n_sub):
        s_cur = s_next
        if r + 1 < n_sub:
            s_next = scores(r + 1)
        v_parts = [v_ref[...]] + ([vc_ref[...]] if has_ctx else [])
        o_ref[r * sub:(r + 1) * sub, :] = _softmax_pv(s_cur, v_parts).astype(o_ref.dtype)


def _attention(q, k, v, *, n_batch, n_heads, dq, dv, tq, n_q_tiles, q_blk0, t_k, k_blk0, t_ctx=None, n_sub=1, name):
    has_ctx = t_ctx is not None
    in_specs = [
        pl.BlockSpec((tq, dq), lambda b, h, i: (q_blk0 + b * n_q_tiles + i, h)),
        pl.BlockSpec((t_k, dq), lambda b, h, i: (k_blk0 + b, h)),
        pl.BlockSpec((t_k, dv), lambda b, h, i: (k_blk0 + b, h)),
    ]
    args = [q, k, v]
    if has_ctx:
        in_specs += [pl.BlockSpec((t_ctx, dq), lambda b, h, i: (b, h)),
                     pl.BlockSpec((t_ctx, dv), lambda b, h, i: (b, h))]
        args += [k, v]
    return pl.pallas_call(
        functools.partial(_attn_kernel, has_ctx=has_ctx, n_sub=n_sub),
        grid=(n_batch, n_heads, n_q_tiles), in_specs=in_specs,
        out_specs=pl.BlockSpec((tq, dv), lambda b, h, i: (b * n_q_tiles + i, h)),
        out_shape=jax.ShapeDtypeStruct((n_batch * n_q_tiles * tq, n_heads * dv), BF16),
        compiler_params=_params(3), name=name)(*args)


def _na_band_start(t, rows):
    return jnp.clip(t * NA_TILE_ROWS - NA_WIN_H // 2, 0, rows - NA_BAND_ROWS)


def _na_kernel(q_ref, k_ref, v_ref, kc_ref, vc_ref, bias_ref, o_ref, *, rows):
    t = pl.program_id(1)
    ks = pl.multiple_of(_na_band_start(t, rows) * GRID_W, GRID_W)
    band = pl.ds(ks, NA_BAND_ROWS * GRID_W)

    def scores(h):
        cols = slice(h * NA_HEAD_DIM, (h + 1) * NA_HEAD_DIM)
        q = q_ref[:, cols]
        return [_qk(q, k_ref[band, cols]) + bias_ref[h], _qk(q, kc_ref[:, cols])]

    s_next = scores(0)
    for h in range(NA_HEADS):
        s_cur = s_next
        if h + 1 < NA_HEADS:
            s_next = scores(h + 1)
        cols = slice(h * NA_HEAD_DIM, (h + 1) * NA_HEAD_DIM)
        o_ref[:, cols] = _softmax_pv(s_cur, [v_ref[band, cols], vc_ref[:, cols]]).astype(o_ref.dtype)


def _na_bias_tables(rel_bias, rows):
    n_tiles = rows // NA_TILE_ROWS
    n_r, n_c = 2 * NA_WIN_H - 1, 2 * NA_WIN_W - 1
    qr = np.arange(NA_TILE_ROWS)[:, None]
    ur = np.arange(NA_BAND_ROWS)[None, :]
    qc = np.arange(GRID_W)[:, None]
    kc = np.arange(GRID_W)[None, :]
    c0 = np.clip(qc - NA_WIN_W // 2, 0, GRID_W - NA_WIN_W)
    col_valid = (kc >= c0) & (kc < c0 + NA_WIN_W)
    col_onehot = np.eye(n_c, dtype=np.float32)[np.clip(kc - qc + NA_WIN_W - 1, 0, n_c - 1)]
    geoms, ids = [], []
    for t in range(n_tiles):
        u0 = int(np.clip(t * NA_TILE_ROWS - NA_WIN_H // 2, 0, rows - NA_BAND_ROWS))
        r = t * NA_TILE_ROWS + qr
        key_row = u0 + ur
        r0 = np.clip(r - NA_WIN_H // 2, 0, rows - NA_WIN_H)
        geom = ((key_row >= r0) & (key_row < r0 + NA_WIN_H), np.clip(key_row - r + NA_WIN_H - 1, 0, n_r - 1))
        for gi, g in enumerate(geoms):
            if all(np.array_equal(a, b) for a, b in zip(g, geom)):
                ids.append(gi)
                break
        else:
            ids.append(len(geoms))
            geoms.append(geom)
    row_valid = np.stack([g[0] for g in geoms])
    row_onehot = np.eye(n_r, dtype=np.float32)[np.stack([g[1] for g in geoms])]
    bias = jnp.einsum('gaur,hrc,bkc->ghabuk', row_onehot, rel_bias.astype(F32), col_onehot,
                      precision=lax.Precision.HIGHEST)
    valid = row_valid[:, None, :, None, :, None] & col_valid[None, None, None, :, None, :]
    tables = jnp.where(valid, bias, NEG)
    return tables.reshape(len(geoms), NA_HEADS, NA_TILE_ROWS * GRID_W, NA_BAND_ROWS * GRID_W), ids


def _na_attention(q, k, v, rel_bias, *, n_batch, seq, t_ctx):
    rows = seq // GRID_W
    n_tiles = rows // NA_TILE_ROWS
    tq = NA_TILE_ROWS * GRID_W
    tables, ids = _na_bias_tables(rel_bias, rows)
    assert ids == [0] + [1] * (n_tiles - 2) + [2], ids
    ctx_tiles = n_batch * t_ctx // tq
    ctx_units = n_batch * t_ctx // seq

    def table_id(t):
        return jnp.where(t == 0, 0, jnp.where(t == n_tiles - 1, 2, 1))

    return pl.pallas_call(
        functools.partial(_na_kernel, rows=rows),
        grid=(n_batch, n_tiles),
        in_specs=[
            pl.BlockSpec((tq, NA_WIDTH), lambda b, t: (ctx_tiles + b * n_tiles + t, 0)),
            pl.BlockSpec((seq, NA_WIDTH), lambda b, t: (ctx_units + b, 0)),
            pl.BlockSpec((seq, NA_WIDTH), lambda b, t: (ctx_units + b, 0)),
            pl.BlockSpec((t_ctx, NA_WIDTH), lambda b, t: (b, 0)),
            pl.BlockSpec((t_ctx, NA_WIDTH), lambda b, t: (b, 0)),
            pl.BlockSpec((None, NA_HEADS, tq, NA_BAND_ROWS * GRID_W), lambda b, t: (table_id(t), 0, 0, 0)),
        ],
        out_specs=pl.BlockSpec((tq, NA_WIDTH), lambda b, t: (b * n_tiles + t, 0)),
        out_shape=jax.ShapeDtypeStruct((n_batch * seq, NA_WIDTH), BF16),
        compiler_params=_params(2), name="na_attention")(q, k, v, k, v, tables)


def _sg_kernel(uv_ref, g_ref, b_ref, ws_ref, bs_ref, o_ref, *, chunks):
    for c in range(chunks):
        r = slice(c * SG_CHUNK, (c + 1) * SG_CHUNK)
        z = uv_ref[r, SG_WIDTH:].astype(F32)
        mu = jnp.mean(z, axis=-1, keepdims=True)
        zc = z - mu
        var = jnp.mean(zc * zc, axis=-1, keepdims=True)
        zn = (zc * lax.rsqrt(var + EPS) * g_ref[...] + b_ref[...]).astype(BF16)
        for g in range(SG_GROUPS):
            cols = slice(g * SG_GROUP_DIM, (g + 1) * SG_GROUP_DIM)
            mixed = jnp.dot(ws_ref[g], zn[:, cols], preferred_element_type=F32) + bs_ref[g]
            o_ref[r, cols] = (uv_ref[r, cols].astype(F32) * mixed).astype(o_ref.dtype)


def _spatial_gating(uv, ln_g, ln_b, w_s, b_s, tm=512):
    rows = uv.shape[0]
    bs = jnp.broadcast_to(b_s.astype(F32)[:, :, None], (SG_GROUPS, SG_CHUNK, SG_GROUP_DIM))
    return pl.pallas_call(
        functools.partial(_sg_kernel, chunks=tm // SG_CHUNK),
        grid=(rows // tm,),
        in_specs=[
            pl.BlockSpec((tm, 2 * SG_WIDTH), lambda i: (i, 0)),
            pl.BlockSpec((1, SG_WIDTH), lambda i: (0, 0)),
            pl.BlockSpec((1, SG_WIDTH), lambda i: (0, 0)),
            pl.BlockSpec((SG_GROUPS, SG_CHUNK, SG_CHUNK), lambda i: (0, 0, 0)),
            pl.BlockSpec((SG_GROUPS, SG_CHUNK, SG_GROUP_DIM), lambda i: (0, 0, 0)),
        ],
        out_specs=pl.BlockSpec((tm, SG_WIDTH), lambda i: (i, 0)),
        out_shape=jax.ShapeDtypeStruct((rows, SG_WIDTH), BF16),
        compiler_params=_params(1), name="spatial_gating")(
            uv, ln_g.reshape(1, SG_WIDTH), ln_b.reshape(1, SG_WIDTH), w_s.astype(BF16), bs)


def _swiglu(g, u):
    g = jnp.minimum(g, SWIGLU_LIMIT)
    u = jnp.clip(u, -SWIGLU_LIMIT, SWIGLU_LIMIT)
    return (u + 1.0) * (g * jax.nn.sigmoid(SWIGLU_ALPHA * g))


def _for_each_valid_part(i, tv_ref, o_ref, compute):
    part = o_ref.shape[0] // MOE_PARTS
    n_valid = (tv_ref[i] + part - 1) // part
    rows = [slice(p * part, (p + 1) * part) for p in range(MOE_PARTS)]
    for n in range(MOE_PARTS + 1):

        @pl.when(n_valid == n)
        def _(n=n):
            acc_next = compute.matmul(rows[0]) if n else None
            for p in range(n):
                acc = acc_next
                if p + 1 < n:
                    acc_next = compute.matmul(rows[p + 1])
                o_ref[rows[p], :] = compute.epilogue(acc).astype(o_ref.dtype)
            for p in range(n, MOE_PARTS):
                o_ref[rows[p], :] = jnp.zeros((part, o_ref.shape[1]), o_ref.dtype)


class _GateUp:
    def __init__(self, x_ref, w_bf, bg_ref, bu_ref):
        self.x_ref, self.w_bf, self.bg_ref, self.bu_ref = x_ref, w_bf, bg_ref, bu_ref

    def matmul(self, rows):
        return jnp.dot(self.x_ref[rows, :], self.w_bf[...], preferred_element_type=F32)

    def epilogue(self, acc):
        tn = acc.shape[1] // 2
        return _swiglu(acc[:, :tn] + self.bg_ref[...], acc[:, tn:] + self.bu_ref[...])


class _Down:
    def __init__(self, h_ref, w_bf, b_ref):
        self.h_ref, self.w_bf, self.b_ref = h_ref, w_bf, b_ref

    def matmul(self, rows):
        return jnp.dot(self.h_ref[rows, :], self.w_bf[...], preferred_element_type=F32)

    def epilogue(self, acc):
        return acc + self.b_ref[...]


def _expert_weight_copies(w_hbm, land, sem, layer, expert, cols):
    tn = land.shape[1] // len(cols)
    return [pltpu.make_async_copy(w_hbm.at[layer, expert, :, pl.ds(pl.multiple_of(col, tn), tn)],
                                  land.at[:, pl.ds(k * tn, tn)], sem.at[k])
            for k, col in enumerate(cols)]


def _cast_rows(src, dst, chunk=256):
    def body(c, carry):
        rows = pl.ds(pl.multiple_of(c * chunk, chunk), chunk)
        dst[rows, :] = src[rows, :].astype(dst.dtype)
        return carry
    lax.fori_loop(0, src.shape[0] // chunk, body, 0)


def _refresh_expert_weights(te_ref, first_ref, nxt_ref, w_hbm, land, bf, sem, layer, col_fn, n_j):
    j = pl.program_id(0)
    i = pl.program_id(1)

    @pl.when(first_ref[i] == 1)
    def _():
        cur = _expert_weight_copies(w_hbm, land, sem, layer, te_ref[i], col_fn(j))

        @pl.when((i == 0) & (j == 0))
        def _():
            for c in cur:
                c.start()

        for c in cur:
            c.wait()
        _cast_rows(land, bf)
        nxt = nxt_ref[i]

        @pl.when(nxt >= 0)
        def _():
            for c in _expert_weight_copies(w_hbm, land, sem, layer, nxt, col_fn(j)):
                c.start()

        @pl.when((nxt < 0) & (j + 1 < n_j))
        def _():
            for c in _expert_weight_copies(w_hbm, land, sem, layer, te_ref[0], col_fn(j + 1)):
                c.start()


def _moe_gu_kernel(te_ref, tv_ref, ts_ref, first_ref, nxt_ref, x_ref, w_hbm, bg_ref, bu_ref, o_ref, w_land, w_bf, sem,
                   *, layer, n_j):
    tn = w_bf.shape[1] // 2
    _refresh_expert_weights(te_ref, first_ref, nxt_ref, w_hbm, w_land, w_bf, sem, layer,
                            lambda j: (j * tn, D_FF + j * tn), n_j)
    _for_each_valid_part(pl.program_id(1), tv_ref, o_ref, _GateUp(x_ref, w_bf, bg_ref, bu_ref))


def _moe_down_kernel(te_ref, tv_ref, ts_ref, first_ref, nxt_ref, ha_ref, hb_ref, w_hbm, b_ref, o_ref, w_land, w_bf, sem,
                     *, layer, n_j, n_first_half):
    tn = w_bf.shape[1]
    _refresh_expert_weights(te_ref, first_ref, nxt_ref, w_hbm, w_land, w_bf, sem, layer,
                            lambda j: (j * tn,), n_j)
    i = pl.program_id(1)
    h_ref = _SplitRows(ha_ref, hb_ref, ts_ref[i] < n_first_half)
    _for_each_valid_part(i, tv_ref, o_ref, _Down(h_ref, w_bf, b_ref))


def _expert_runs(tile_e):
    n_tiles = tile_e.shape[0]
    idx = jnp.arange(n_tiles, dtype=jnp.int32)
    first = jnp.concatenate([jnp.ones((1,), bool), tile_e[1:] != tile_e[:-1]])
    later_first = lax.cummin(jnp.where(first, idx, n_tiles)[::-1])[::-1]
    next_first = jnp.concatenate([later_first[1:], jnp.full((1,), n_tiles, jnp.int32)])
    nxt = jnp.where(next_first < n_tiles, tile_e[jnp.minimum(next_first, n_tiles - 1)], -1)
    return first.astype(jnp.int32), nxt.astype(jnp.int32)


def _moe_experts(x_halves, tile_e, tile_valid, tile_src, layer, w_gu, b_gu, w_down, b_down):
    tm, tn, tn_down = MOE_TM, MOE_TN_GU, MOE_TN_DOWN
    d = x_halves[0].shape[1]
    n_a = x_halves[0].shape[0] // tm
    n_tiles = n_a + x_halves[1].shape[0] // tm
    n_j = D_FF // tn
    n_jd = d // tn_down
    x_map = lambda j, i, te, tv, ts, fr, nx: (ts[i], 0)
    out_map = lambda j, i, te, tv, ts, fr, nx: (i, j)

    hid = []
    for x_half, t0, t1 in ((x_halves[0], 0, n_a), (x_halves[1], n_a, n_tiles)):
        te = tile_e[t0:t1]
        src = jnp.clip(tile_src[t0:t1] - t0, 0, t1 - t0 - 1)
        hid.append(pl.pallas_call(
            functools.partial(_moe_gu_kernel, layer=layer, n_j=n_j),
            grid_spec=pltpu.PrefetchScalarGridSpec(
                num_scalar_prefetch=5, grid=(n_j, t1 - t0),
                in_specs=[
                    pl.BlockSpec((tm, d), x_map),
                    pl.BlockSpec(memory_space=pl.ANY),
                    pl.BlockSpec((None, None, 1, tn), lambda j, i, te, tv, ts, fr, nx: (layer, te[i], 0, j)),
                    pl.BlockSpec((None, None, 1, tn), lambda j, i, te, tv, ts, fr, nx: (layer, te[i], 0, n_j + j)),
                ],
                out_specs=pl.BlockSpec((tm, tn), out_map),
                scratch_shapes=[pltpu.VMEM((d, 2 * tn), F32), pltpu.VMEM((d, 2 * tn), BF16),
                                pltpu.SemaphoreType.DMA((2,))]),
            out_shape=jax.ShapeDtypeStruct(((t1 - t0) * tm, D_FF), BF16),
            compiler_params=_params(2), name="moe_gate_up")(
                te, tile_valid[t0:t1], src, *_expert_runs(te), x_half, w_gu, b_gu[:, :, None, :], b_gu[:, :, None, :]))

    return pl.pallas_call(
        functools.partial(_moe_down_kernel, layer=layer, n_j=n_jd, n_first_half=n_a),
        grid_spec=pltpu.PrefetchScalarGridSpec(
            num_scalar_prefetch=5, grid=(n_jd, n_tiles),
            in_specs=[
                pl.BlockSpec((tm, D_FF), lambda j, i, te, tv, ts, fr, nx: (jnp.minimum(ts[i], n_a - 1), 0)),
                pl.BlockSpec((tm, D_FF), lambda j, i, te, tv, ts, fr, nx: (jnp.maximum(ts[i] - n_a, 0), 0)),
                pl.BlockSpec(memory_space=pl.ANY),
                pl.BlockSpec((None, None, 1, tn_down), lambda j, i, te, tv, ts, fr, nx: (layer, te[i], 0, j)),
            ],
            out_specs=pl.BlockSpec((tm, tn_down), out_map),
            scratch_shapes=[pltpu.VMEM((D_FF, tn_down), F32), pltpu.VMEM((D_FF, tn_down), BF16),
                            pltpu.SemaphoreType.DMA((1,))]),
        out_shape=jax.ShapeDtypeStruct((n_tiles * tm, d), BF16),
        compiler_params=_params(2), name="moe_down")(
            tile_e, tile_valid, tile_src, *_expert_runs(tile_e), hid[0], hid[1], w_down, b_down[:, :, None, :])


def _moe_route(logits):
    n = logits.shape[0]
    nk = n * TOP_K
    tm = MOE_TM
    i32 = jnp.int32
    top_val, top_idx = lax.top_k(logits, TOP_K)
    gate = jax.nn.softmax(top_val, axis=-1)
    flat_e = top_idx.reshape(nk).astype(i32)
    iota = jnp.arange(nk, dtype=i32)
    bits = (nk - 1).bit_length()
    assert (N_EXPERTS << bits) < 2 ** 31
    order = lax.sort((flat_e << bits) | iota) & ((1 << bits) - 1)
    _, inv_order = lax.sort((order, iota), num_keys=1)
    onehot = flat_e[:, None] == jnp.arange(N_EXPERTS, dtype=i32)[None, :]
    counts = jnp.sum(onehot, axis=0, dtype=i32)
    padded = (counts + tm - 1) // tm * tm
    pad_end = jnp.cumsum(padded)
    pad_start = pad_end - padded
    start = jnp.cumsum(counts) - counts
    pos = inv_order + jnp.sum(jnp.where(onehot, (pad_start - start)[None, :], 0), axis=1)
    pos = pos.reshape(n, TOP_K).T.reshape(nk)

    n_tiles = nk // tm + N_EXPERTS
    n_used = pad_end[-1] // tm
    tile_src = jnp.minimum(jnp.arange(n_tiles, dtype=i32), n_used - 1)
    tile_e = jnp.minimum(jnp.sum(pad_end[None, :] <= (tile_src * tm)[:, None], axis=1, dtype=i32), N_EXPERTS - 1)
    in_use = jnp.arange(n_tiles, dtype=i32) < n_used
    rank0 = tile_src * tm - pad_start[tile_e]
    tile_valid = jnp.where(in_use, jnp.clip(counts[tile_e] - rank0, 0, tm), 0).astype(i32)
    within = jnp.arange(tm, dtype=i32)[None, :]
    valid = within < tile_valid[:, None]
    src = jnp.clip((start[tile_e] + rank0)[:, None] + within, 0, nk - 1)
    filler = (jnp.arange(n_tiles * tm, dtype=i32) % n).reshape(n_tiles, tm)
    rows_tok = jnp.where(valid, order.at[src].get(mode="promise_in_bounds") // TOP_K, filler).reshape(n_tiles * tm)
    return gate, rows_tok, pos, tile_e, tile_valid, tile_src.astype(i32)


def _moe_combine(x_ref, g2_ref, gate_ref, y_ref):
    gate = gate_ref[...]
    f = gate[:, 0:1] * y_ref[0].astype(F32)
    for k in range(1, TOP_K):
        f = f + gate[:, k:k + 1] * y_ref[k].astype(F32)
    return x_ref[...] + g2_ref[...] * f


def _moe_combine_kernel(x_ref, g2_ref, gate_ref, y_ref, o_ref):
    o_ref[...] = _moe_combine(x_ref, g2_ref, gate_ref, y_ref)


def _moe_combine_norm_kernel(x_ref, g2_ref, gate_ref, y_ref, ng_ref, nsh_ref, nsc_ref, o_ref, h_ref):
    x = _moe_combine(x_ref, g2_ref, gate_ref, y_ref)
    o_ref[...] = x
    h_ref[...] = _norm_mod_val(x, ng_ref[...], nsh_ref[...], nsc_ref[...]).astype(h_ref.dtype)


def _moe_block(xs, h2, logits, mod, layer, n_ctx_rows, seq, w_gu, b_gu, w_down, b_down, next_norm=None, tm=256):
    n, d = h2.shape
    gate, rows_tok, pos, tile_e, tile_valid, tile_src = _moe_route(logits)
    half = tile_e.shape[0] // 4 * MOE_TM
    x_halves = [h2.at[r].get(mode="promise_in_bounds") for r in (rows_tok[:half], rows_tok[half:])]
    y = _moe_experts(x_halves, tile_e, tile_valid, tile_src, layer, w_gu, b_gu, w_down, b_down)
    y_k = y.at[pos].get(mode="promise_in_bounds").reshape(TOP_K, n, d)
    mrow = functools.partial(_mod_row, tm=tm, n_ctx_rows=n_ctx_rows, seq=seq)
    row_spec = pl.BlockSpec((tm, d), lambda i: (i, 0))
    mod_spec = lambda k: pl.BlockSpec((None, 1, d), lambda i: (mrow(i), 0, k))
    in_specs = [row_spec, mod_spec(5), pl.BlockSpec((tm, TOP_K), lambda i: (i, 0)),
                pl.BlockSpec((TOP_K, tm, d), lambda i: (0, i, 0))]
    if next_norm is None:
        return pl.pallas_call(
            _moe_combine_kernel, grid=(n // tm,), in_specs=in_specs, out_specs=row_spec,
            out_shape=jax.ShapeDtypeStruct((n, d), F32),
            compiler_params=_params(1), name="moe_combine")(xs, mod, gate, y_k)
    gain, next_mod = next_norm
    return pl.pallas_call(
        _moe_combine_norm_kernel, grid=(n // tm,),
        in_specs=in_specs + [pl.BlockSpec((1, d), lambda i: (0, 0)), mod_spec(0), mod_spec(1)],
        out_specs=[row_spec, row_spec],
        out_shape=[jax.ShapeDtypeStruct((n, d), F32), jax.ShapeDtypeStruct((n, d), BF16)],
        compiler_params=_params(1), name="moe_combine_norm")(
            xs, mod, gate, y_k, gain.reshape(1, d), next_mod, next_mod)


def _rope_tail_layout(x):
    half = QK_ROPE // 2
    z = jnp.zeros(x.shape[:-1] + (LANE // 2 - half,), x.dtype)
    return jnp.concatenate([x[..., :half], z, x[..., half:], z], axis=-1)


def _axial_rope_tables(seq, tm):
    t = jnp.arange(seq)
    row = (t // GRID_W).astype(F32)
    col = (t % GRID_W).astype(F32)
    n_freq = QK_ROPE // 4
    inv = ROPE_THETA ** (-jnp.arange(n_freq, dtype=F32) / n_freq)
    ang = jnp.concatenate([row[:, None] * inv, col[:, None] * inv], axis=-1)
    cos, sin = jnp.cos(ang), jnp.sin(ang)
    tab = jnp.concatenate([_rope_tail_layout(jnp.concatenate([cos, cos], axis=-1)),
                           _rope_tail_layout(jnp.concatenate([-sin, sin], axis=-1))], axis=-1)
    ident = jnp.concatenate([_rope_tail_layout(jnp.ones((tm, QK_ROPE), F32)), jnp.zeros((tm, LANE), F32)], axis=-1)
    return jnp.concatenate([ident, tab], axis=0)


def _ada_mod(cvec, ada_w, ada_b, layer):
    n = cvec.shape[0]
    m_pad = 16
    a = jnp.zeros((m_pad, D_MODEL), F32).at[:n].set(jax.nn.silu(cvec)).astype(BF16)
    n_cols = ada_w.shape[-1]
    tn = 1024
    (mod,) = _matmul(
        a, ada_w, w_lead=(layer,), n_cols=n_cols, tm=m_pad, tn=tn, epilogue=_ep_bias,
        extras=[(ada_b[:, None, :], (None, 1, tn), lambda j, i: (layer, 0, j))],
        outs=[(jax.ShapeDtypeStruct((m_pad, n_cols), F32), (m_pad, tn), lambda j, i: (i, j))],
        name="ada_mod")
    return mod[:n, None, :]


def _mla_project(h, n_ctx, seq, tm, mla_w_in, mla_q_norm_g, mla_kv_norm_g, mla_w_uq, mla_w_ukv, mla_q_g, mla_k_g):
    n_all = h.shape[0]
    n_lat = n_all - n_ctx
    w_in_pad = jnp.concatenate([mla_w_in[..., :Q_LORA + KV_LORA], _rope_tail_layout(mla_w_in[..., Q_LORA + KV_LORA:])],
                               axis=-1)
    c_q, c_kv, k_pe = _matmul(
        h, w_in_pad, w_lead=(0,), n_cols=MLA_IN_PAD, tm=tm, tn=MLA_IN_PAD, epilogue=_ep_mla_in,
        extras=[(mla_q_norm_g[0].reshape(1, Q_LORA), (1, Q_LORA), lambda j, i: (0, 0)),
                (mla_kv_norm_g[0].reshape(1, KV_LORA), (1, KV_LORA), lambda j, i: (0, 0))],
        outs=[(jax.ShapeDtypeStruct((n_all, Q_LORA), BF16), (tm, Q_LORA), lambda j, i: (i, 0)),
              (jax.ShapeDtypeStruct((n_all, KV_LORA), BF16), (tm, KV_LORA), lambda j, i: (i, 0)),
              (jax.ShapeDtypeStruct((n_all, LANE), F32), (tm, LANE), lambda j, i: (i, 0))],
        name="mla_in")

    rope_tab = _axial_rope_tables(seq, tm)
    n_ctx_tiles = n_ctx // tm
    seq_tiles = seq // tm
    pad_head = lambda a: jnp.concatenate([a[..., :QK_NOPE], _rope_tail_layout(a[..., QK_NOPE:])], axis=-1)
    pad_gain = lambda g, s: pad_head(g * s).reshape(1, QK_PAD)
    w_uq = pad_head(mla_w_uq[0].reshape(Q_LORA, MLA_HEADS, QK_DIM)).reshape(Q_LORA, MLA_HEADS * QK_PAD)
    tn_up = 1024
    (q,) = _matmul(
        c_q, w_uq, row0=n_ctx, n_cols=MLA_HEADS * QK_PAD, tm=tm, tn=tn_up, epilogue=_ep_mla_q,
        extras=[(pad_gain(mla_q_g[0], QK_DIM ** -0.5), (1, QK_PAD), lambda j, i: (0, 0)),
                (rope_tab, (tm, 2 * LANE), lambda j, i: (1 + i % seq_tiles, 0))],
        outs=[(jax.ShapeDtypeStruct((n_lat, MLA_HEADS * QK_PAD), BF16), (tm, tn_up), lambda j, i: (i, j))],
        name="mla_up_q")
    heads_per_tile = tn_up // (QK_NOPE + V_DIM)
    k, v = _matmul(
        c_kv, mla_w_ukv, w_lead=(0,), n_cols=MLA_HEADS * (QK_NOPE + V_DIM), tm=tm, tn=tn_up, epilogue=_ep_mla_kv,
        extras=[(pad_gain(mla_k_g[0], 1.0), (1, QK_PAD), lambda j, i: (0, 0)),
                (rope_tab, (tm, 2 * LANE),
                 lambda j, i: (jnp.where(i < n_ctx_tiles, 0, 1 + (i - n_ctx_tiles) % seq_tiles), 0)),
                (k_pe, (tm, LANE), lambda j, i: (i, 0))],
        outs=[(jax.ShapeDtypeStruct((n_all, MLA_HEADS * QK_PAD), BF16), (tm, heads_per_tile * QK_PAD),
               lambda j, i: (i, j)),
              (jax.ShapeDtypeStruct((n_all, MLA_HEADS * V_DIM), BF16), (tm, heads_per_tile * V_DIM),
               lambda j, i: (i, j))],
        name="mla_up_kv")
    return q, k, v


def kernel(x, c, ctx, c_ctx, ada_w, ada_b, norm1_g, norm2_g, ab_w_in, ab_w_out, na_q_g, na_k_g, na_rel_bias,
           sg_norm_g, sg_norm_b, sg_w, sg_b, mla_w_in, mla_q_norm_g, mla_kv_norm_g, mla_w_uq, mla_w_ukv,
           mla_q_g, mla_k_g, mla_w_out, moe_w_router, moe_b_router, moe_w_gu, moe_b_gu, moe_w_down, moe_b_down):
    n_batch, seq, d = x.shape
    t_ctx = ctx.shape[1]
    n_ctx = n_batch * t_ctx
    n_lat = n_batch * seq
    n_all = n_ctx + n_lat
    tm = MATMUL_TM
    tn = 512
    cvec = jnp.concatenate([c_ctx[None, :], c], axis=0)
    xs = (ctx.reshape(n_ctx, d), x.reshape(n_lat, d))
    mod0 = _ada_mod(cvec, ada_w, ada_b, 0)
    mod1 = _ada_mod(cvec, ada_w, ada_b, 1)

    h = _norm_mod(xs, norm1_g[0], mod0, 0, 1, n_ctx, seq)
    w_in = ab_w_in
    q_gain = (na_q_g[0] * (NA_HEAD_DIM ** -0.5)).reshape(1, NA_HEAD_DIM)
    k_gain = na_k_g[0].reshape(1, NA_HEAD_DIM)

    def in_proj(col0, n_cols, epilogue, extras, name):
        (o,) = _matmul(h, w_in, w_lead=(0,), col0=col0, n_cols=n_cols, tm=tm, tn=tn, epilogue=epilogue,
                       extras=extras,
                       outs=[(jax.ShapeDtypeStruct((n_all, n_cols), BF16), (tm, tn), lambda j, i: (i, j))],
                       name=name)
        return o

    gain_spec = lambda g: [(g, (1, NA_HEAD_DIM), lambda j, i: (0, 0))]
    q = in_proj(0, NA_WIDTH, _ep_head_rms, gain_spec(q_gain), "ab_in_q")
    k = in_proj(NA_WIDTH, NA_WIDTH, _ep_head_rms, gain_spec(k_gain), "ab_in_k")
    v = in_proj(2 * NA_WIDTH, NA_WIDTH, _ep_cast, [], "ab_in_v")
    uv = in_proj(3 * NA_WIDTH, 2 * SG_WIDTH, _ep_gelu, [], "ab_in_uv")

    a_lat = _na_attention(q, k, v, na_rel_bias[0], n_batch=n_batch, seq=seq, t_ctx=t_ctx)
    a_ctx = _attention(q, k, v, n_batch=n_batch, n_heads=NA_HEADS, dq=NA_HEAD_DIM, dv=NA_HEAD_DIM,
                       tq=t_ctx, n_q_tiles=1, q_blk0=0, t_k=t_ctx, k_blk0=0, name="ctx_attention")
    gated = _spatial_gating(uv, sg_norm_g[0], sg_norm_b[0], sg_w[0], sg_b[0])

    def out_proj(inp, n_rows, w, resid, resid_row0, mod, n_ctx_rows, name):
        mr = functools.partial(_mod_row, tm=tm, n_ctx_rows=n_ctx_rows, seq=seq)
        (o,) = _matmul(inp, w, w_lead=(0,), n_cols=d, tm=tm, tn=tn, epilogue=_ep_residual,
                       extras=[(resid, (tm, tn), lambda j, i: (resid_row0 // tm + i, j)),
                               (mod, (None, 1, tn), lambda j, i: (mr(i), 0, 2 * d // tn + j))],
                       outs=[(jax.ShapeDtypeStruct((n_rows, d), F32), (tm, tn), lambda j, i: (i, j))],
                       name=name)
        return o

    xs = out_proj([(a_ctx, a_lat), gated], n_all, ab_w_out, xs, 0, mod0, n_ctx, "ab_out")
    h2, logits = _norm_mod(xs, norm2_g[0], mod0, 3, 4, n_ctx, seq, router=(moe_w_router[0], moe_b_router[0]))
    xs, h = _moe_block(xs, h2, logits, mod0, 0, n_ctx, seq, moe_w_gu, moe_b_gu, moe_w_down, moe_b_down,
                       next_norm=(norm1_g[1], mod1))

    q, k, v = _mla_project(h, n_ctx, seq, tm, mla_w_in, mla_q_norm_g, mla_kv_norm_g, mla_w_uq, mla_w_ukv,
                           mla_q_g, mla_k_g)

    tq = seq
    attn = _attention(q, k, v, n_batch=n_batch, n_heads=MLA_HEADS, dq=QK_PAD, dv=V_DIM, tq=tq,
                      n_q_tiles=seq // tq, q_blk0=0, t_k=seq, k_blk0=n_ctx // seq, t_ctx=t_ctx, n_sub=8,
                      name="mla_attention")
    x_lat = out_proj(attn, n_lat, mla_w_out, xs, n_ctx, mod1, 0, "mla_out")
    h2, logits = _norm_mod(x_lat, norm2_g[1], mod1, 3, 4, 0, seq, router=(moe_w_router[1], moe_b_router[1]))
    x_lat = _moe_block(x_lat, h2, logits, mod1, 1, 0, seq, moe_w_gu, moe_b_gu, moe_w_down, moe_b_down)
    return x_lat.reshape(n_batch, seq, d)
```

```python
import functools

import numpy as np
import jax
import jax.numpy as jnp
from jax import lax
from jax.experimental import pallas as pl
from jax.experimental.pallas import tpu as pltpu

F32 = jnp.float32
BF16 = jnp.bfloat16

D_MODEL = 2048
GRID_W = 64
EPS = 1e-6
NEG = -1e30

NA_HEADS = 8
NA_HEAD_DIM = 128
NA_WIDTH = NA_HEADS * NA_HEAD_DIM
NA_WIN_H = 8
NA_WIN_W = 16
SG_GROUPS = 8
SG_WIDTH = D_MODEL // 2
SG_GROUP_DIM = SG_WIDTH // SG_GROUPS
SG_CHUNK = 128

MLA_HEADS = 16
Q_LORA = 512
KV_LORA = 256
QK_NOPE = 128
QK_ROPE = 64
V_DIM = 128
QK_DIM = QK_NOPE + QK_ROPE
QK_PAD = 256
MLA_IN_PAD = Q_LORA + KV_LORA + 128
ROPE_THETA = 10000.0

N_EXPERTS = 32
TOP_K = 4
D_FF = D_MODEL
SWIGLU_ALPHA = 1.702
SWIGLU_LIMIT = 7.0

LANE = 128
V7X_VMEM_LIMIT = 56 * 1024 * 1024

MATMUL_TM = 1024
MATMUL_SUB_ROWS = 256
NA_TILE_ROWS = 4
NA_BAND_ROWS = NA_TILE_ROWS + NA_WIN_H - 1
MOE_TM = 1024
MOE_PARTS = 4
MOE_TN_GU = 1024
MOE_TN_DOWN = 1024


def _params(n_axes):
    return pltpu.CompilerParams(dimension_semantics=("arbitrary",) * n_axes,
                                vmem_limit_bytes=V7X_VMEM_LIMIT)


def _norm_mod_val(x, g, shift, scale):
    y = x * lax.rsqrt(jnp.mean(x * x, axis=-1, keepdims=True) + EPS) * g
    return y * (1.0 + scale) + shift


def _norm_mod_body(x_ref, g_ref, sh_ref, sc_ref):
    return _norm_mod_val(x_ref[...], g_ref[...], sh_ref[...], sc_ref[...])


class _SplitRows:
    def __init__(self, ctx_ref, lat_ref, is_ctx):
        self.ctx_ref, self.lat_ref, self.is_ctx = ctx_ref, lat_ref, is_ctx
        self.shape, self.dtype = lat_ref.shape, lat_ref.dtype

    def __getitem__(self, key):
        return jnp.where(self.is_ctx, self.ctx_ref[key], self.lat_ref[key])


def _split_specs(block, index_map, n_ctx_tiles):
    def ctx_map(*ids):
        r, *rest = index_map(*ids)
        return (jnp.minimum(r, n_ctx_tiles - 1), *rest)

    def lat_map(*ids):
        r, *rest = index_map(*ids)
        return (jnp.maximum(r - n_ctx_tiles, 0), *rest)

    return [pl.BlockSpec(block, ctx_map), pl.BlockSpec(block, lat_map)]


def _norm_mod_kernel(x_ref, g_ref, sh_ref, sc_ref, o_ref):
    o_ref[...] = _norm_mod_body(x_ref, g_ref, sh_ref, sc_ref).astype(o_ref.dtype)


def _norm_mod_split_kernel(xc_ref, xl_ref, g_ref, sh_ref, sc_ref, o_ref, *, n_ctx_tiles):
    x_ref = _SplitRows(xc_ref, xl_ref, pl.program_id(0) < n_ctx_tiles)
    o_ref[...] = _norm_mod_body(x_ref, g_ref, sh_ref, sc_ref).astype(o_ref.dtype)


def _norm_mod_router_kernel(x_ref, g_ref, sh_ref, sc_ref, wcat_ref, br_ref, o_ref, lg_ref):
    h = _norm_mod_body(x_ref, g_ref, sh_ref, sc_ref)
    h_hi = h.astype(BF16)
    o_ref[...] = h_hi
    n_e = lg_ref.shape[1]
    h_lo = (h - h_hi.astype(F32)).astype(BF16)
    a = jnp.dot(h_hi, wcat_ref[...], preferred_element_type=F32)
    b = jnp.dot(h_lo, wcat_ref[:, :n_e], preferred_element_type=F32)
    lg_ref[...] = a[:, :n_e] + a[:, n_e:] + b + br_ref[...]


def _mod_row(i, tm, n_ctx_rows, seq):
    n_ctx_tiles = n_ctx_rows // tm
    return jnp.where(i < n_ctx_tiles, 0, 1 + (i - n_ctx_tiles) // (seq // tm))


def _norm_mod(x, gain, mod, k_shift, k_scale, n_ctx_rows, seq, router=None, tm=256):
    split = isinstance(x, tuple)
    rows = sum(a.shape[0] for a in x) if split else x.shape[0]
    d = gain.shape[0]
    mrow = functools.partial(_mod_row, tm=tm, n_ctx_rows=n_ctx_rows, seq=seq)
    x_specs = (_split_specs((tm, d), lambda i: (i, 0), n_ctx_rows // tm) if split
               else [pl.BlockSpec((tm, d), lambda i: (i, 0))])
    in_specs = x_specs + [
        pl.BlockSpec((1, d), lambda i: (0, 0)),
        pl.BlockSpec((None, 1, d), lambda i: (mrow(i), 0, k_shift)),
        pl.BlockSpec((None, 1, d), lambda i: (mrow(i), 0, k_scale)),
    ]
    args = [*x, gain.reshape(1, d), mod, mod] if split else [x, gain.reshape(1, d), mod, mod]
    if router is None:
        kern = (functools.partial(_norm_mod_split_kernel, n_ctx_tiles=n_ctx_rows // tm) if split
                else _norm_mod_kernel)
        return pl.pallas_call(
            kern, grid=(rows // tm,), in_specs=in_specs,
            out_specs=pl.BlockSpec((tm, d), lambda i: (i, 0)),
            out_shape=jax.ShapeDtypeStruct((rows, d), BF16),
            compiler_params=_params(1), name="norm_mod")(*args)
    assert not split
    w_r, b_r = router
    n_e = w_r.shape[1]
    w_hi = w_r.astype(BF16)
    w_cat = jnp.concatenate([w_hi, (w_r - w_hi.astype(F32)).astype(BF16)], axis=1)
    in_specs += [pl.BlockSpec((d, 2 * n_e), lambda i: (0, 0)), pl.BlockSpec((1, n_e), lambda i: (0, 0))]
    return pl.pallas_call(
        _norm_mod_router_kernel, grid=(rows // tm,), in_specs=in_specs,
        out_specs=[pl.BlockSpec((tm, d), lambda i: (i, 0)), pl.BlockSpec((tm, n_e), lambda i: (i, 0))],
        out_shape=[jax.ShapeDtypeStruct((rows, d), BF16), jax.ShapeDtypeStruct((rows, n_e), F32)],
        compiler_params=_params(1), name="norm_mod_router")(*args, w_cat, b_r.reshape(1, n_e))


def _matmul_kernel(*refs, x_split, extra_split, n_out, epilogue, n_sub, n_ctx_tiles):
    is_ctx = pl.program_id(1) < n_ctx_tiles
    refs = list(refs)

    def take(split):
        if split:
            return _SplitRows(refs.pop(0), refs.pop(0), is_ctx)
        return refs.pop(0)

    xs = [take(sp) for sp in x_split]
    w_ref = refs.pop(0)
    extra = [take(sp) for sp in extra_split]
    outs, wbf_ref = refs[:n_out], refs[-1]
    tm = outs[0].shape[0]

    @pl.when(pl.program_id(1) == 0)
    def _():
        wbf_ref[...] = w_ref[...].astype(BF16)

    sub = tm // n_sub

    def dot(r):
        acc, k0 = None, 0
        for x in xs:
            part = jnp.dot(x[r * sub:(r + 1) * sub, :], wbf_ref[k0:k0 + x.shape[1], :], preferred_element_type=F32)
            acc = part if acc is None else acc + part
            k0 += x.shape[1]
        return acc

    acc_next = dot(0)
    for r in range(n_sub):
        acc = acc_next
        if r + 1 < n_sub:
            acc_next = dot(r + 1)
        rows = slice(r * sub, (r + 1) * sub)
        epilogue(acc, [_RowView(e, rows) if e.shape[0] == tm else e for e in extra],
                 [_RowView(o, rows) for o in outs])


class _RowView:
    def __init__(self, ref, rows):
        self.ref, self.rows = ref, rows
        self.shape = (rows.stop - rows.start,) + tuple(ref.shape[1:])
        self.dtype = ref.dtype

    def _key(self, key):
        if key is Ellipsis:
            return (self.rows, slice(None))
        assert isinstance(key, tuple) and key[0] == slice(None), key
        return (self.rows,) + tuple(key[1:])

    def __getitem__(self, key):
        return self.ref[self._key(key)]

    def __setitem__(self, key, value):
        self.ref[self._key(key)] = value


def _matmul(x, w, *, w_lead=(), col0=0, n_cols, tm, tn, epilogue, extras=(), outs, name, row0=0, m=None):
    parts = x if isinstance(x, list) else [x]
    n_rows = lambda a: sum(b.shape[0] for b in a) if isinstance(a, tuple) else a.shape[0]
    width = lambda a: a[0].shape[1] if isinstance(a, tuple) else a.shape[1]
    splits = [a for a in parts + [e[0] for e in extras] if isinstance(a, tuple)]
    n_ctx_tiles = splits[0][0].shape[0] // tm if splits else 0
    assert all(a[0].shape[0] == n_ctx_tiles * tm for a in splits) and (row0 == 0 or not splits)
    k = sum(width(a) for a in parts)
    m = n_rows(parts[0]) - row0 if m is None else m
    assert m % tm == 0 and row0 % tm == 0 and n_cols % tn == 0 and col0 % tn == 0
    lead = tuple(w_lead)

    def specs(a, block, index_map):
        if isinstance(a, tuple):
            return _split_specs(block, index_map, n_ctx_tiles)
        return [pl.BlockSpec(block, index_map)]

    in_specs, args = [], []
    for a in parts:
        in_specs += specs(a, (tm, width(a)), lambda j, i: (row0 // tm + i, 0))
        args += list(a) if isinstance(a, tuple) else [a]
    in_specs.append(pl.BlockSpec((None,) * len(lead) + (k, tn), lambda j, i: lead + (0, col0 // tn + j)))
    args.append(w)
    for a, bs, im in extras:
        in_specs += specs(a, bs, im)
        args += list(a) if isinstance(a, tuple) else [a]
    out_specs = [pl.BlockSpec(bs, im) for _, bs, im in outs]
    kern = functools.partial(
        _matmul_kernel, x_split=tuple(isinstance(a, tuple) for a in parts),
        extra_split=tuple(isinstance(e[0], tuple) for e in extras), n_out=len(outs), epilogue=epilogue,
        n_sub=max(1, tm // MATMUL_SUB_ROWS), n_ctx_tiles=n_ctx_tiles)
    return pl.pallas_call(
        kern, grid=(n_cols // tn, m // tm), in_specs=in_specs, out_specs=out_specs,
        out_shape=[s for s, _, _ in outs],
        scratch_shapes=[pltpu.VMEM((k, tn), BF16)],
        compiler_params=_params(2), name=name)(*args)


def _ep_bias(acc, extra, outs):
    outs[0][...] = acc + extra[0][...]


def _ep_cast(acc, extra, outs):
    outs[0][...] = acc.astype(outs[0].dtype)


def _ep_gelu(acc, extra, outs):
    outs[0][...] = jax.nn.gelu(acc).astype(outs[0].dtype)


def _ep_head_rms(acc, extra, outs):
    g = extra[0][...]
    for h in range(acc.shape[1] // LANE):
        a = acc[:, h * LANE:(h + 1) * LANE]
        r = lax.rsqrt(jnp.mean(a * a, axis=-1, keepdims=True) + EPS)
        outs[0][:, h * LANE:(h + 1) * LANE] = (a * r * g).astype(outs[0].dtype)


def _ep_residual(acc, extra, outs):
    outs[0][...] = extra[0][...] + extra[1][...] * acc


def _rms(a, g):
    return a * lax.rsqrt(jnp.mean(a * a, axis=-1, keepdims=True) + EPS) * g


def _ep_mla_in(acc, extra, outs):
    outs[0][...] = _rms(acc[:, :Q_LORA], extra[0][...]).astype(BF16)
    outs[1][...] = _rms(acc[:, Q_LORA:Q_LORA + KV_LORA], extra[1][...]).astype(BF16)
    outs[2][...] = acc[:, Q_LORA + KV_LORA:]


def _rope_tail(t, tab_ref):
    return t * tab_ref[:, :LANE] + pltpu.roll(t, LANE // 2, axis=1) * tab_ref[:, LANE:]


def _ep_mla_q(acc, extra, outs):
    g_ref, tab_ref = extra
    g0 = g_ref[:, :LANE]
    g1 = g_ref[:, LANE:]
    for h in range(acc.shape[1] // QK_PAD):
        a0 = acc[:, h * QK_PAD:h * QK_PAD + LANE]
        a1 = acc[:, h * QK_PAD + LANE:(h + 1) * QK_PAD]
        r = lax.rsqrt(jnp.sum(a0 * a0 + a1 * a1, axis=-1, keepdims=True) * (1.0 / QK_DIM) + EPS)
        outs[0][:, h * QK_PAD:h * QK_PAD + LANE] = (a0 * r * g0).astype(BF16)
        outs[0][:, h * QK_PAD + LANE:(h + 1) * QK_PAD] = _rope_tail(a1 * r * g1, tab_ref).astype(BF16)


def _ep_mla_kv(acc, extra, outs):
    g_ref, tab_ref, pe_ref = extra
    k_out, v_out = outs
    g0 = g_ref[:, :LANE]
    pe = pe_ref[...]
    pe_sq = pe * pe
    pe_roped = _rope_tail(pe * g_ref[:, LANE:], tab_ref)
    for h in range(acc.shape[1] // (QK_NOPE + V_DIM)):
        base = h * (QK_NOPE + V_DIM)
        kn = acc[:, base:base + QK_NOPE]
        r = lax.rsqrt(jnp.sum(kn * kn + pe_sq, axis=-1, keepdims=True) * (1.0 / QK_DIM) + EPS)
        k_out[:, h * QK_PAD:h * QK_PAD + LANE] = (kn * r * g0).astype(BF16)
        k_out[:, h * QK_PAD + LANE:(h + 1) * QK_PAD] = (pe_roped * r).astype(BF16)
        v_out[:, h * V_DIM:(h + 1) * V_DIM] = acc[:, base + QK_NOPE:base + QK_NOPE + V_DIM].astype(BF16)


def _softmax_pv(s_parts, v_parts):
    m = s_parts[0].max(axis=-1, keepdims=True)
    for s in s_parts[1:]:
        m = jnp.maximum(m, s.max(axis=-1, keepdims=True))
    l = 0.0
    o = 0.0
    for s, v in zip(s_parts, v_parts):
        p = jnp.exp(s - m)
        l = l + p.sum(axis=-1, keepdims=True)
        o = o + jnp.dot(p.astype(BF16), v, preferred_element_type=F32)
    return o / l


def _qk(q, k):
    return lax.dot_general(q, k, (((1,), (1,)), ((), ())), preferred_element_type=F32)


def _attn_kernel(*refs, has_ctx, n_sub):
    if has_ctx:
        q_ref, k_ref, v_ref, kc_ref, vc_ref, o_ref = refs
    else:
        q_ref, k_ref, v_ref, o_ref = refs
    sub = q_ref.shape[0] // n_sub

    def scores(r):
        q = q_ref[r * sub:(r + 1) * sub, :]
        parts = [_qk(q, k_ref[...])]
        if has_ctx:
            parts.append(_qk(q, kc_ref[...]))
        return parts

    s_next = scores(0)
    for r in range(n_sub):
        s_cur = s_next
        if r + 1 < n_sub:
            s_next = scores(r + 1)
        v_parts = [v_ref[...]] + ([vc_ref[...]] if has_ctx else [])
        o_ref[r * sub:(r + 1) * sub, :] = _softmax_pv(s_cur, v_parts).astype(o_ref.dtype)


def _attention(q, k, v, *, n_batch, n_heads, dq, dv, tq, n_q_tiles, q_blk0, t_k, k_blk0, t_ctx=None, n_sub=1, name):
    has_ctx = t_ctx is not None
    in_specs = [
        pl.BlockSpec((tq, dq), lambda b, h, i: (q_blk0 + b * n_q_tiles + i, h)),
        pl.BlockSpec((t_k, dq), lambda b, h, i: (k_blk0 + b, h)),
        pl.BlockSpec((t_k, dv), lambda b, h, i: (k_blk0 + b, h)),
    ]
    args = [q, k, v]
    if has_ctx:
        in_specs += [pl.BlockSpec((t_ctx, dq), lambda b, h, i: (b, h)),
                     pl.BlockSpec((t_ctx, dv), lambda b, h, i: (b, h))]
        args += [k, v]
    return pl.pallas_call(
        functools.partial(_attn_kernel, has_ctx=has_ctx, n_sub=n_sub),
        grid=(n_batch, n_heads, n_q_tiles), in_specs=in_specs,
        out_specs=pl.BlockSpec((tq, dv), lambda b, h, i: (b * n_q_tiles + i, h)),
        out_shape=jax.ShapeDtypeStruct((n_batch * n_q_tiles * tq, n_heads * dv), BF16),
        compiler_params=_params(3), name=name)(*args)


def _na_band_start(t, rows):
    return jnp.clip(t * NA_TILE_ROWS - NA_WIN_H // 2, 0, rows - NA_BAND_ROWS)


def _na_kernel(q_ref, k_ref, v_ref, kc_ref, vc_ref, bias_ref, o_ref, *, rows):
    t = pl.program_id(1)
    ks = pl.multiple_of(_na_band_start(t, rows) * GRID_W, GRID_W)
    band = pl.ds(ks, NA_BAND_ROWS * GRID_W)

    def scores(h):
        cols = slice(h * NA_HEAD_DIM, (h + 1) * NA_HEAD_DIM)
        q = q_ref[:, cols]
        return [_qk(q, k_ref[band, cols]) + bias_ref[h], _qk(q, kc_ref[:, cols])]

    s_next = scores(0)
    for h in range(NA_HEADS):
        s_cur = s_next
        if h + 1 < NA_HEADS:
            s_next = scores(h + 1)
        cols = slice(h * NA_HEAD_DIM, (h + 1) * NA_HEAD_DIM)
        o_ref[:, cols] = _softmax_pv(s_cur, [v_ref[band, cols], vc_ref[:, cols]]).astype(o_ref.dtype)


def _na_bias_tables(rel_bias, rows):
    n_tiles = rows // NA_TILE_ROWS
    n_r, n_c = 2 * NA_WIN_H - 1, 2 * NA_WIN_W - 1
    qr = np.arange(NA_TILE_ROWS)[:, None]
    ur = np.arange(NA_BAND_ROWS)[None, :]
    qc = np.arange(GRID_W)[:, None]
    kc = np.arange(GRID_W)[None, :]
    c0 = np.clip(qc - NA_WIN_W // 2, 0, GRID_W - NA_WIN_W)
    col_valid = (kc >= c0) & (kc < c0 + NA_WIN_W)
    col_onehot = np.eye(n_c, dtype=np.float32)[np.clip(kc - qc + NA_WIN_W - 1, 0, n_c - 1)]
    geoms, ids = [], []
    for t in range(n_tiles):
        u0 = int(np.clip(t * NA_TILE_ROWS - NA_WIN_H // 2, 0, rows - NA_BAND_ROWS))
        r = t * NA_TILE_ROWS + qr
        key_row = u0 + ur
        r0 = np.clip(r - NA_WIN_H // 2, 0, rows - NA_WIN_H)
        geom = ((key_row >= r0) & (key_row < r0 + NA_WIN_H), np.clip(key_row - r + NA_WIN_H - 1, 0, n_r - 1))
        for gi, g in enumerate(geoms):
            if all(np.array_equal(a, b) for a, b in zip(g, geom)):
                ids.append(gi)
                break
        else:
            ids.append(len(geoms))
            geoms.append(geom)
    row_valid = np.stack([g[0] for g in geoms])
    row_onehot = np.eye(n_r, dtype=np.float32)[np.stack([g[1] for g in geoms])]
    bias = jnp.einsum('gaur,hrc,bkc->ghabuk', row_onehot, rel_bias.astype(F32), col_onehot,
                      precision=lax.Precision.HIGHEST)
    valid = row_valid[:, None, :, None, :, None] & col_valid[None, None, None, :, None, :]
    tables = jnp.where(valid, bias, NEG)
    return tables.reshape(len(geoms), NA_HEADS, NA_TILE_ROWS * GRID_W, NA_BAND_ROWS * GRID_W), ids


def _na_attention(q, k, v, rel_bias, *, n_batch, seq, t_ctx):
    rows = seq // GRID_W
    n_tiles = rows // NA_TILE_ROWS
    tq = NA_TILE_ROWS * GRID_W
    tables, ids = _na_bias_tables(rel_bias, rows)
    assert ids == [0] + [1] * (n_tiles - 2) + [2], ids
    ctx_tiles = n_batch * t_ctx // tq
    ctx_units = n_batch * t_ctx // seq

    def table_id(t):
        return jnp.where(t == 0, 0, jnp.where(t == n_tiles - 1, 2, 1))

    return pl.pallas_call(
        functools.partial(_na_kernel, rows=rows),
        grid=(n_batch, n_tiles),
        in_specs=[
            pl.BlockSpec((tq, NA_WIDTH), lambda b, t: (ctx_tiles + b * n_tiles + t, 0)),
            pl.BlockSpec((seq, NA_WIDTH), lambda b, t: (ctx_units + b, 0)),
            pl.BlockSpec((seq, NA_WIDTH), lambda b, t: (ctx_units + b, 0)),
            pl.BlockSpec((t_ctx, NA_WIDTH), lambda b, t: (b, 0)),
            pl.BlockSpec((t_ctx, NA_WIDTH), lambda b, t: (b, 0)),
            pl.BlockSpec((None, NA_HEADS, tq, NA_BAND_ROWS * GRID_W), lambda b, t: (table_id(t), 0, 0, 0)),
        ],
        out_specs=pl.BlockSpec((tq, NA_WIDTH), lambda b, t: (b * n_tiles + t, 0)),
        out_shape=jax.ShapeDtypeStruct((n_batch * seq, NA_WIDTH), BF16),
        compiler_params=_params(2), name="na_attention")(q, k, v, k, v, tables)


def _sg_kernel(uv_ref, g_ref, b_ref, ws_ref, bs_ref, o_ref, *, chunks):
    for c in range(chunks):
        r = slice(c * SG_CHUNK, (c + 1) * SG_CHUNK)
        z = uv_ref[r, SG_WIDTH:].astype(F32)
        mu = jnp.mean(z, axis=-1, keepdims=True)
        zc = z - mu
        var = jnp.mean(zc * zc, axis=-1, keepdims=True)
        zn = (zc * lax.rsqrt(var + EPS) * g_ref[...] + b_ref[...]).astype(BF16)
        for g in range(SG_GROUPS):
            cols = slice(g * SG_GROUP_DIM, (g + 1) * SG_GROUP_DIM)
            mixed = jnp.dot(ws_ref[g], zn[:, cols], preferred_element_type=F32) + bs_ref[g]
            o_ref[r, cols] = (uv_ref[r, cols].astype(F32) * mixed).astype(o_ref.dtype)


def _spatial_gating(uv, ln_g, ln_b, w_s, b_s, tm=512):
    rows = uv.shape[0]
    bs = jnp.broadcast_to(b_s.astype(F32)[:, :, None], (SG_GROUPS, SG_CHUNK, SG_GROUP_DIM))
    return pl.pallas_call(
        functools.partial(_sg_kernel, chunks=tm // SG_CHUNK),
        grid=(rows // tm,),
        in_specs=[
            pl.BlockSpec((tm, 2 * SG_WIDTH), lambda i: (i, 0)),
            pl.BlockSpec((1, SG_WIDTH), lambda i: (0, 0)),
            pl.BlockSpec((1, SG_WIDTH), lambda i: (0, 0)),
            pl.BlockSpec((SG_GROUPS, SG_CHUNK, SG_CHUNK), lambda i: (0, 0, 0)),
            pl.BlockSpec((SG_GROUPS, SG_CHUNK, SG_GROUP_DIM), lambda i: (0, 0, 0)),
        ],
        out_specs=pl.BlockSpec((tm, SG_WIDTH), lambda i: (i, 0)),
        out_shape=jax.ShapeDtypeStruct((rows, SG_WIDTH), BF16),
        compiler_params=_params(1), name="spatial_gating")(
            uv, ln_g.reshape(1, SG_WIDTH), ln_b.reshape(1, SG_WIDTH), w_s.astype(BF16), bs)


def _swiglu(g, u):
    g = jnp.minimum(g, SWIGLU_LIMIT)
    u = jnp.clip(u, -SWIGLU_LIMIT, SWIGLU_LIMIT)
    return (u + 1.0) * (g * jax.nn.sigmoid(SWIGLU_ALPHA * g))


def _for_each_valid_part(i, tv_ref, o_ref, compute):
    part = o_ref.shape[0] // MOE_PARTS
    n_valid = (tv_ref[i] + part - 1) // part
    rows = [slice(p * part, (p + 1) * part) for p in range(MOE_PARTS)]
    for n in range(MOE_PARTS + 1):

        @pl.when(n_valid == n)
        def _(n=n):
            acc_next = compute.matmul(rows[0]) if n else None
            for p in range(n):
                acc = acc_next
                if p + 1 < n:
                    acc_next = compute.matmul(rows[p + 1])
                o_ref[rows[p], :] = compute.epilogue(acc).astype(o_ref.dtype)
            for p in range(n, MOE_PARTS):
                o_ref[rows[p], :] = jnp.zeros((part, o_ref.shape[1]), o_ref.dtype)


class _GateUp:
    def __init__(self, x_ref, w_bf, bg_ref, bu_ref):
        self.x_ref, self.w_bf, self.bg_ref, self.bu_ref = x_ref, w_bf, bg_ref, bu_ref

    def matmul(self, rows):
        return jnp.dot(self.x_ref[rows, :], self.w_bf[...], preferred_element_type=F32)

    def epilogue(self, acc):
        tn = acc.shape[1] // 2
        return _swiglu(acc[:, :tn] + self.bg_ref[...], acc[:, tn:] + self.bu_ref[...])


class _Down:
    def __init__(self, h_ref, w_bf, b_ref):
        self.h_ref, self.w_bf, self.b_ref = h_ref, w_bf, b_ref

    def matmul(self, rows):
        return jnp.dot(self.h_ref[rows, :], self.w_bf[...], preferred_element_type=F32)

    def epilogue(self, acc):
        return acc + self.b_ref[...]


def _expert_weight_copies(w_hbm, land, sem, layer, expert, cols):
    tn = land.shape[1] // len(cols)
    return [pltpu.make_async_copy(w_hbm.at[layer, expert, :, pl.ds(pl.multiple_of(col, tn), tn)],
                                  land.at[:, pl.ds(k * tn, tn)], sem.at[k])
            for k, col in enumerate(cols)]


def _cast_rows(src, dst, chunk=256):
    def body(c, carry):
        rows = pl.ds(pl.multiple_of(c * chunk, chunk), chunk)
        dst[rows, :] = src[rows, :].astype(dst.dtype)
        return carry
    lax.fori_loop(0, src.shape[0] // chunk, body, 0)


def _refresh_expert_weights(te_ref, first_ref, nxt_ref, w_hbm, land, bf, sem, layer, col_fn, n_j):
    j = pl.program_id(0)
    i = pl.program_id(1)

    @pl.when(first_ref[i] == 1)
    def _():
        cur = _expert_weight_copies(w_hbm, land, sem, layer, te_ref[i], col_fn(j))

        @pl.when((i == 0) & (j == 0))
        def _():
            for c in cur:
                c.start()

        for c in cur:
            c.wait()
        _cast_rows(land, bf)
        nxt = nxt_ref[i]

        @pl.when(nxt >= 0)
        def _():
            for c in _expert_weight_copies(w_hbm, land, sem, layer, nxt, col_fn(j)):
                c.start()

        @pl.when((nxt < 0) & (j + 1 < n_j))
        def _():
            for c in _expert_weight_copies(w_hbm, land, sem, layer, te_ref[0], col_fn(j + 1)):
                c.start()


def _moe_gu_kernel(te_ref, tv_ref, ts_ref, first_ref, nxt_ref, x_ref, w_hbm, bg_ref, bu_ref, o_ref, w_land, w_bf, sem,
                   *, layer, n_j):
    tn = w_bf.shape[1] // 2
    _refresh_expert_weights(te_ref, first_ref, nxt_ref, w_hbm, w_land, w_bf, sem, layer,
                            lambda j: (j * tn, D_FF + j * tn), n_j)
    _for_each_valid_part(pl.program_id(1), tv_ref, o_ref, _GateUp(x_ref, w_bf, bg_ref, bu_ref))


def _moe_down_kernel(te_ref, tv_ref, ts_ref, first_ref, nxt_ref, ha_ref, hb_ref, w_hbm, b_ref, o_ref, w_land, w_bf, sem,
                     *, layer, n_j, n_first_half):
    tn = w_bf.shape[1]
    _refresh_expert_weights(te_ref, first_ref, nxt_ref, w_hbm, w_land, w_bf, sem, layer,
                            lambda j: (j * tn,), n_j)
    i = pl.program_id(1)
    h_ref = _SplitRows(ha_ref, hb_ref, ts_ref[i] < n_first_half)
    _for_each_valid_part(i, tv_ref, o_ref, _Down(h_ref, w_bf, b_ref))


def _expert_runs(tile_e):
    n_tiles = tile_e.shape[0]
    idx = jnp.arange(n_tiles, dtype=jnp.int32)
    first = jnp.concatenate([jnp.ones((1,), bool), tile_e[1:] != tile_e[:-1]])
    later_first = lax.cummin(jnp.where(first, idx, n_tiles)[::-1])[::-1]
    next_first = jnp.concatenate([later_first[1:], jnp.full((1,), n_tiles, jnp.int32)])
    nxt = jnp.where(next_first < n_tiles, tile_e[jnp.minimum(next_first, n_tiles - 1)], -1)
    return first.astype(jnp.int32), nxt.astype(jnp.int32)


def _moe_experts(x_halves, tile_e, tile_valid, tile_src, layer, w_gu, b_gu, w_down, b_down):
    tm, tn, tn_down = MOE_TM, MOE_TN_GU, MOE_TN_DOWN
    d = x_halves[0].shape[1]
    n_a = x_halves[0].shape[0] // tm
    n_tiles = n_a + x_halves[1].shape[0] // tm
    n_j = D_FF // tn
    n_jd = d // tn_down
    x_map = lambda j, i, te, tv, ts, fr, nx: (ts[i], 0)
    out_map = lambda j, i, te, tv, ts, fr, nx: (i, j)

    hid = []
    for x_half, t0, t1 in ((x_halves[0], 0, n_a), (x_halves[1], n_a, n_tiles)):
        te = tile_e[t0:t1]
        src = jnp.clip(tile_src[t0:t1] - t0, 0, t1 - t0 - 1)
        hid.append(pl.pallas_call(
            functools.partial(_moe_gu_kernel, layer=layer, n_j=n_j),
            grid_spec=pltpu.PrefetchScalarGridSpec(
                num_scalar_prefetch=5, grid=(n_j, t1 - t0),
                in_specs=[
                    pl.BlockSpec((tm, d), x_map),
                    pl.BlockSpec(memory_space=pl.ANY),
                    pl.BlockSpec((None, None, 1, tn), lambda j, i, te, tv, ts, fr, nx: (layer, te[i], 0, j)),
                    pl.BlockSpec((None, None, 1, tn), lambda j, i, te, tv, ts, fr, nx: (layer, te[i], 0, n_j + j)),
                ],
                out_specs=pl.BlockSpec((tm, tn), out_map),
                scratch_shapes=[pltpu.VMEM((d, 2 * tn), F32), pltpu.VMEM((d, 2 * tn), BF16),
                                pltpu.SemaphoreType.DMA((2,))]),
            out_shape=jax.ShapeDtypeStruct(((t1 - t0) * tm, D_FF), BF16),
            compiler_params=_params(2), name="moe_gate_up")(
                te, tile_valid[t0:t1], src, *_expert_runs(te), x_half, w_gu, b_gu[:, :, None, :], b_gu[:, :, None, :]))

    return pl.pallas_call(
        functools.partial(_moe_down_kernel, layer=layer, n_j=n_jd, n_first_half=n_a),
        grid_spec=pltpu.PrefetchScalarGridSpec(
            num_scalar_prefetch=5, grid=(n_jd, n_tiles),
            in_specs=[
                pl.BlockSpec((tm, D_FF), lambda j, i, te, tv, ts, fr, nx: (jnp.minimum(ts[i], n_a - 1), 0)),
                pl.BlockSpec((tm, D_FF), lambda j, i, te, tv, ts, fr, nx: (jnp.maximum(ts[i] - n_a, 0), 0)),
                pl.BlockSpec(memory_space=pl.ANY),
                pl.BlockSpec((None, None, 1, tn_down), lambda j, i, te, tv, ts, fr, nx: (layer, te[i], 0, j)),
            ],
            out_specs=pl.BlockSpec((tm, tn_down), out_map),
            scratch_shapes=[pltpu.VMEM((D_FF, tn_down), F32), pltpu.VMEM((D_FF, tn_down), BF16),
                            pltpu.SemaphoreType.DMA((1,))]),
        out_shape=jax.ShapeDtypeStruct((n_tiles * tm, d), BF16),
        compiler_params=_params(2), name="moe_down")(
            tile_e, tile_valid, tile_src, *_expert_runs(tile_e), hid[0], hid[1], w_down, b_down[:, :, None, :])


def _moe_route(logits):
    n = logits.shape[0]
    nk = n * TOP_K
    tm = MOE_TM
    i32 = jnp.int32
    top_val, top_idx = lax.top_k(logits, TOP_K)
    gate = jax.nn.softmax(top_val, axis=-1)
    flat_e = top_idx.reshape(nk).astype(i32)
    iota = jnp.arange(nk, dtype=i32)
    bits = (nk - 1).bit_length()
    assert (N_EXPERTS << bits) < 2 ** 31
    order = lax.sort((flat_e << bits) | iota) & ((1 << bits) - 1)
    _, inv_order = lax.sort((order, iota), num_keys=1)
    onehot = flat_e[:, None] == jnp.arange(N_EXPERTS, dtype=i32)[None, :]
    counts = jnp.sum(onehot, axis=0, dtype=i32)
    padded = (counts + tm - 1) // tm * tm
    pad_end = jnp.cumsum(padded)
    pad_start = pad_end - padded
    start = jnp.cumsum(counts) - counts
    pos = inv_order + jnp.sum(jnp.where(onehot, (pad_start - start)[None, :], 0), axis=1)
    pos = pos.reshape(n, TOP_K).T.reshape(nk)

    n_tiles = nk // tm + N_EXPERTS
    n_used = pad_end[-1] // tm
    tile_src = jnp.minimum(jnp.arange(n_tiles, dtype=i32), n_used - 1)
    tile_e = jnp.minimum(jnp.sum(pad_end[None, :] <= (tile_src * tm)[:, None], axis=1, dtype=i32), N_EXPERTS - 1)
    in_use = jnp.arange(n_tiles, dtype=i32) < n_used
    rank0 = tile_src * tm - pad_start[tile_e]
    tile_valid = jnp.where(in_use, jnp.clip(counts[tile_e] - rank0, 0, tm), 0).astype(i32)
    within = jnp.arange(tm, dtype=i32)[None, :]
    valid = within < tile_valid[:, None]
    src = jnp.clip((start[tile_e] + rank0)[:, None] + within, 0, nk - 1)
    filler = (jnp.arange(n_tiles * tm, dtype=i32) % n).reshape(n_tiles, tm)
    rows_tok = jnp.where(valid, order.at[src].get(mode="promise_in_bounds") // TOP_K, filler).reshape(n_tiles * tm)
    return gate, rows_tok, pos, tile_e, tile_valid, tile_src.astype(i32)


def _moe_combine(x_ref, g2_ref, gate_ref, y_ref):
    gate = gate_ref[...]
    f = gate[:, 0:1] * y_ref[0].astype(F32)
    for k in range(1, TOP_K):
        f = f + gate[:, k:k + 1] * y_ref[k].astype(F32)
    return x_ref[...] + g2_ref[...] * f


def _moe_combine_kernel(x_ref, g2_ref, gate_ref, y_ref, o_ref):
    o_ref[...] = _moe_combine(x_ref, g2_ref, gate_ref, y_ref)


def _moe_combine_norm_kernel(x_ref, g2_ref, gate_ref, y_ref, ng_ref, nsh_ref, nsc_ref, o_ref, h_ref):
    x = _moe_combine(x_ref, g2_ref, gate_ref, y_ref)
    o_ref[...] = x
    h_ref[...] = _norm_mod_val(x, ng_ref[...], nsh_ref[...], nsc_ref[...]).astype(h_ref.dtype)


def _moe_block(xs, h2, logits, mod, layer, n_ctx_rows, seq, w_gu, b_gu, w_down, b_down, next_norm=None, tm=256):
    n, d = h2.shape
    gate, rows_tok, pos, tile_e, tile_valid, tile_src = _moe_route(logits)
    half = tile_e.shape[0] // 4 * MOE_TM
    x_halves = [h2.at[r].get(mode="promise_in_bounds") for r in (rows_tok[:half], rows_tok[half:])]
    y = _moe_experts(x_halves, tile_e, tile_valid, tile_src, layer, w_gu, b_gu, w_down, b_down)
    y_k = y.at[pos].get(mode="promise_in_bounds").reshape(TOP_K, n, d)
    mrow = functools.partial(_mod_row, tm=tm, n_ctx_rows=n_ctx_rows, seq=seq)
    row_spec = pl.BlockSpec((tm, d), lambda i: (i, 0))
    mod_spec = lambda k: pl.BlockSpec((None, 1, d), lambda i: (mrow(i), 0, k))
    in_specs = [row_spec, mod_spec(5), pl.BlockSpec((tm, TOP_K), lambda i: (i, 0)),
                pl.BlockSpec((TOP_K, tm, d), lambda i: (0, i, 0))]
    if next_norm is None:
        return pl.pallas_call(
            _moe_combine_kernel, grid=(n // tm,), in_specs=in_specs, out_specs=row_spec,
            out_shape=jax.ShapeDtypeStruct((n, d), F32),
            compiler_params=_params(1), name="moe_combine")(xs, mod, gate, y_k)
    gain, next_mod = next_norm
    return pl.pallas_call(
        _moe_combine_norm_kernel, grid=(n // tm,),
        in_specs=in_specs + [pl.BlockSpec((1, d), lambda i: (0, 0)), mod_spec(0), mod_spec(1)],
        out_specs=[row_spec, row_spec],
        out_shape=[jax.ShapeDtypeStruct((n, d), F32), jax.ShapeDtypeStruct((n, d), BF16)],
        compiler_params=_params(1), name="moe_combine_norm")(
            xs, mod, gate, y_k, gain.reshape(1, d), next_mod, next_mod)


def _rope_tail_layout(x):
    half = QK_ROPE // 2
    z = jnp.zeros(x.shape[:-1] + (LANE // 2 - half,), x.dtype)
    return jnp.concatenate([x[..., :half], z, x[..., half:], z], axis=-1)


def _axial_rope_tables(seq, tm):
    t = jnp.arange(seq)
    row = (t // GRID_W).astype(F32)
    col = (t % GRID_W).astype(F32)
    n_freq = QK_ROPE // 4
    inv = ROPE_THETA ** (-jnp.arange(n_freq, dtype=F32) / n_freq)
    ang = jnp.concatenate([row[:, None] * inv, col[:, None] * inv], axis=-1)
    cos, sin = jnp.cos(ang), jnp.sin(ang)
    tab = jnp.concatenate([_rope_tail_layout(jnp.concatenate([cos, cos], axis=-1)),
                           _rope_tail_layout(jnp.concatenate([-sin, sin], axis=-1))], axis=-1)
    ident = jnp.concatenate([_rope_tail_layout(jnp.ones((tm, QK_ROPE), F32)), jnp.zeros((tm, LANE), F32)], axis=-1)
    return jnp.concatenate([ident, tab], axis=0)


def _ada_mod(cvec, ada_w, ada_b, layer):
    n = cvec.shape[0]
    m_pad = 16
    a = jnp.zeros((m_pad, D_MODEL), F32).at[:n].set(jax.nn.silu(cvec)).astype(BF16)
    n_cols = ada_w.shape[-1]
    tn = 1024
    (mod,) = _matmul(
        a, ada_w, w_lead=(layer,), n_cols=n_cols, tm=m_pad, tn=tn, epilogue=_ep_bias,
        extras=[(ada_b[:, None, :], (None, 1, tn), lambda j, i: (layer, 0, j))],
        outs=[(jax.ShapeDtypeStruct((m_pad, n_cols), F32), (m_pad, tn), lambda j, i: (i, j))],
        name="ada_mod")
    return mod[:n, None, :]


def _mla_project(h, n_ctx, seq, tm, mla_w_in, mla_q_norm_g, mla_kv_norm_g, mla_w_uq, mla_w_ukv, mla_q_g, mla_k_g):
    n_all = h.shape[0]
    n_lat = n_all - n_ctx
    w_in_pad = jnp.concatenate([mla_w_in[..., :Q_LORA + KV_LORA], _rope_tail_layout(mla_w_in[..., Q_LORA + KV_LORA:])],
                               axis=-1)
    c_q, c_kv, k_pe = _matmul(
        h, w_in_pad, w_lead=(0,), n_cols=MLA_IN_PAD, tm=tm, tn=MLA_IN_PAD, epilogue=_ep_mla_in,
        extras=[(mla_q_norm_g[0].reshape(1, Q_LORA), (1, Q_LORA), lambda j, i: (0, 0)),
                (mla_kv_norm_g[0].reshape(1, KV_LORA), (1, KV_LORA), lambda j, i: (0, 0))],
        outs=[(jax.ShapeDtypeStruct((n_all, Q_LORA), BF16), (tm, Q_LORA), lambda j, i: (i, 0)),
              (jax.ShapeDtypeStruct((n_all, KV_LORA), BF16), (tm, KV_LORA), lambda j, i: (i, 0)),
              (jax.ShapeDtypeStruct((n_all, LANE), F32), (tm, LANE), lambda j, i: (i, 0))],
        name="mla_in")

    rope_tab = _axial_rope_tables(seq, tm)
    n_ctx_tiles = n_ctx // tm
    seq_tiles = seq // tm
    pad_head = lambda a: jnp.concatenate([a[..., :QK_NOPE], _rope_tail_layout(a[..., QK_NOPE:])], axis=-1)
    pad_gain = lambda g, s: pad_head(g * s).reshape(1, QK_PAD)
    w_uq = pad_head(mla_w_uq[0].reshape(Q_LORA, MLA_HEADS, QK_DIM)).reshape(Q_LORA, MLA_HEADS * QK_PAD)
    tn_up = 1024
    (q,) = _matmul(
        c_q, w_uq, row0=n_ctx, n_cols=MLA_HEADS * QK_PAD, tm=tm, tn=tn_up, epilogue=_ep_mla_q,
        extras=[(pad_gain(mla_q_g[0], QK_DIM ** -0.5), (1, QK_PAD), lambda j, i: (0, 0)),
                (rope_tab, (tm, 2 * LANE), lambda j, i: (1 + i % seq_tiles, 0))],
        outs=[(jax.ShapeDtypeStruct((n_lat, MLA_HEADS * QK_PAD), BF16), (tm, tn_up), lambda j, i: (i, j))],
        name="mla_up_q")
    heads_per_tile = tn_up // (QK_NOPE + V_DIM)
    k, v = _matmul(
        c_kv, mla_w_ukv, w_lead=(0,), n_cols=MLA_HEADS * (QK_NOPE + V_DIM), tm=tm, tn=tn_up, epilogue=_ep_mla_kv,
        extras=[(pad_gain(mla_k_g[0], 1.0), (1, QK_PAD), lambda j, i: (0, 0)),
                (rope_tab, (tm, 2 * LANE),
                 lambda j, i: (jnp.where(i < n_ctx_tiles, 0, 1 + (i - n_ctx_tiles) % seq_tiles), 0)),
                (k_pe, (tm, LANE), lambda j, i: (i, 0))],
        outs=[(jax.ShapeDtypeStruct((n_all, MLA_HEADS * QK_PAD), BF16), (tm, heads_per_tile * QK_PAD),
               lambda j, i: (i, j)),
              (jax.ShapeDtypeStruct((n_all, MLA_HEADS * V_DIM), BF16), (tm, heads_per_tile * V_DIM),
               lambda j, i: (i, j))],
        name="mla_up_kv")
    return q, k, v


def kernel(x, c, ctx, c_ctx, ada_w, ada_b, norm1_g, norm2_g, ab_w_in, ab_w_out, na_q_g, na_k_g, na_rel_bias,
           sg_norm_g, sg_norm_b, sg_w, sg_b, mla_w_in, mla_q_norm_g, mla_kv_norm_g, mla_w_uq, mla_w_ukv,
           mla_q_g, mla_k_g, mla_w_out, moe_w_router, moe_b_router, moe_w_gu, moe_b_gu, moe_w_down, moe_b_down):
    n_batch, seq, d = x.shape
    t_ctx = ctx.shape[1]
    n_ctx = n_batch * t_ctx
    n_lat = n_batch * seq
    n_all = n_ctx + n_lat
    tm = MATMUL_TM
    tn = 512
    cvec = jnp.concatenate([c_ctx[None, :], c], axis=0)
    xs = (ctx.reshape(n_ctx, d), x.reshape(n_lat, d))
    mod0 = _ada_mod(cvec, ada_w, ada_b, 0)
    mod1 = _ada_mod(cvec, ada_w, ada_b, 1)

    h = _norm_mod(xs, norm1_g[0], mod0, 0, 1, n_ctx, seq)
    w_in = ab_w_in
    q_gain = (na_q_g[0] * (NA_HEAD_DIM ** -0.5)).reshape(1, NA_HEAD_DIM)
    k_gain = na_k_g[0].reshape(1, NA_HEAD_DIM)

    def in_proj(col0, n_cols, epilogue, extras, name):
        tn_in = 2 * tn
        (o,) = _matmul(h, w_in, w_lead=(0,), col0=col0, n_cols=n_cols, tm=tm, tn=tn_in, epilogue=epilogue,
                       extras=extras,
                       outs=[(jax.ShapeDtypeStruct((n_all, n_cols), BF16), (tm, tn_in), lambda j, i: (i, j))],
                       name=name)
        return o

    gain_spec = lambda g: [(g, (1, NA_HEAD_DIM), lambda j, i: (0, 0))]
    q = in_proj(0, NA_WIDTH, _ep_head_rms, gain_spec(q_gain), "ab_in_q")
    k = in_proj(NA_WIDTH, NA_WIDTH, _ep_head_rms, gain_spec(k_gain), "ab_in_k")
    v = in_proj(2 * NA_WIDTH, NA_WIDTH, _ep_cast, [], "ab_in_v")
    uv = in_proj(3 * NA_WIDTH, 2 * SG_WIDTH, _ep_gelu, [], "ab_in_uv")

    a_lat = _na_attention(q, k, v, na_rel_bias[0], n_batch=n_batch, seq=seq, t_ctx=t_ctx)
    a_ctx = _attention(q, k, v, n_batch=n_batch, n_heads=NA_HEADS, dq=NA_HEAD_DIM, dv=NA_HEAD_DIM,
                       tq=t_ctx, n_q_tiles=1, q_blk0=0, t_k=t_ctx, k_blk0=0, name="ctx_attention")
    gated = _spatial_gating(uv, sg_norm_g[0], sg_norm_b[0], sg_w[0], sg_b[0])

    def out_proj(inp, n_rows, w, resid, resid_row0, mod, n_ctx_rows, name):
        mr = functools.partial(_mod_row, tm=tm, n_ctx_rows=n_ctx_rows, seq=seq)
        (o,) = _matmul(inp, w, w_lead=(0,), n_cols=d, tm=tm, tn=tn, epilogue=_ep_residual,
                       extras=[(resid, (tm, tn), lambda j, i: (resid_row0 // tm + i, j)),
                               (mod, (None, 1, tn), lambda j, i: (mr(i), 0, 2 * d // tn + j))],
                       outs=[(jax.ShapeDtypeStruct((n_rows, d), F32), (tm, tn), lambda j, i: (i, j))],
                       name=name)
        return o

    xs = out_proj([(a_ctx, a_lat), gated], n_all, ab_w_out, xs, 0, mod0, n_ctx, "ab_out")
    h2, logits = _norm_mod(xs, norm2_g[0], mod0, 3, 4, n_ctx, seq, router=(moe_w_router[0], moe_b_router[0]))
    xs, h = _moe_block(xs, h2, logits, mod0, 0, n_ctx, seq, moe_w_gu, moe_b_gu, moe_w_down, moe_b_down,
                       next_norm=(norm1_g[1], mod1))

    q, k, v = _mla_project(h, n_ctx, seq, tm, mla_w_in, mla_q_norm_g, mla_kv_norm_g, mla_w_uq, mla_w_ukv,
                           mla_q_g, mla_k_g)

    tq = seq
    attn = _attention(q, k, v, n_batch=n_batch, n_heads=MLA_HEADS, dq=QK_PAD, dv=V_DIM, tq=tq,
                      n_q_tiles=seq // tq, q_blk0=0, t_k=seq, k_blk0=n_ctx // seq, t_ctx=t_ctx, n_sub=8,
                      name="mla_attention")
    x_lat = out_proj(attn, n_lat, mla_w_out, xs, n_ctx, mod1, 0, "mla_out")
    h2, logits = _norm_mod(x_lat, norm2_g[1], mod1, 3, 4, 0, seq, router=(moe_w_router[1], moe_b_router[1]))
    x_lat = _moe_block(x_lat, h2, logits, mod1, 1, 0, seq, moe_w_gu, moe_b_gu, moe_w_down, moe_b_down)
    return x_lat.reshape(n_batch, seq, d)
```
